```python
import math
import jax, jax.numpy as jnp
from jax import lax
import numpy as np

D_MODEL = 2048
BATCH = 8
SEQ = 4096
DEPTH = 2

CTX_LEN = 256
GRID_W = 64
N_MIXERS = 2
N_FOURIER_LAYERS = (DEPTH + 1) // 2
N_ATTN_LAYERS = DEPTH // 2
FOURIER_GROUPS = 8
FOURIER_GROUP_DIM = D_MODEL // FOURIER_GROUPS
HEAD_DIM = 128
N_HEADS = D_MODEL // HEAD_DIM
N_KV_HEADS = 4
GQA_GROUP = N_HEADS // N_KV_HEADS
Q_DIM = N_HEADS * HEAD_DIM
KV_DIM = N_KV_HEADS * HEAD_DIM
ROPE_FREQS = HEAD_DIM // 4
ROPE_THETA = 10000.0
Q_BLOCK = 128
D_FF = 5632
N_MOD = 9
EPS = 1e-6

kernel_name = "hybrid_fourier_gqa_macaron_dit"


def rmsnorm(x, g):
    xf = x.astype(jnp.float32)
    y = xf * lax.rsqrt(jnp.mean(xf * xf, axis=-1, keepdims=True) + EPS)
    return (y * g.astype(jnp.float32)).astype(x.dtype)


def modulate(x, g, shift, scale):
    return rmsnorm(x, g) * (1.0 + scale) + shift


def swiglu(h, w_in, w_out):
    gate, up = jnp.split(h @ w_in, 2, axis=-1)
    return (jax.nn.silu(gate) * up) @ w_out


def axial_rope_tables(n_tokens):
    rows = n_tokens // GRID_W
    row = jnp.repeat(jnp.arange(rows), GRID_W).astype(jnp.float32)
    col = jnp.tile(jnp.arange(GRID_W), rows).astype(jnp.float32)
    inv_freq = ROPE_THETA ** (-jnp.arange(ROPE_FREQS, dtype=jnp.float32) / ROPE_FREQS)
    a_r = row[:, None] * inv_freq
    a_c = col[:, None] * inv_freq
    ang = jnp.concatenate([a_r, a_r, a_c, a_c], axis=-1)
    return jnp.cos(ang), jnp.sin(ang)


def apply_axial_rope(x, cos, sin):
    xs = x.astype(jnp.float32).reshape(*x.shape[:-1], 2, 2, ROPE_FREQS)
    x1, x2 = xs[..., 0, :], xs[..., 1, :]
    rot = jnp.stack([-x2, x1], axis=-2).reshape(x.shape)
    return (x.astype(jnp.float32) * cos[:, None, :] + rot * sin[:, None, :]).astype(x.dtype)


def fourier_mix(h, w_out):
    b, n, _ = h.shape
    hg = h.astype(jnp.float32).reshape(b, n, FOURIER_GROUPS, FOURIER_GROUP_DIM)
    f = jnp.fft.fft2(hg, axes=(1, 3), norm="ortho").real
    return f.reshape(b, n, D_MODEL).astype(h.dtype) @ w_out


def gqa_mix(h_lat, h_ctx, w_qkv, q_g, k_g, w_o, cos, sin, need_ctx):
    b, s, _ = h_lat.shape
    l = h_ctx.shape[1]
    scale = 1.0 / math.sqrt(HEAD_DIM)

    q_l, k_l, v_l = jnp.split(h_lat @ w_qkv, [Q_DIM, Q_DIM + KV_DIM], axis=-1)
    q_l = apply_axial_rope(rmsnorm(q_l.reshape(b, s, N_HEADS, HEAD_DIM), q_g), cos, sin)
    k_l = apply_axial_rope(rmsnorm(k_l.reshape(b, s, N_KV_HEADS, HEAD_DIM), k_g), cos, sin)
    v_l = v_l.reshape(b, s, N_KV_HEADS, HEAD_DIM)

    if need_ctx:
        q_c, k_c, v_c = jnp.split(h_ctx @ w_qkv, [Q_DIM, Q_DIM + KV_DIM], axis=-1)
    else:
        k_c, v_c = jnp.split(h_ctx @ w_qkv[:, Q_DIM:], [KV_DIM], axis=-1)
    k_c = rmsnorm(k_c.reshape(b, l, N_KV_HEADS, HEAD_DIM), k_g)
    v_c = v_c.reshape(b, l, N_KV_HEADS, HEAD_DIM)

    k_all = jnp.concatenate([k_c, k_l], axis=1).transpose(0, 2, 1, 3)
    v_all = jnp.concatenate([v_c, v_l], axis=1).transpose(0, 2, 1, 3)

    n_blk = s // Q_BLOCK
    qb = q_l.reshape(b, n_blk, Q_BLOCK, N_KV_HEADS, GQA_GROUP, HEAD_DIM).transpose(1, 0, 3, 4, 2, 5)

    def attend(q_blk):
        sc = jnp.einsum('bkgqd,bkld->bkgql', q_blk, k_all).astype(jnp.float32) * scale
        p = jax.nn.softmax(sc, axis=-1).astype(v_all.dtype)
        return jnp.einsum('bkgql,bkld->bkgqd', p, v_all)

    o = lax.map(attend, qb)
    o = o.transpose(1, 0, 4, 2, 3, 5).reshape(b, s, Q_DIM)
    out_lat = o @ w_o

    out_ctx = None
    if need_ctx:
        qc = rmsnorm(q_c.reshape(b, l, N_HEADS, HEAD_DIM), q_g)
        qc = qc.reshape(b, l, N_KV_HEADS, GQA_GROUP, HEAD_DIM).transpose(0, 2, 3, 1, 4)
        kc = k_c.transpose(0, 2, 1, 3)
        vc = v_c.transpose(0, 2, 1, 3)
        sc = jnp.einsum('bkgqd,bkld->bkgql', qc, kc).astype(jnp.float32) * scale
        p = jax.nn.softmax(sc, axis=-1).astype(vc.dtype)
        oc = jnp.einsum('bkgql,bkld->bkgqd', p, vc).transpose(0, 3, 1, 2, 4).reshape(b, l, Q_DIM)
        out_ctx = oc @ w_o
    return out_lat, out_ctx


def setup_inputs(seed: int = 0) -> dict:
    key = jax.random.key(seed)
    ks = jax.random.split(key, 14)
    f32 = jnp.float32
    d = D_MODEL
    nrm = lambda k, shape, s: jax.random.normal(k, shape, f32) * s
    return {
        "x": nrm(ks[0], (BATCH, SEQ, d), 1.0),
        "c": nrm(ks[1], (BATCH, d), 1.0),
        "ctx": nrm(ks[2], (BATCH, CTX_LEN, d), 1.0),
        "c_ctx": nrm(ks[3], (d,), 1.0),
        "w_ada": nrm(ks[4], (DEPTH, d, N_MOD * d), 0.5 * d ** -0.5),
        "b_ada": nrm(ks[5], (DEPTH, N_MOD * d), 0.01),
        "norm_g": 1.0 + nrm(ks[6], (DEPTH, 3, d), 0.01),
        "w_ffn_in": nrm(ks[7], (DEPTH, 2, d, 2 * D_FF), d ** -0.5),
        "w_ffn_out": nrm(ks[8], (DEPTH, 2, D_FF, d), D_FF ** -0.5),
        "w_fourier_out": nrm(ks[9], (N_FOURIER_LAYERS, d, d), d ** -0.5),
        "w_qkv": nrm(ks[10], (N_ATTN_LAYERS, d, Q_DIM + 2 * KV_DIM), d ** -0.5),
        "q_norm_g": 1.0 + nrm(ks[11], (N_ATTN_LAYERS, HEAD_DIM), 0.01),
        "k_norm_g": 1.0 + nrm(ks[12], (N_ATTN_LAYERS, HEAD_DIM), 0.01),
        "w_attn_out": nrm(ks[13], (N_ATTN_LAYERS, Q_DIM, d), Q_DIM ** -0.5),
    }


def reference(x, c, ctx, c_ctx, w_ada, b_ada, norm_g, w_ffn_in, w_ffn_out,
              w_fourier_out, w_qkv, q_norm_g, k_norm_g, w_attn_out):
    b, n, d = x.shape
    cos, sin = axial_rope_tables(n)
    s_c = jax.nn.silu(c)
    s_cc = jax.nn.silu(c_ctx)

    for i in range(DEPTH):
        last = i == DEPTH - 1
        mixer = i % N_MIXERS
        j = i // N_MIXERS
        m_l = (s_c @ w_ada[i] + b_ada[i]).reshape(b, N_MOD, d)[:, :, None, :]
        m_c = (s_cc @ w_ada[i] + b_ada[i]).reshape(N_MOD, d)
        mod_l = [m_l[:, k] for k in range(N_MOD)]
        mod_c = [m_c[k] for k in range(N_MOD)]
        ctx_feeds_latent = (not last) or mixer == 1

        x = x + 0.5 * mod_l[2] * swiglu(modulate(x, norm_g[i, 0], mod_l[0], mod_l[1]),
                                        w_ffn_in[i, 0], w_ffn_out[i, 0])
        if ctx_feeds_latent:
            ctx = ctx + 0.5 * mod_c[2] * swiglu(modulate(ctx, norm_g[i, 0], mod_c[0], mod_c[1]),
                                                w_ffn_in[i, 0], w_ffn_out[i, 0])

        h_l = modulate(x, norm_g[i, 1], mod_l[3], mod_l[4])
        if mixer == 0:
            x = x + mod_l[5] * fourier_mix(h_l, w_fourier_out[j])
            if not last:
                h_c = modulate(ctx, norm_g[i, 1], mod_c[3], mod_c[4])
                ctx = ctx + mod_c[5] * fourier_mix(h_c, w_fourier_out[j])
        else:
            h_c = modulate(ctx, norm_g[i, 1], mod_c[3], mod_c[4])
            o_l, o_c = gqa_mix(h_l, h_c, w_qkv[j], q_norm_g[j], k_norm_g[j], w_attn_out[j],
                               cos, sin, need_ctx=not last)
            x = x + mod_l[5] * o_l
            if not last:
                ctx = ctx + mod_c[5] * o_c

        x = x + 0.5 * mod_l[8] * swiglu(modulate(x, norm_g[i, 2], mod_l[6], mod_l[7]),
                                        w_ffn_in[i, 1], w_ffn_out[i, 1])
        if not last:
            ctx = ctx + 0.5 * mod_c[8] * swiglu(modulate(ctx, norm_g[i, 2], mod_c[6], mod_c[7]),
                                                w_ffn_in[i, 1], w_ffn_out[i, 1])
    return x
```

```python
import functools
import math

import jax
import jax.numpy as jnp
from jax import lax
from jax.experimental import pallas as pl
from jax.experimental.pallas import tpu as pltpu

F32 = jnp.float32
BF16 = jnp.bfloat16

GRID_W = 64
FOURIER_GROUPS = 8
HEAD_DIM = 128
N_KV_HEADS = 4
ROPE_FREQS = HEAD_DIM // 4
ROPE_THETA = 10000.0
N_MOD = 9
EPS = 1e-6
MOD_ROWS = 16

VMEM_LIMIT_BYTES = 56 * 1024 * 1024


def _params(*sem):
    return pltpu.CompilerParams(dimension_semantics=sem, vmem_limit_bytes=VMEM_LIMIT_BYTES)


def _modulate(x, g, shift, scale):
    ms = jnp.mean(x * x, axis=-1, keepdims=True)
    return (x * lax.rsqrt(ms + EPS)) * g * (1.0 + scale) + shift


def _ada_kernel(s_ref, w_ref, b_ref, o_ref):
    w = w_ref[0].astype(BF16)
    o_ref[0] = jnp.dot(s_ref[...], w, preferred_element_type=F32) + b_ref[0]


def _ada(s16, w_ada, b_ada, tn=1024):
    depth, d, n = w_ada.shape
    return pl.pallas_call(
        _ada_kernel,
        grid=(depth, n // tn),
        in_specs=[
            pl.BlockSpec((MOD_ROWS, d), lambda l, j: (0, 0)),
            pl.BlockSpec((1, d, tn), lambda l, j: (l, 0, j)),
            pl.BlockSpec((1, 1, tn), lambda l, j: (l, 0, j)),
        ],
        out_specs=pl.BlockSpec((1, MOD_ROWS, tn), lambda l, j: (l, 0, j)),
        out_shape=jax.ShapeDtypeStruct((depth, MOD_ROWS, n), F32),
        compiler_params=_params("parallel", "parallel"),
        name="ada",
    )(s16, w_ada, b_ada.reshape(depth, 1, n))


class _Mods:
    def __init__(self, table, layer, rows_per_mod, ctx):
        self.table, self.layer, self.rows_per_mod, self.ctx = table, layer, rows_per_mod, ctx

    def tile(self, tm, rows):
        tm = min(tm, rows if self.ctx else self.rows_per_mod)
        assert rows % tm == 0 and (self.ctx or self.rows_per_mod % tm == 0)
        return tm

    def spec(self, k, tm, d):
        layer, rpm, ctx = self.layer, self.rows_per_mod, self.ctx

        def index(i, *_):
            row = (MOD_ROWS - 8) if ctx else (i * tm) // rpm
            return ((layer * MOD_ROWS + row) * N_MOD + k, 0, 0)

        return pl.BlockSpec((1, 1, d), index)


def _ffn_kernel(x_ref, g_ref, sh_ref, sc_ref, gt_ref, wg_ref, wu_ref, wo_ref, o_ref, h_ref, acc_ref):
    k = pl.program_id(1)

    @pl.when(k == 0)
    def _():
        h_ref[...] = _modulate(x_ref[...], g_ref[...], sh_ref[0], sc_ref[0]).astype(BF16)

    h = h_ref[...]
    gate = jnp.dot(h, wg_ref[...], preferred_element_type=F32)
    up = jnp.dot(h, wu_ref[...], preferred_element_type=F32)
    act = (jax.nn.silu(gate) * up).astype(BF16)
    part = jnp.dot(act, wo_ref[...], preferred_element_type=F32)

    @pl.when(k == 0)
    def _():
        acc_ref[...] = part

    @pl.when(k > 0)
    def _():
        acc_ref[...] += part

    @pl.when(k == pl.num_programs(1) - 1)
    def _():
        o_ref[...] = x_ref[...] + 0.5 * gt_ref[0] * acc_ref[...]


def _ffn(x2, g, mods, k0, w_in, w_out, tm=512, tf=512):
    rows, d = x2.shape
    f = w_out.shape[0]
    tm, tf = mods.tile(tm, rows), min(tf, f)
    nk = f // tf
    return pl.pallas_call(
        _ffn_kernel,
        grid=(rows // tm, nk),
        in_specs=[
            pl.BlockSpec((tm, d), lambda i, k: (i, 0)),
            pl.BlockSpec((1, d), lambda i, k: (0, 0)),
            mods.spec(k0, tm, d), mods.spec(k0 + 1, tm, d), mods.spec(k0 + 2, tm, d),
            pl.BlockSpec((d, tf), lambda i, k: (0, k)),
            pl.BlockSpec((d, tf), lambda i, k: (0, nk + k)),
            pl.BlockSpec((tf, d), lambda i, k: (k, 0)),
        ],
        out_specs=pl.BlockSpec((tm, d), lambda i, k: (i, 0)),
        out_shape=jax.ShapeDtypeStruct((rows, d), F32),
        scratch_shapes=[pltpu.VMEM((tm, d), BF16), pltpu.VMEM((tm, d), F32)],
        compiler_params=_params("parallel", "arbitrary"),
        name="ffn",
    )(x2, g.reshape(1, d), mods.table, mods.table, mods.table, w_in, w_in, w_out)


def _mod_kernel(x_ref, g_ref, sh_ref, sc_ref, o_ref):
    o_ref[...] = _modulate(x_ref[...], g_ref[...], sh_ref[0], sc_ref[0]).astype(o_ref.dtype)


def _modulated(x2, g, mods, k0, tm=512):
    rows, d = x2.shape
    tm = mods.tile(tm, rows)
    return pl.pallas_call(
        _mod_kernel,
        grid=(rows // tm,),
        in_specs=[
            pl.BlockSpec((tm, d), lambda i: (i, 0)),
            pl.BlockSpec((1, d), lambda i: (0, 0)),
            mods.spec(k0, tm, d), mods.spec(k0 + 1, tm, d),
        ],
        out_specs=pl.BlockSpec((tm, d), lambda i: (i, 0)),
        out_shape=jax.ShapeDtypeStruct((rows, d), BF16),
        compiler_params=_params("parallel"),
        name="modulate",
    )(x2, g.reshape(1, d), mods.table, mods.table)


def _dft_tables(n):
    idx = jnp.arange(n, dtype=jnp.int32)
    phase = (idx[:, None] * idx[None, :]) % n
    ang = phase.astype(F32) * (2.0 * math.pi / n)
    s = 1.0 / math.sqrt(n)
    return (jnp.cos(ang) * s).astype(BF16), (jnp.sin(ang) * s).astype(BF16)


def _pos_dft_kernel(c_ref, s_ref, h_ref, zr_ref, zi_ref):
    h = h_ref[0]
    zr_ref[0] = jnp.dot(c_ref[...], h, preferred_element_type=F32).astype(BF16)
    zi_ref[0] = jnp.dot(s_ref[...], h, preferred_element_type=F32).astype(BF16)


def _pos_dft(h3, cos_n, sin_n, tk=256):
    b, n, d = h3.shape
    tk = min(tk, n)
    out = jax.ShapeDtypeStruct((b, n, d), BF16)
    return pl.pallas_call(
        _pos_dft_kernel,
        grid=(b, n // tk),
        in_specs=[
            pl.BlockSpec((tk, n), lambda bi, i: (i, 0)),
            pl.BlockSpec((tk, n), lambda bi, i: (i, 0)),
            pl.BlockSpec((1, n, d), lambda bi, i: (bi, 0, 0), pipeline_mode=pl.Buffered(1)),
        ],
        out_specs=[pl.BlockSpec((1, tk, d), lambda bi, i: (bi, i, 0))] * 2,
        out_shape=[out, out],
        compiler_params=_params("parallel", "parallel"),
        name="pos_dft",
    )(cos_n, sin_n, h3)


def _chan_kernel(x_ref, zr_ref, zi_ref, tab_ref, w_ref, gt_ref, o_ref, f_ref):
    gd = tab_ref.shape[1]
    tab = tab_ref[...]
    for grp in range(FOURIER_GROUPS):
        cols = slice(grp * gd, (grp + 1) * gd)
        z = jnp.concatenate([zr_ref[:, cols], zi_ref[:, cols]], axis=-1)
        f_ref[:, cols] = jnp.dot(z, tab, preferred_element_type=F32).astype(BF16)
    y = jnp.dot(f_ref[...], w_ref[...], preferred_element_type=F32)
    o_ref[...] = x_ref[...] + gt_ref[0] * y


def _chan_mix(x2, zr2, zi2, tab, w_out, mods, k_gate, tm=512):
    rows, d = x2.shape
    tm = mods.tile(tm, rows)
    return pl.pallas_call(
        _chan_kernel,
        grid=(rows // tm,),
        in_specs=[
            pl.BlockSpec((tm, d), lambda i: (i, 0)),
            pl.BlockSpec((tm, d), lambda i: (i, 0)),
            pl.BlockSpec((tm, d), lambda i: (i, 0)),
            pl.BlockSpec(tab.shape, lambda i: (0, 0)),
            pl.BlockSpec((d, d), lambda i: (0, 0)),
            mods.spec(k_gate, tm, d),
        ],
        out_specs=pl.BlockSpec((tm, d), lambda i: (i, 0)),
        out_shape=jax.ShapeDtypeStruct((rows, d), F32),
        scratch_shapes=[pltpu.VMEM((tm, d), BF16)],
        compiler_params=_params("parallel"),
        name="chan_mix",
    )(x2, zr2, zi2, tab, w_out, mods.table)


def _fourier_layer(x3, g, mods, w_out, chan_tab):
    b, n, d = x3.shape
    x2 = x3.reshape(b * n, d)
    h = _modulated(x2, g, mods, 3).reshape(b, n, d)
    cos_n, sin_n = _dft_tables(n)
    zr, zi = _pos_dft(h, cos_n, sin_n)
    out = _chan_mix(x2, zr.reshape(b * n, d), zi.reshape(b * n, d), chan_tab, w_out, mods, 5)
    return out.reshape(b, n, d)


def _rope_tables(n):
    pos = jnp.arange(n, dtype=jnp.int32)
    row = (pos // GRID_W).astype(F32)
    col = (pos % GRID_W).astype(F32)
    inv_freq = ROPE_THETA ** (-jnp.arange(ROPE_FREQS, dtype=F32) / ROPE_FREQS)
    a_r = row[:, None] * inv_freq
    a_c = col[:, None] * inv_freq
    ang = jnp.concatenate([a_r, a_r, a_c, a_c], axis=-1)
    cos, sin = jnp.cos(ang), jnp.sin(ang)
    first_half = (jnp.arange(HEAD_DIM) % (2 * ROPE_FREQS)) < ROPE_FREQS
    sin_from_upper = jnp.where(first_half[None, :], -sin, 0.0)
    sin_from_lower = jnp.where(first_half[None, :], 0.0, sin)
    return cos, sin_from_upper, sin_from_lower


def _head_norm(t, g):
    ms = jnp.mean(t * t, axis=-1, keepdims=True)
    return t * lax.rsqrt(ms + EPS) * g


def _rope(t, cos, s_up, s_lo):
    up = pltpu.roll(t, HEAD_DIM - ROPE_FREQS, 1)
    lo = pltpu.roll(t, ROPE_FREQS, 1)
    return t * cos + up * s_up + lo * s_lo


def _qkv_kernel(x_ref, g_ref, sh_ref, sc_ref, w_ref, qg_ref, kg_ref, *refs,
                n_q_heads, use_rope, q_scale):
    if use_rope:
        cos, s_up, s_lo = (r[...] for r in refs[:3])
        refs = refs[3:]
    q_ref = refs[0] if n_q_heads else None
    k_ref, v_ref = refs[-2:]
    h = _modulate(x_ref[...], g_ref[...], sh_ref[0], sc_ref[0]).astype(BF16)
    qkv = jnp.dot(h, w_ref[...], preferred_element_type=F32)
    for hd in range(n_q_heads + N_KV_HEADS):
        t = qkv[:, hd * HEAD_DIM:(hd + 1) * HEAD_DIM]
        is_q = hd < n_q_heads
        t = _head_norm(t, qg_ref[...] if is_q else kg_ref[...])
        if use_rope:
            t = _rope(t, cos, s_up, s_lo)
        if is_q:
            q_ref[:, hd * HEAD_DIM:(hd + 1) * HEAD_DIM] = (t * q_scale).astype(BF16)
        else:
            kh = hd - n_q_heads
            k_ref[:, kh * HEAD_DIM:(kh + 1) * HEAD_DIM] = t.astype(BF16)
    v0 = (n_q_heads + N_KV_HEADS) * HEAD_DIM
    v_ref[...] = qkv[:, v0:].astype(BF16)


def _qkv_proj(x2, g, mods, k0, w, q_g, k_g, rope, n_q_heads, seq, tm=512):
    rows, d = x2.shape
    tm = mods.tile(tm, rows)
    kv_dim = N_KV_HEADS * HEAD_DIM
    q_dim = n_q_heads * HEAD_DIM
    use_rope = rope is not None
    rope = rope if use_rope else ()
    tiles_per_seq = seq // tm
    kern = functools.partial(_qkv_kernel, n_q_heads=n_q_heads, use_rope=use_rope,
                             q_scale=1.0 / math.sqrt(HEAD_DIM))
    head_vec = pl.BlockSpec((1, HEAD_DIM), lambda i: (0, 0))
    rope_spec = pl.BlockSpec((tm, HEAD_DIM), lambda i: (i % tiles_per_seq, 0))
    out_widths = ([q_dim] if n_q_heads else []) + [kv_dim, kv_dim]
    return pl.pallas_call(
        kern,
        grid=(rows // tm,),
        in_specs=[
            pl.BlockSpec((tm, d), lambda i: (i, 0)),
            pl.BlockSpec((1, d), lambda i: (0, 0)),
            mods.spec(k0, tm, d), mods.spec(k0 + 1, tm, d),
            pl.BlockSpec(w.shape, lambda i: (0, 0)),
            head_vec, head_vec,
        ] + [rope_spec] * len(rope),
        out_specs=[pl.BlockSpec((tm, wd), lambda i: (i, 0)) for wd in out_widths],
        out_shape=[jax.ShapeDtypeStruct((rows, wd), BF16) for wd in out_widths],
        compiler_params=_params("parallel"),
        name="qkv_proj",
    )(x2, g.reshape(1, d), mods.table, mods.table, w,
      q_g.reshape(1, HEAD_DIM), k_g.reshape(1, HEAD_DIM), *rope)


def _attn_kernel(q_ref, kc_ref, vc_ref, kl_ref, vl_ref, o_ref, *, group):
    nt = (((1,), (1,)), ((), ()))
    kc, vc, kl, vl = kc_ref[0], vc_ref[0], kl_ref[0], vl_ref[0]
    for gi in range(group):
        cols = slice(gi * HEAD_DIM, (gi + 1) * HEAD_DIM)
        q = q_ref[0, :, cols]
        s_c = lax.dot_general(q, kc, nt, preferred_element_type=F32)
        s_l = lax.dot_general(q, kl, nt, preferred_element_type=F32)
        m = jnp.maximum(jnp.max(s_c, axis=-1, keepdims=True), jnp.max(s_l, axis=-1, keepdims=True))
        p_c = jnp.exp(s_c - m)
        p_l = jnp.exp(s_l - m)
        denom = jnp.sum(p_c, axis=-1, keepdims=True) + jnp.sum(p_l, axis=-1, keepdims=True)
        o = (jnp.dot(p_c.astype(BF16), vc, preferred_element_type=F32)
             + jnp.dot(p_l.astype(BF16), vl, preferred_element_type=F32))
        o_ref[0, :, cols] = (o / denom).astype(BF16)


def _attention(q3, kc3, vc3, kl3, vl3, tq=256):
    b, s, qd = q3.shape
    tq = min(tq, s)
    l = kc3.shape[1]
    group = qd // (N_KV_HEADS * HEAD_DIM)
    gw = group * HEAD_DIM
    kv_spec = lambda n: pl.BlockSpec((1, n, HEAD_DIM), lambda bi, kh, i: (bi, 0, kh))
    return pl.pallas_call(
        functools.partial(_attn_kernel, group=group),
        grid=(b, N_KV_HEADS, s // tq),
        in_specs=[
            pl.BlockSpec((1, tq, gw), lambda bi, kh, i: (bi, i, kh)),
            kv_spec(l), kv_spec(l), kv_spec(s), kv_spec(s),
        ],
        out_specs=pl.BlockSpec((1, tq, gw), lambda bi, kh, i: (bi, i, kh)),
        out_shape=jax.ShapeDtypeStruct((b, s, qd), BF16),
        compiler_params=_params("parallel", "parallel", "parallel"),
        name="attention",
    )(q3, kc3, vc3, kl3, vl3)


def _oproj_kernel(x_ref, a_ref, w_ref, gt_ref, o_ref):
    y = jnp.dot(a_ref[...], w_ref[...], preferred_element_type=F32)
    o_ref[...] = x_ref[...] + gt_ref[0] * y


def _out_proj(x2, a2, w, mods, k_gate, tm=512):
    rows, d = x2.shape
    tm = mods.tile(tm, rows)
    return pl.pallas_call(
        _oproj_kernel,
        grid=(rows // tm,),
        in_specs=[
            pl.BlockSpec((tm, d), lambda i: (i, 0)),
            pl.BlockSpec((tm, a2.shape[1]), lambda i: (i, 0)),
            pl.BlockSpec(w.shape, lambda i: (0, 0)),
            mods.spec(k_gate, tm, d),
        ],
        out_specs=pl.BlockSpec((tm, d), lambda i: (i, 0)),
        out_shape=jax.ShapeDtypeStruct((rows, d), F32),
        compiler_params=_params("parallel"),
        name="out_proj",
    )(x2, a2, w, mods.table)


def kernel(x, c, ctx, c_ctx, w_ada, b_ada, norm_g, w_ffn_in, w_ffn_out, w_fourier_out, w_qkv,
           q_norm_g, k_norm_g, w_attn_out):
    b, n, d = x.shape
    l = ctx.shape[1]
    depth = w_ada.shape[0]
    assert depth == 2 and b <= 8, "layer 0 = Fourier mixer, layer 1 = attention mixer"
    q_dim = w_attn_out.shape[1]
    n_heads = q_dim // HEAD_DIM

    s16 = jnp.zeros((MOD_ROWS, d), F32).at[:b].set(jax.nn.silu(c)).at[8].set(jax.nn.silu(c_ctx))
    table = _ada(s16.astype(BF16), w_ada, b_ada).reshape(depth * MOD_ROWS * N_MOD, 1, d)

    w_in = w_ffn_in.astype(BF16)
    w_out = w_ffn_out.astype(BF16)
    w_four = w_fourier_out.astype(BF16)
    w_qkv_b = w_qkv.astype(BF16)
    w_o = w_attn_out.astype(BF16)

    gd = d // FOURIER_GROUPS
    cos_c, sin_c = _dft_tables(gd)
    chan_tab = jnp.concatenate([cos_c, -sin_c], axis=0)

    xl = x.reshape(b * n, d)
    xc = ctx.reshape(b * l, d)

    ml = _Mods(table, 0, n, ctx=False)
    mc = _Mods(table, 0, l, ctx=True)
    xl = _ffn(xl, norm_g[0, 0], ml, 0, w_in[0, 0], w_out[0, 0])
    xc = _ffn(xc, norm_g[0, 0], mc, 0, w_in[0, 0], w_out[0, 0])
    xl = _fourier_layer(xl.reshape(b, n, d), norm_g[0, 1], ml, w_four[0], chan_tab).reshape(b * n, d)
    xc = _fourier_layer(xc.reshape(b, l, d), norm_g[0, 1], mc, w_four[0], chan_tab).reshape(b * l, d)
    xl = _ffn(xl, norm_g[0, 2], ml, 6, w_in[0, 1], w_out[0, 1])
    xc = _ffn(xc, norm_g[0, 2], mc, 6, w_in[0, 1], w_out[0, 1])

    ml = _Mods(table, 1, n, ctx=False)
    mc = _Mods(table, 1, l, ctx=True)
    xl = _ffn(xl, norm_g[1, 0], ml, 0, w_in[1, 0], w_out[1, 0])
    xc = _ffn(xc, norm_g[1, 0], mc, 0, w_in[1, 0], w_out[1, 0])
    rope = _rope_tables(n)
    q, kl, vl = _qkv_proj(xl, norm_g[1, 1], ml, 3, w_qkv_b[0], q_norm_g[0], k_norm_g[0], rope,
                          n_heads, n)
    kc, vc = _qkv_proj(xc, norm_g[1, 1], mc, 3, w_qkv_b[0][:, q_dim:], q_norm_g[0], k_norm_g[0],
                       None, 0, l)
    kv_dim = N_KV_HEADS * HEAD_DIM
    o = _attention(q.reshape(b, n, q_dim), kc.reshape(b, l, kv_dim), vc.reshape(b, l, kv_dim),
                   kl.reshape(b, n, kv_dim), vl.reshape(b, n, kv_dim))
    xl = _out_proj(xl, o.reshape(b * n, q_dim), w_o[0], ml, 5)
    xl = _ffn(xl, norm_g[1, 2], ml, 6, w_in[1, 1], w_out[1, 1])
    return xl.reshape(b, n, d)
```

```python
import functools
import math

import jax
import jax.numpy as jnp
from jax import lax
from jax.experimental import pallas as pl
from jax.experimental.pallas import tpu as pltpu

F32 = jnp.float32
BF16 = jnp.bfloat16

GRID_W = 64
FOURIER_GROUPS = 8
HEAD_DIM = 128
N_KV_HEADS = 4
ROPE_FREQS = HEAD_DIM // 4
ROPE_THETA = 10000.0
N_MOD = 9
EPS = 1e-6
MOD_ROWS = 16
DFT_TABLE_SPLIT = 64

VMEM_LIMIT_BYTES = 56 * 1024 * 1024


def _params(*sem):
    return pltpu.CompilerParams(dimension_semantics=sem, vmem_limit_bytes=VMEM_LIMIT_BYTES)


def _modulate(x, g, shift, scale):
    ms = jnp.mean(x * x, axis=-1, keepdims=True)
    return (x * lax.rsqrt(ms + EPS)) * g * (1.0 + scale) + shift


def _ada_kernel(s_ref, w_ref, b_ref, o_ref):
    w = w_ref[0].astype(BF16)
    o_ref[0] = jnp.dot(s_ref[...], w, preferred_element_type=F32) + b_ref[0]


def _ada(s16, w_ada, b_ada, tn=1024):
    depth, d, n = w_ada.shape
    return pl.pallas_call(
        _ada_kernel,
        grid=(depth, n // tn),
        in_specs=[
            pl.BlockSpec((MOD_ROWS, d), lambda l, j: (0, 0)),
            pl.BlockSpec((1, d, tn), lambda l, j: (l, 0, j)),
            pl.BlockSpec((1, 1, tn), lambda l, j: (l, 0, j)),
        ],
        out_specs=pl.BlockSpec((1, MOD_ROWS, tn), lambda l, j: (l, 0, j)),
        out_shape=jax.ShapeDtypeStruct((depth, MOD_ROWS, n), F32),
        compiler_params=_params("parallel", "parallel"),
        name="ada",
    )(s16, w_ada, b_ada.reshape(depth, 1, n))


class _Mods:
    def __init__(self, table, layer, rows_per_mod, ctx):
        self.table, self.layer, self.rows_per_mod, self.ctx = table, layer, rows_per_mod, ctx

    def tile(self, tm, rows):
        tm = min(tm, rows if self.ctx else self.rows_per_mod)
        assert rows % tm == 0 and (self.ctx or self.rows_per_mod % tm == 0)
        return tm

    def spec(self, k, tm, d):
        layer, rpm, ctx = self.layer, self.rows_per_mod, self.ctx

        def index(i, *_):
            row = (MOD_ROWS - 8) if ctx else (i * tm) // rpm
            return ((layer * MOD_ROWS + row) * N_MOD + k, 0, 0)

        return pl.BlockSpec((1, 1, d), index)


def _ffn_kernel(x_ref, g_ref, sh_ref, sc_ref, gt_ref, wg_ref, wu_ref, wo_ref, o_ref, h_ref):
    @pl.when(pl.program_id(1) == 0)
    def _():
        x = x_ref[...]
        h_ref[...] = _modulate(x, g_ref[0], sh_ref[0], sc_ref[0]).astype(BF16)
        o_ref[...] = x

    h = h_ref[...]
    gate = jnp.dot(h, wg_ref[...], preferred_element_type=F32)
    up = jnp.dot(h, wu_ref[...], preferred_element_type=F32)
    act = (jax.nn.silu(gate) * up).astype(BF16)
    o_ref[...] += (0.5 * gt_ref[0]) * jnp.dot(act, wo_ref[...], preferred_element_type=F32)


def _ffn(x2, norm_g, gi, mods, k0, w_in, w_out, wi, tm=512, tf=512):
    rows, d = x2.shape
    f = w_out.shape[1]
    tm, tf = mods.tile(tm, rows), min(tf, f)
    nk = f // tf
    return pl.pallas_call(
        _ffn_kernel,
        grid=(rows // tm, nk),
        in_specs=[
            pl.BlockSpec((tm, d), lambda i, k: (i, 0)),
            pl.BlockSpec((1, 1, d), lambda i, k: (gi, 0, 0)),
            mods.spec(k0, tm, d), mods.spec(k0 + 1, tm, d), mods.spec(k0 + 2, tm, d),
            pl.BlockSpec((None, d, tf), lambda i, k: (wi, 0, k)),
            pl.BlockSpec((None, d, tf), lambda i, k: (wi, 0, nk + k)),
            pl.BlockSpec((None, tf, d), lambda i, k: (wi, k, 0)),
        ],
        out_specs=pl.BlockSpec((tm, d), lambda i, k: (i, 0)),
        out_shape=jax.ShapeDtypeStruct((rows, d), F32),
        scratch_shapes=[pltpu.VMEM((tm, d), BF16)],
        compiler_params=_params("parallel", "arbitrary"),
        name="ffn",
    )(x2, norm_g, mods.table, mods.table, mods.table, w_in, w_in, w_out)


def _mod_kernel(x_ref, g_ref, sh_ref, sc_ref, o_ref):
    o_ref[...] = _modulate(x_ref[...], g_ref[0], sh_ref[0], sc_ref[0]).astype(o_ref.dtype)


def _modulated(x2, norm_g, gi, mods, k0, tm=512):
    rows, d = x2.shape
    tm = mods.tile(tm, rows)
    return pl.pallas_call(
        _mod_kernel,
        grid=(rows // tm,),
        in_specs=[
            pl.BlockSpec((tm, d), lambda i: (i, 0)),
            pl.BlockSpec((1, 1, d), lambda i: (gi, 0, 0)),
            mods.spec(k0, tm, d), mods.spec(k0 + 1, tm, d),
        ],
        out_specs=pl.BlockSpec((tm, d), lambda i: (i, 0)),
        out_shape=jax.ShapeDtypeStruct((rows, d), BF16),
        compiler_params=_params("parallel"),
        name="modulate",
    )(x2, norm_g, mods.table, mods.table)


def _dft_tables(n):
    def unit(rows, cols):
        phase = (rows[:, None] * cols[None, :]) % n
        ang = phase.astype(F32) * (2.0 * math.pi / n)
        return jnp.cos(ang), jnp.sin(ang)

    idx = jnp.arange(n, dtype=jnp.int32)
    s = 1.0 / math.sqrt(n)
    m = DFT_TABLE_SPLIT
    if n <= m or n % m:
        c, sn = unit(idx, idx)
    else:
        ca, sa = unit(idx, jnp.arange(n // m, dtype=jnp.int32) * m)
        cb, sb = unit(idx, jnp.arange(m, dtype=jnp.int32))
        c = (ca[:, :, None] * cb[:, None, :] - sa[:, :, None] * sb[:, None, :]).reshape(n, n)
        sn = (sa[:, :, None] * cb[:, None, :] + ca[:, :, None] * sb[:, None, :]).reshape(n, n)
    return (c * s).astype(BF16), (sn * s).astype(BF16)


def _pos_dft_kernel(c_ref, s_ref, h_ref, zr_ref, zi_ref):
    h = h_ref[0]
    zr_ref[0] = jnp.dot(c_ref[...], h, preferred_element_type=F32).astype(BF16)
    zi_ref[0] = jnp.dot(s_ref[...], h, preferred_element_type=F32).astype(BF16)


def _pos_dft(h3, cos_n, sin_n, tk=256):
    b, n, d = h3.shape
    tk = min(tk, n)
    out = jax.ShapeDtypeStruct((b, n, d), BF16)
    return pl.pallas_call(
        _pos_dft_kernel,
        grid=(b, n // tk),
        in_specs=[
            pl.BlockSpec((tk, n), lambda bi, i: (i, 0)),
            pl.BlockSpec((tk, n), lambda bi, i: (i, 0)),
            pl.BlockSpec((1, n, d), lambda bi, i: (bi, 0, 0), pipeline_mode=pl.Buffered(1)),
        ],
        out_specs=[pl.BlockSpec((1, tk, d), lambda bi, i: (bi, i, 0))] * 2,
        out_shape=[out, out],
        compiler_params=_params("parallel", "parallel"),
        name="pos_dft",
    )(cos_n, sin_n, h3)


def _chan_kernel(x_ref, zr_ref, zi_ref, tab_ref, w_ref, gt_ref, o_ref, f_ref):
    gd = tab_ref.shape[1]
    tab = tab_ref[...]
    for grp in range(FOURIER_GROUPS):
        cols = slice(grp * gd, (grp + 1) * gd)
        z = jnp.concatenate([zr_ref[:, cols], zi_ref[:, cols]], axis=-1)
        f_ref[:, cols] = jnp.dot(z, tab, preferred_element_type=F32).astype(BF16)
    y = jnp.dot(f_ref[...], w_ref[...], preferred_element_type=F32)
    o_ref[...] = x_ref[...] + gt_ref[0] * y


def _chan_mix(x2, zr2, zi2, tab, w_out, mods, k_gate, tm=512):
    rows, d = x2.shape
    tm = mods.tile(tm, rows)
    return pl.pallas_call(
        _chan_kernel,
        grid=(rows // tm,),
        in_specs=[
            pl.BlockSpec((tm, d), lambda i: (i, 0)),
            pl.BlockSpec((tm, d), lambda i: (i, 0)),
            pl.BlockSpec((tm, d), lambda i: (i, 0)),
            pl.BlockSpec(tab.shape, lambda i: (0, 0)),
            pl.BlockSpec((d, d), lambda i: (0, 0)),
            mods.spec(k_gate, tm, d),
        ],
        out_specs=pl.BlockSpec((tm, d), lambda i: (i, 0)),
        out_shape=jax.ShapeDtypeStruct((rows, d), F32),
        scratch_shapes=[pltpu.VMEM((tm, d), BF16)],
        compiler_params=_params("parallel"),
        name="chan_mix",
    )(x2, zr2, zi2, tab, w_out, mods.table)


def _fourier_layer(x3, norm_g, gi, mods, w_out, chan_tab):
    b, n, d = x3.shape
    x2 = x3.reshape(b * n, d)
    h = _modulated(x2, norm_g, gi, mods, 3).reshape(b, n, d)
    cos_n, sin_n = _dft_tables(n)
    zr, zi = _pos_dft(h, cos_n, sin_n)
    out = _chan_mix(x2, zr.reshape(b * n, d), zi.reshape(b * n, d), chan_tab, w_out, mods, 5)
    return out.reshape(b, n, d)


def _rope_tables(n):
    pos = jnp.arange(n, dtype=jnp.int32)
    row = (pos // GRID_W).astype(F32)
    col = (pos % GRID_W).astype(F32)
    inv_freq = ROPE_THETA ** (-jnp.arange(ROPE_FREQS, dtype=F32) / ROPE_FREQS)
    a_r = row[:, None] * inv_freq
    a_c = col[:, None] * inv_freq
    ang = jnp.concatenate([a_r, a_r, a_c, a_c], axis=-1)
    cos, sin = jnp.cos(ang), jnp.sin(ang)
    first_half = (jnp.arange(HEAD_DIM) % (2 * ROPE_FREQS)) < ROPE_FREQS
    sin_from_upper = jnp.where(first_half[None, :], -sin, 0.0)
    sin_from_lower = jnp.where(first_half[None, :], 0.0, sin)
    return cos, sin_from_upper, sin_from_lower


def _head_norm(t, g):
    ms = jnp.mean(t * t, axis=-1, keepdims=True)
    return t * lax.rsqrt(ms + EPS) * g


def _rope(t, cos, s_up, s_lo):
    up = pltpu.roll(t, HEAD_DIM - ROPE_FREQS, 1)
    lo = pltpu.roll(t, ROPE_FREQS, 1)
    return t * cos + up * s_up + lo * s_lo


def _qkv_kernel(x_ref, g_ref, sh_ref, sc_ref, w_ref, qg_ref, kg_ref, *refs,
                n_q_heads, use_rope, q_scale):
    if use_rope:
        cos, s_up, s_lo = (r[...] for r in refs[:3])
        refs = refs[3:]
    q_ref = refs[0] if n_q_heads else None
    k_ref, v_ref = refs[-2:]
    h = _modulate(x_ref[...], g_ref[0], sh_ref[0], sc_ref[0]).astype(BF16)
    qkv = jnp.dot(h, w_ref[...], preferred_element_type=F32)
    for hd in range(n_q_heads + N_KV_HEADS):
        t = qkv[:, hd * HEAD_DIM:(hd + 1) * HEAD_DIM]
        is_q = hd < n_q_heads
        t = _head_norm(t, qg_ref[...] if is_q else kg_ref[...])
        if use_rope:
            t = _rope(t, cos, s_up, s_lo)
        if is_q:
            q_ref[:, hd * HEAD_DIM:(hd + 1) * HEAD_DIM] = (t * q_scale).astype(BF16)
        else:
            kh = hd - n_q_heads
            k_ref[:, kh * HEAD_DIM:(kh + 1) * HEAD_DIM] = t.astype(BF16)
    v0 = (n_q_heads + N_KV_HEADS) * HEAD_DIM
    v_ref[...] = qkv[:, v0:].astype(BF16)


def _qkv_proj(x2, norm_g, gi, mods, k0, w, q_g, k_g, rope, n_q_heads, seq, tm=512):
    rows, d = x2.shape
    tm = mods.tile(tm, rows)
    kv_dim = N_KV_HEADS * HEAD_DIM
    q_dim = n_q_heads * HEAD_DIM
    width = q_dim + 2 * kv_dim
    col_block, rem = divmod(w.shape[1] - width, width)
    assert rem == 0
    use_rope = rope is not None
    rope = rope if use_rope else ()
    tiles_per_seq = seq // tm
    kern = functools.partial(_qkv_kernel, n_q_heads=n_q_heads, use_rope=use_rope,
                             q_scale=1.0 / math.sqrt(HEAD_DIM))
    head_vec = pl.BlockSpec((1, HEAD_DIM), lambda i: (0, 0))
    rope_spec = pl.BlockSpec((tm, HEAD_DIM), lambda i: (i % tiles_per_seq, 0))
    out_widths = ([q_dim] if n_q_heads else []) + [kv_dim, kv_dim]
    return pl.pallas_call(
        kern,
        grid=(rows // tm,),
        in_specs=[
            pl.BlockSpec((tm, d), lambda i: (i, 0)),
            pl.BlockSpec((1, 1, d), lambda i: (gi, 0, 0)),
            mods.spec(k0, tm, d), mods.spec(k0 + 1, tm, d),
            pl.BlockSpec((d, width), lambda i: (0, col_block)),
            head_vec, head_vec,
        ] + [rope_spec] * len(rope),
        out_specs=[pl.BlockSpec((tm, wd), lambda i: (i, 0)) for wd in out_widths],
        out_shape=[jax.ShapeDtypeStruct((rows, wd), BF16) for wd in out_widths],
        compiler_params=_params("parallel"),
        name="qkv_proj",
    )(x2, norm_g, mods.table, mods.table, w,
      q_g.reshape(1, HEAD_DIM), k_g.reshape(1, HEAD_DIM), *rope)


def _attn_kernel(q_ref, kc_ref, vc_ref, kl_ref, vl_ref, o_ref, *, group):
    nt = (((1,), (1,)), ((), ()))
    kc, vc, kl, vl = kc_ref[0], vc_ref[0], kl_ref[0], vl_ref[0]
    for gi in range(group):
        cols = slice(gi * HEAD_DIM, (gi + 1) * HEAD_DIM)
        q = q_ref[0, :, cols]
        s_c = lax.dot_general(q, kc, nt, preferred_element_type=F32)
        s_l = lax.dot_general(q, kl, nt, preferred_element_type=F32)
        m = jnp.maximum(jnp.max(s_c, axis=-1, keepdims=True), jnp.max(s_l, axis=-1, keepdims=True))
        p_c = jnp.exp(s_c - m)
        p_l = jnp.exp(s_l - m)
        denom = jnp.sum(p_c, axis=-1, keepdims=True) + jnp.sum(p_l, axis=-1, keepdims=True)
        o = (jnp.dot(p_c.astype(BF16), vc, preferred_element_type=F32)
             + jnp.dot(p_l.astype(BF16), vl, preferred_element_type=F32))
        o_ref[0, :, cols] = (o / denom).astype(BF16)


def _attention(q3, kc3, vc3, kl3, vl3, tq=256):
    b, s, qd = q3.shape
    tq = min(tq, s)
    l = kc3.shape[1]
    group = qd // (N_KV_HEADS * HEAD_DIM)
    gw = group * HEAD_DIM
    kv_spec = lambda n: pl.BlockSpec((1, n, HEAD_DIM), lambda bi, kh, i: (bi, 0, kh))
    return pl.pallas_call(
        functools.partial(_attn_kernel, group=group),
        grid=(b, N_KV_HEADS, s // tq),
        in_specs=[
            pl.BlockSpec((1, tq, gw), lambda bi, kh, i: (bi, i, kh)),
            kv_spec(l), kv_spec(l), kv_spec(s), kv_spec(s),
        ],
        out_specs=pl.BlockSpec((1, tq, gw), lambda bi, kh, i: (bi, i, kh)),
        out_shape=jax.ShapeDtypeStruct((b, s, qd), BF16),
        compiler_params=_params("parallel", "parallel", "parallel"),
        name="attention",
    )(q3, kc3, vc3, kl3, vl3)


def _oproj_kernel(x_ref, a_ref, w_ref, gt_ref, o_ref):
    y = jnp.dot(a_ref[...], w_ref[...], preferred_element_type=F32)
    o_ref[...] = x_ref[...] + gt_ref[0] * y


def _out_proj(x2, a2, w, mods, k_gate, tm=512):
    rows, d = x2.shape
    tm = mods.tile(tm, rows)
    return pl.pallas_call(
        _oproj_kernel,
        grid=(rows // tm,),
        in_specs=[
            pl.BlockSpec((tm, d), lambda i: (i, 0)),
            pl.BlockSpec((tm, a2.shape[1]), lambda i: (i, 0)),
            pl.BlockSpec(w.shape, lambda i: (0, 0)),
            mods.spec(k_gate, tm, d),
        ],
        out_specs=pl.BlockSpec((tm, d), lambda i: (i, 0)),
        out_shape=jax.ShapeDtypeStruct((rows, d), F32),
        compiler_params=_params("parallel"),
        name="out_proj",
    )(x2, a2, w, mods.table)


def kernel(x, c, ctx, c_ctx, w_ada, b_ada, norm_g, w_ffn_in, w_ffn_out, w_fourier_out, w_qkv,
           q_norm_g, k_norm_g, w_attn_out):
    b, n, d = x.shape
    l = ctx.shape[1]
    depth = w_ada.shape[0]
    assert depth == 2 and b <= 8, "layer 0 = Fourier mixer, layer 1 = attention mixer"
    q_dim = w_attn_out.shape[1]
    n_heads = q_dim // HEAD_DIM

    s16 = jnp.zeros((MOD_ROWS, d), F32).at[:b].set(jax.nn.silu(c)).at[8].set(jax.nn.silu(c_ctx))
    table = _ada(s16.astype(BF16), w_ada, b_ada).reshape(depth * MOD_ROWS * N_MOD, 1, d)

    f = w_ffn_out.shape[2]
    w_in = w_ffn_in.astype(BF16).reshape(depth * 2, d, 2 * f)
    w_out = w_ffn_out.astype(BF16).reshape(depth * 2, f, d)
    ng = norm_g.reshape(depth * 3, 1, d)
    w_four = w_fourier_out.astype(BF16)
    w_qkv_b = w_qkv.astype(BF16)
    w_o = w_attn_out.astype(BF16)

    gd = d // FOURIER_GROUPS
    cos_c, sin_c = _dft_tables(gd)
    chan_tab = jnp.concatenate([cos_c, -sin_c], axis=0)

    xl = x.reshape(b * n, d)
    xc = ctx.reshape(b * l, d)

    ml = _Mods(table, 0, n, ctx=False)
    mc = _Mods(table, 0, l, ctx=True)
    xl = _ffn(xl, ng, 0, ml, 0, w_in, w_out, 0)
    xc = _ffn(xc, ng, 0, mc, 0, w_in, w_out, 0)
    xl = _fourier_layer(xl.reshape(b, n, d), ng, 1, ml, w_four[0], chan_tab).reshape(b * n, d)
    xc = _fourier_layer(xc.reshape(b, l, d), ng, 1, mc, w_four[0], chan_tab).reshape(b * l, d)
    xl = _ffn(xl, ng, 2, ml, 6, w_in, w_out, 1)
    xc = _ffn(xc, ng, 2, mc, 6, w_in, w_out, 1)

    ml = _Mods(table, 1, n, ctx=False)
    mc = _Mods(table, 1, l, ctx=True)
    xl = _ffn(xl, ng, 3, ml, 0, w_in, w_out, 2)
    xc = _ffn(xc, ng, 3, mc, 0, w_in, w_out, 2)
    rope = _rope_tables(n)
    q, kl, vl = _qkv_proj(xl, ng, 4, ml, 3, w_qkv_b[0], q_norm_g[0], k_norm_g[0], rope, n_heads, n)
    kc, vc = _qkv_proj(xc, ng, 4, mc, 3, w_qkv_b[0], q_norm_g[0], k_norm_g[0], None, 0, l)
    kv_dim = N_KV_HEADS * HEAD_DIM
    o = _attention(q.reshape(b, n, q_dim), kc.reshape(b, l, kv_dim), vc.reshape(b, l, kv_dim),
                   kl.reshape(b, n, kv_dim), vl.reshape(b, n, kv_dim))
    xl = _out_proj(xl, o.reshape(b * n, q_dim), w_o[0], ml, 5)
    xl = _ffn(xl, ng, 5, ml, 6, w_in, w_out, 3)
    return xl.reshape(b, n, d)
```

```python
import functools
import math

import jax
import jax.numpy as jnp
from jax import lax
from jax.experimental import pallas as pl
from jax.experimental.pallas import tpu as pltpu

F32 = jnp.float32
BF16 = jnp.bfloat16

GRID_W = 64
FOURIER_GROUPS = 8
HEAD_DIM = 128
N_KV_HEADS = 4
ROPE_FREQS = HEAD_DIM // 4
ROPE_THETA = 10000.0
N_MOD = 9
EPS = 1e-6
MOD_ROWS = 16
DFT_TABLE_SPLIT = 64
ATTN_SUB_ROWS = 256

VMEM_LIMIT_BYTES = 56 * 1024 * 1024


def _params(*sem):
    return pltpu.CompilerParams(dimension_semantics=sem, vmem_limit_bytes=VMEM_LIMIT_BYTES)


def _modulate(x, g, shift, scale):
    ms = jnp.mean(x * x, axis=-1, keepdims=True)
    return (x * lax.rsqrt(ms + EPS)) * g * (1.0 + scale) + shift


def _ada_kernel(s_ref, w_ref, b_ref, o_ref):
    w = w_ref[0].astype(BF16)
    o_ref[0] = jnp.dot(s_ref[...], w, preferred_element_type=F32) + b_ref[0]


def _ada(s16, w_ada, b_ada, tn=1024):
    depth, d, n = w_ada.shape
    return pl.pallas_call(
        _ada_kernel,
        grid=(depth, n // tn),
        in_specs=[
            pl.BlockSpec((MOD_ROWS, d), lambda l, j: (0, 0)),
            pl.BlockSpec((1, d, tn), lambda l, j: (l, 0, j)),
            pl.BlockSpec((1, 1, tn), lambda l, j: (l, 0, j)),
        ],
        out_specs=pl.BlockSpec((1, MOD_ROWS, tn), lambda l, j: (l, 0, j)),
        out_shape=jax.ShapeDtypeStruct((depth, MOD_ROWS, n), F32),
        compiler_params=_params("parallel", "parallel"),
        name="ada",
    )(s16, w_ada, b_ada.reshape(depth, 1, n))


class _Mods:
    def __init__(self, table, layer, rows_per_mod, ctx):
        self.table, self.layer, self.rows_per_mod, self.ctx = table, layer, rows_per_mod, ctx

    def tile(self, tm, rows):
        tm = min(tm, rows if self.ctx else self.rows_per_mod)
        assert rows % tm == 0 and (self.ctx or self.rows_per_mod % tm == 0)
        return tm

    def spec(self, k, tm, d):
        layer, rpm, ctx = self.layer, self.rows_per_mod, self.ctx

        def index(i, *_):
            row = (MOD_ROWS - 8) if ctx else (i * tm) // rpm
            return ((layer * MOD_ROWS + row) * N_MOD + k, 0, 0)

        return pl.BlockSpec((1, 1, d), index)


def _ffn_kernel(x_ref, g_ref, sh_ref, sc_ref, gt_ref, wg_ref, wu_ref, wo_ref, o_ref, h_ref):
    @pl.when(pl.program_id(1) == 0)
    def _():
        x = x_ref[...]
        h_ref[...] = _modulate(x, g_ref[0], sh_ref[0], sc_ref[0]).astype(BF16)
        o_ref[...] = x

    h = h_ref[...]
    gate = jnp.dot(h, wg_ref[...], preferred_element_type=F32)
    up = jnp.dot(h, wu_ref[...], preferred_element_type=F32)
    act = (jax.nn.silu(gate) * up).astype(BF16)
    o_ref[...] += (0.5 * gt_ref[0]) * jnp.dot(act, wo_ref[...], preferred_element_type=F32)


def _ffn(x2, norm_g, gi, mods, k0, w_in, w_out, wi, tm=512, tf=512):
    rows, d = x2.shape
    f = w_out.shape[1]
    tm, tf = mods.tile(tm, rows), min(tf, f)
    nk = f // tf
    return pl.pallas_call(
        _ffn_kernel,
        grid=(rows // tm, nk),
        in_specs=[
            pl.BlockSpec((tm, d), lambda i, k: (i, 0)),
            pl.BlockSpec((1, 1, d), lambda i, k: (gi, 0, 0)),
            mods.spec(k0, tm, d), mods.spec(k0 + 1, tm, d), mods.spec(k0 + 2, tm, d),
            pl.BlockSpec((None, d, tf), lambda i, k: (wi, 0, k)),
            pl.BlockSpec((None, d, tf), lambda i, k: (wi, 0, nk + k)),
            pl.BlockSpec((None, tf, d), lambda i, k: (wi, k, 0)),
        ],
        out_specs=pl.BlockSpec((tm, d), lambda i, k: (i, 0)),
        out_shape=jax.ShapeDtypeStruct((rows, d), F32),
        scratch_shapes=[pltpu.VMEM((tm, d), BF16)],
        compiler_params=_params("parallel", "arbitrary"),
        name="ffn",
    )(x2, norm_g, mods.table, mods.table, mods.table, w_in, w_in, w_out)


def _mod_kernel(x_ref, g_ref, sh_ref, sc_ref, o_ref):
    o_ref[...] = _modulate(x_ref[...], g_ref[0], sh_ref[0], sc_ref[0]).astype(o_ref.dtype)


def _modulated(x2, norm_g, gi, mods, k0, tm=512):
    rows, d = x2.shape
    tm = mods.tile(tm, rows)
    return pl.pallas_call(
        _mod_kernel,
        grid=(rows // tm,),
        in_specs=[
            pl.BlockSpec((tm, d), lambda i: (i, 0)),
            pl.BlockSpec((1, 1, d), lambda i: (gi, 0, 0)),
            mods.spec(k0, tm, d), mods.spec(k0 + 1, tm, d),
        ],
        out_specs=pl.BlockSpec((tm, d), lambda i: (i, 0)),
        out_shape=jax.ShapeDtypeStruct((rows, d), BF16),
        compiler_params=_params("parallel"),
        name="modulate",
    )(x2, norm_g, mods.table, mods.table)


def _dft_tables(n):
    def unit(rows, cols):
        phase = (rows[:, None] * cols[None, :]) % n
        ang = phase.astype(F32) * (2.0 * math.pi / n)
        return jnp.cos(ang), jnp.sin(ang)

    idx = jnp.arange(n, dtype=jnp.int32)
    s = 1.0 / math.sqrt(n)
    m = DFT_TABLE_SPLIT
    if n <= m or n % m:
        c, sn = unit(idx, idx)
    else:
        ca, sa = unit(idx, jnp.arange(n // m, dtype=jnp.int32) * m)
        cb, sb = unit(idx, jnp.arange(m, dtype=jnp.int32))
        c = (ca[:, :, None] * cb[:, None, :] - sa[:, :, None] * sb[:, None, :]).reshape(n, n)
        sn = (sa[:, :, None] * cb[:, None, :] + ca[:, :, None] * sb[:, None, :]).reshape(n, n)
    return (c * s).astype(BF16), (sn * s).astype(BF16)


def _pos_dft_kernel(c_ref, s_ref, h_ref, zr_ref, zi_ref):
    h = h_ref[0]
    zr_ref[0] = jnp.dot(c_ref[...], h, preferred_element_type=F32).astype(BF16)
    zi_ref[0] = jnp.dot(s_ref[...], h, preferred_element_type=F32).astype(BF16)


def _pos_dft(h3, cos_n, sin_n, tk=256):
    b, n, d = h3.shape
    tk = min(tk, n)
    out = jax.ShapeDtypeStruct((b, n, d), BF16)
    return pl.pallas_call(
        _pos_dft_kernel,
        grid=(b, n // tk),
        in_specs=[
            pl.BlockSpec((tk, n), lambda bi, i: (i, 0)),
            pl.BlockSpec((tk, n), lambda bi, i: (i, 0)),
            pl.BlockSpec((1, n, d), lambda bi, i: (bi, 0, 0), pipeline_mode=pl.Buffered(1)),
        ],
        out_specs=[pl.BlockSpec((1, tk, d), lambda bi, i: (bi, i, 0))] * 2,
        out_shape=[out, out],
        compiler_params=_params("parallel", "parallel"),
        name="pos_dft",
    )(cos_n, sin_n, h3)


def _chan_kernel(x_ref, zr_ref, zi_ref, tab_ref, w_ref, gt_ref, o_ref, f_ref):
    gd = tab_ref.shape[1]
    tab = tab_ref[...]
    for grp in range(FOURIER_GROUPS):
        cols = slice(grp * gd, (grp + 1) * gd)
        z = jnp.concatenate([zr_ref[:, cols], zi_ref[:, cols]], axis=-1)
        f_ref[:, cols] = jnp.dot(z, tab, preferred_element_type=F32).astype(BF16)
    y = jnp.dot(f_ref[...], w_ref[...], preferred_element_type=F32)
    o_ref[...] = x_ref[...] + gt_ref[0] * y


def _chan_mix(x2, zr2, zi2, tab, w_out, mods, k_gate, tm=512):
    rows, d = x2.shape
    tm = mods.tile(tm, rows)
    return pl.pallas_call(
        _chan_kernel,
        grid=(rows // tm,),
        in_specs=[
            pl.BlockSpec((tm, d), lambda i: (i, 0)),
            pl.BlockSpec((tm, d), lambda i: (i, 0)),
            pl.BlockSpec((tm, d), lambda i: (i, 0)),
            pl.BlockSpec(tab.shape, lambda i: (0, 0)),
            pl.BlockSpec((d, d), lambda i: (0, 0)),
            mods.spec(k_gate, tm, d),
        ],
        out_specs=pl.BlockSpec((tm, d), lambda i: (i, 0)),
        out_shape=jax.ShapeDtypeStruct((rows, d), F32),
        scratch_shapes=[pltpu.VMEM((tm, d), BF16)],
        compiler_params=_params("parallel"),
        name="chan_mix",
    )(x2, zr2, zi2, tab, w_out, mods.table)


def _fourier_layer(x3, norm_g, gi, mods, w_out, chan_tab):
    b, n, d = x3.shape
    x2 = x3.reshape(b * n, d)
    h = _modulated(x2, norm_g, gi, mods, 3).reshape(b, n, d)
    cos_n, sin_n = _dft_tables(n)
    zr, zi = _pos_dft(h, cos_n, sin_n)
    out = _chan_mix(x2, zr.reshape(b * n, d), zi.reshape(b * n, d), chan_tab, w_out, mods, 5)
    return out.reshape(b, n, d)


def _rope_tables(n):
    pos = jnp.arange(n, dtype=jnp.int32)
    row = (pos // GRID_W).astype(F32)
    col = (pos % GRID_W).astype(F32)
    inv_freq = ROPE_THETA ** (-jnp.arange(ROPE_FREQS, dtype=F32) / ROPE_FREQS)
    a_r = row[:, None] * inv_freq
    a_c = col[:, None] * inv_freq
    ang = jnp.concatenate([a_r, a_r, a_c, a_c], axis=-1)
    cos, sin = jnp.cos(ang), jnp.sin(ang)
    first_half = (jnp.arange(HEAD_DIM) % (2 * ROPE_FREQS)) < ROPE_FREQS
    sin_from_upper = jnp.where(first_half[None, :], -sin, 0.0)
    sin_from_lower = jnp.where(first_half[None, :], 0.0, sin)
    return cos, sin_from_upper, sin_from_lower


def _head_norm(t, g):
    ms = jnp.mean(t * t, axis=-1, keepdims=True)
    return t * lax.rsqrt(ms + EPS) * g


def _rope(t, cos, s_up, s_lo):
    up = pltpu.roll(t, HEAD_DIM - ROPE_FREQS, 1)
    lo = pltpu.roll(t, ROPE_FREQS, 1)
    return t * cos + up * s_up + lo * s_lo


def _qkv_kernel(x_ref, g_ref, sh_ref, sc_ref, w_ref, qg_ref, kg_ref, *refs,
                n_q_heads, use_rope, q_scale):
    if use_rope:
        cos, s_up, s_lo = (r[...] for r in refs[:3])
        refs = refs[3:]
    q_ref = refs[0] if n_q_heads else None
    k_ref, v_ref = refs[-2:]
    h = _modulate(x_ref[...], g_ref[0], sh_ref[0], sc_ref[0]).astype(BF16)
    qkv = jnp.dot(h, w_ref[...], preferred_element_type=F32)
    for hd in range(n_q_heads + N_KV_HEADS):
        t = qkv[:, hd * HEAD_DIM:(hd + 1) * HEAD_DIM]
        is_q = hd < n_q_heads
        t = _head_norm(t, qg_ref[...] if is_q else kg_ref[...])
        if use_rope:
            t = _rope(t, cos, s_up, s_lo)
        if is_q:
            q_ref[:, hd * HEAD_DIM:(hd + 1) * HEAD_DIM] = (t * q_scale).astype(BF16)
        else:
            kh = hd - n_q_heads
            k_ref[:, kh * HEAD_DIM:(kh + 1) * HEAD_DIM] = t.astype(BF16)
    v0 = (n_q_heads + N_KV_HEADS) * HEAD_DIM
    v_ref[...] = qkv[:, v0:].astype(BF16)


def _qkv_proj(x2, norm_g, gi, mods, k0, w, q_g, k_g, rope, n_q_heads, seq, tm=512):
    rows, d = x2.shape
    tm = mods.tile(tm, rows)
    kv_dim = N_KV_HEADS * HEAD_DIM
    q_dim = n_q_heads * HEAD_DIM
    width = q_dim + 2 * kv_dim
    col_block, rem = divmod(w.shape[1] - width, width)
    assert rem == 0
    use_rope = rope is not None
    rope = rope if use_rope else ()
    tiles_per_seq = seq // tm
    kern = functools.partial(_qkv_kernel, n_q_heads=n_q_heads, use_rope=use_rope,
                             q_scale=math.log2(math.e) / math.sqrt(HEAD_DIM))
    head_vec = pl.BlockSpec((1, HEAD_DIM), lambda i: (0, 0))
    rope_spec = pl.BlockSpec((tm, HEAD_DIM), lambda i: (i % tiles_per_seq, 0))
    out_widths = ([q_dim] if n_q_heads else []) + [kv_dim, kv_dim]
    return pl.pallas_call(
        kern,
        grid=(rows // tm,),
        in_specs=[
            pl.BlockSpec((tm, d), lambda i: (i, 0)),
            pl.BlockSpec((1, 1, d), lambda i: (gi, 0, 0)),
            mods.spec(k0, tm, d), mods.spec(k0 + 1, tm, d),
            pl.BlockSpec((d, width), lambda i: (0, col_block)),
            head_vec, head_vec,
        ] + [rope_spec] * len(rope),
        out_specs=[pl.BlockSpec((tm, wd), lambda i: (i, 0)) for wd in out_widths],
        out_shape=[jax.ShapeDtypeStruct((rows, wd), BF16) for wd in out_widths],
        compiler_params=_params("parallel"),
        name="qkv_proj",
    )(x2, norm_g, mods.table, mods.table, w,
      q_g.reshape(1, HEAD_DIM), k_g.reshape(1, HEAD_DIM), *rope)


def _attn_kernel(q_ref, kc_ref, vc_ref, kl_ref, vl_ref, o_ref, k_ref, v_ref, *, group):
    l = kc_ref.shape[1]

    @pl.when(pl.program_id(2) == 0)
    def _():
        k_ref[:l] = kc_ref[0]
        k_ref[l:] = kl_ref[0]
        v_ref[:l, :HEAD_DIM] = vc_ref[0]
        v_ref[l:, :HEAD_DIM] = vl_ref[0]
        v_ref[:, HEAD_DIM:] = jnp.ones((v_ref.shape[0], HEAD_DIM), BF16)

    nt = (((1,), (1,)), ((), ()))
    k, v = k_ref[...], v_ref[...]
    tq = q_ref.shape[1]
    sub = min(tq, ATTN_SUB_ROWS)
    for r0 in range(0, tq, sub):
        rows = slice(r0, r0 + sub)
        for gi in range(group):
            cols = slice(gi * HEAD_DIM, (gi + 1) * HEAD_DIM)
            s = lax.dot_general(q_ref[0, rows, cols], k, nt, preferred_element_type=F32)
            p = jnp.exp2(s - jnp.max(s, axis=-1, keepdims=True)).astype(BF16)
            oe = jnp.dot(p, v, preferred_element_type=F32)
            o_ref[0, rows, cols] = (oe[:, :HEAD_DIM] / oe[:, HEAD_DIM:HEAD_DIM + 1]).astype(BF16)


def _attention(q3, kc3, vc3, kl3, vl3, tq=1024):
    b, s, qd = q3.shape
    tq = min(tq, s)
    l = kc3.shape[1]
    group = qd // (N_KV_HEADS * HEAD_DIM)
    gw = group * HEAD_DIM
    kv_spec = lambda n: pl.BlockSpec((1, n, HEAD_DIM), lambda bi, kh, i: (bi, 0, kh))
    return pl.pallas_call(
        functools.partial(_attn_kernel, group=group),
        grid=(b, N_KV_HEADS, s // tq),
        in_specs=[
            pl.BlockSpec((1, tq, gw), lambda bi, kh, i: (bi, i, kh)),
            kv_spec(l), kv_spec(l), kv_spec(s), kv_spec(s),
        ],
        out_specs=pl.BlockSpec((1, tq, gw), lambda bi, kh, i: (bi, i, kh)),
        out_shape=jax.ShapeDtypeStruct((b, s, qd), BF16),
        scratch_shapes=[pltpu.VMEM((l + s, HEAD_DIM), BF16), pltpu.VMEM((l + s, 2 * HEAD_DIM), BF16)],
        compiler_params=_params("parallel", "parallel", "arbitrary"),
        name="attention",
    )(q3, kc3, vc3, kl3, vl3)


def _oproj_kernel(x_ref, a_ref, w_ref, gt_ref, o_ref):
    y = jnp.dot(a_ref[...], w_ref[...], preferred_element_type=F32)
    o_ref[...] = x_ref[...] + gt_ref[0] * y


def _out_proj(x2, a2, w, mods, k_gate, tm=512):
    rows, d = x2.shape
    tm = mods.tile(tm, rows)
    return pl.pallas_call(
        _oproj_kernel,
        grid=(rows // tm,),
        in_specs=[
            pl.BlockSpec((tm, d), lambda i: (i, 0)),
            pl.BlockSpec((tm, a2.shape[1]), lambda i: (i, 0)),
            pl.BlockSpec(w.shape, lambda i: (0, 0)),
            mods.spec(k_gate, tm, d),
        ],
        out_specs=pl.BlockSpec((tm, d), lambda i: (i, 0)),
        out_shape=jax.ShapeDtypeStruct((rows, d), F32),
        compiler_params=_params("parallel"),
        name="out_proj",
    )(x2, a2, w, mods.table)


def kernel(x, c, ctx, c_ctx, w_ada, b_ada, norm_g, w_ffn_in, w_ffn_out, w_fourier_out, w_qkv,
           q_norm_g, k_norm_g, w_attn_out):
    b, n, d = x.shape
    l = ctx.shape[1]
    depth = w_ada.shape[0]
    assert depth == 2 and b <= 8, "layer 0 = Fourier mixer, layer 1 = attention mixer"
    q_dim = w_attn_out.shape[1]
    n_heads = q_dim // HEAD_DIM

    s16 = jnp.zeros((MOD_ROWS, d), F32).at[:b].set(jax.nn.silu(c)).at[8].set(jax.nn.silu(c_ctx))
    table = _ada(s16.astype(BF16), w_ada, b_ada).reshape(depth * MOD_ROWS * N_MOD, 1, d)

    f = w_ffn_out.shape[2]
    w_in = w_ffn_in.astype(BF16).reshape(depth * 2, d, 2 * f)
    w_out = w_ffn_out.astype(BF16).reshape(depth * 2, f, d)
    ng = norm_g.reshape(depth * 3, 1, d)
    w_four = w_fourier_out.astype(BF16)
    w_qkv_b = w_qkv.astype(BF16)
    w_o = w_attn_out.astype(BF16)

    gd = d // FOURIER_GROUPS
    cos_c, sin_c = _dft_tables(gd)
    chan_tab = jnp.concatenate([cos_c, -sin_c], axis=0)

    xl = x.reshape(b * n, d)
    xc = ctx.reshape(b * l, d)

    ml = _Mods(table, 0, n, ctx=False)
    mc = _Mods(table, 0, l, ctx=True)
    xl = _ffn(xl, ng, 0, ml, 0, w_in, w_out, 0)
    xc = _ffn(xc, ng, 0, mc, 0, w_in, w_out, 0)
    xl = _fourier_layer(xl.reshape(b, n, d), ng, 1, ml, w_four[0], chan_tab).reshape(b * n, d)
    xc = _fourier_layer(xc.reshape(b, l, d), ng, 1, mc, w_four[0], chan_tab).reshape(b * l, d)
    xl = _ffn(xl, ng, 2, ml, 6, w_in, w_out, 1)
    xc = _ffn(xc, ng, 2, mc, 6, w_in, w_out, 1)

    ml = _Mods(table, 1, n, ctx=False)
    mc = _Mods(table, 1, l, ctx=True)
    xl = _ffn(xl, ng, 3, ml, 0, w_in, w_out, 2)
    xc = _ffn(xc, ng, 3, mc, 0, w_in, w_out, 2)
    rope = _rope_tables(n)
    q, kl, vl = _qkv_proj(xl, ng, 4, ml, 3, w_qkv_b[0], q_norm_g[0], k_norm_g[0], rope, n_heads, n)
    kc, vc = _qkv_proj(xc, ng, 4, mc, 3, w_qkv_b[0], q_norm_g[0], k_norm_g[0], None, 0, l)
    kv_dim = N_KV_HEADS * HEAD_DIM
    o = _attention(q.reshape(b, n, q_dim), kc.reshape(b, l, kv_dim), vc.reshape(b, l, kv_dim),
                   kl.reshape(b, n, kv_dim), vl.reshape(b, n, kv_dim))
    xl = _out_proj(xl, o.reshape(b * n, q_dim), w_o[0], ml, 5)
    xl = _ffn(xl, ng, 5, ml, 6, w_in, w_out, 3)
    return xl.reshape(b, n, d)
```

```python
import functools
import math

import jax
import jax.numpy as jnp
from jax import lax
from jax.experimental import pallas as pl
from jax.experimental.pallas import tpu as pltpu

F32 = jnp.float32
BF16 = jnp.bfloat16

GRID_W = 64
FOURIER_GROUPS = 8
HEAD_DIM = 128
N_KV_HEADS = 4
ROPE_FREQS = HEAD_DIM // 4
ROPE_THETA = 10000.0
N_MOD = 9
EPS = 1e-6
MOD_ROWS = 16
DFT_TABLE_SPLIT = 64
ATTN_SUB_ROWS = 256
MOD_CHUNK_ROWS = 16

VMEM_LIMIT_BYTES = 62 * 1024 * 1024


def _params(*sem):
    return pltpu.CompilerParams(dimension_semantics=sem, vmem_limit_bytes=VMEM_LIMIT_BYTES)


def _modulate(x, g, shift, scale):
    ms = jnp.mean(x * x, axis=-1, keepdims=True)
    return (x * lax.rsqrt(ms + EPS)) * g * (1.0 + scale) + shift


def _ada_kernel(s_ref, w_ref, b_ref, o_ref):
    w = w_ref[0].astype(BF16)
    o_ref[0] = jnp.dot(s_ref[...], w, preferred_element_type=F32) + b_ref[0]


def _ada(s16, w_ada, b_ada, tn=1024):
    depth, d, n = w_ada.shape
    return pl.pallas_call(
        _ada_kernel,
        grid=(depth, n // tn),
        in_specs=[
            pl.BlockSpec((MOD_ROWS, d), lambda l, j: (0, 0)),
            pl.BlockSpec((1, d, tn), lambda l, j: (l, 0, j)),
            pl.BlockSpec((1, 1, tn), lambda l, j: (l, 0, j)),
        ],
        out_specs=pl.BlockSpec((1, MOD_ROWS, tn), lambda l, j: (l, 0, j)),
        out_shape=jax.ShapeDtypeStruct((depth, MOD_ROWS, n), F32),
        compiler_params=_params("parallel", "parallel"),
        name="ada",
    )(s16, w_ada, b_ada.reshape(depth, 1, n))


class _Mods:
    def __init__(self, table, layer, rows_per_mod, ctx):
        self.table, self.layer, self.rows_per_mod, self.ctx = table, layer, rows_per_mod, ctx

    def tile(self, tm, rows):
        tm = min(tm, rows if self.ctx else self.rows_per_mod)
        assert rows % tm == 0 and (self.ctx or self.rows_per_mod % tm == 0)
        return tm

    def spec(self, k, tm, d):
        layer, rpm, ctx = self.layer, self.rows_per_mod, self.ctx

        def index(i, *_):
            row = (MOD_ROWS - 8) if ctx else (i * tm) // rpm
            return ((layer * MOD_ROWS + row) * N_MOD + k, 0, 0)

        return pl.BlockSpec((1, 1, d), index)


def _ffn_kernel(x_ref, g_ref, sh_ref, sc_ref, gt_ref, wg_ref, wu_ref, wo_ref, o_ref, h_ref):
    @pl.when(pl.program_id(1) == 0)
    def _():
        gain = g_ref[0] * (1.0 + sc_ref[0])
        shift = sh_ref[0]

        def rows(c, carry):
            r = pl.ds(pl.multiple_of(c * MOD_CHUNK_ROWS, MOD_CHUNK_ROWS), MOD_CHUNK_ROWS)
            x = x_ref[r, :]
            ms = jnp.mean(x * x, axis=-1, keepdims=True)
            h_ref[r, :] = ((x * lax.rsqrt(ms + EPS)) * gain + shift).astype(BF16)
            o_ref[r, :] = x
            return carry

        lax.fori_loop(0, x_ref.shape[0] // MOD_CHUNK_ROWS, rows, 0, unroll=8)

    h = h_ref[...]
    gate = jnp.dot(h, wg_ref[...], preferred_element_type=F32)
    up = jnp.dot(h, wu_ref[...], preferred_element_type=F32)
    act = (jax.nn.silu(gate) * up).astype(BF16)
    o_ref[...] += (0.5 * gt_ref[0]) * jnp.dot(act, wo_ref[...], preferred_element_type=F32)


def _ffn(x2, norm_g, gi, mods, k0, w_in, w_out, wi, tm=1024, tf=512):
    rows, d = x2.shape
    f = w_out.shape[1]
    tm, tf = mods.tile(tm, rows), min(tf, f)
    nk = f // tf
    return pl.pallas_call(
        _ffn_kernel,
        grid=(rows // tm, nk),
        in_specs=[
            pl.BlockSpec((tm, d), lambda i, k: (i, 0)),
            pl.BlockSpec((1, 1, d), lambda i, k: (gi, 0, 0)),
            mods.spec(k0, tm, d), mods.spec(k0 + 1, tm, d), mods.spec(k0 + 2, tm, d),
            pl.BlockSpec((None, d, tf), lambda i, k: (wi, 0, k)),
            pl.BlockSpec((None, d, tf), lambda i, k: (wi, 0, nk + k)),
            pl.BlockSpec((None, tf, d), lambda i, k: (wi, k, 0)),
        ],
        out_specs=pl.BlockSpec((tm, d), lambda i, k: (i, 0)),
        out_shape=jax.ShapeDtypeStruct((rows, d), F32),
        scratch_shapes=[pltpu.VMEM((tm, d), BF16)],
        compiler_params=_params("parallel", "arbitrary"),
        name="ffn",
    )(x2, norm_g, mods.table, mods.table, mods.table, w_in, w_in, w_out)


def _mod_kernel(x_ref, g_ref, sh_ref, sc_ref, o_ref):
    o_ref[...] = _modulate(x_ref[...], g_ref[0], sh_ref[0], sc_ref[0]).astype(o_ref.dtype)


def _modulated(x2, norm_g, gi, mods, k0, tm=512):
    rows, d = x2.shape
    tm = mods.tile(tm, rows)
    return pl.pallas_call(
        _mod_kernel,
        grid=(rows // tm,),
        in_specs=[
            pl.BlockSpec((tm, d), lambda i: (i, 0)),
            pl.BlockSpec((1, 1, d), lambda i: (gi, 0, 0)),
            mods.spec(k0, tm, d), mods.spec(k0 + 1, tm, d),
        ],
        out_specs=pl.BlockSpec((tm, d), lambda i: (i, 0)),
        out_shape=jax.ShapeDtypeStruct((rows, d), BF16),
        compiler_params=_params("parallel"),
        name="modulate",
    )(x2, norm_g, mods.table, mods.table)


def _dft_tables(n):
    def unit(rows, cols):
        phase = (rows[:, None] * cols[None, :]) % n
        ang = phase.astype(F32) * (2.0 * math.pi / n)
        return jnp.cos(ang), jnp.sin(ang)

    idx = jnp.arange(n, dtype=jnp.int32)
    s = 1.0 / math.sqrt(n)
    m = DFT_TABLE_SPLIT
    if n <= m or n % m:
        c, sn = unit(idx, idx)
    else:
        ca, sa = unit(idx, jnp.arange(n // m, dtype=jnp.int32) * m)
        cb, sb = unit(idx, jnp.arange(m, dtype=jnp.int32))
        c = (ca[:, :, None] * cb[:, None, :] - sa[:, :, None] * sb[:, None, :]).reshape(n, n)
        sn = (sa[:, :, None] * cb[:, None, :] + ca[:, :, None] * sb[:, None, :]).reshape(n, n)
    return (c * s).astype(BF16), (sn * s).astype(BF16)


def _pos_dft_kernel(c_ref, s_ref, h_ref, zr_ref, zi_ref):
    h = h_ref[0]
    zr_ref[0] = jnp.dot(c_ref[...], h, preferred_element_type=F32).astype(BF16)
    zi_ref[0] = jnp.dot(s_ref[...], h, preferred_element_type=F32).astype(BF16)


def _pos_dft(h3, cos_n, sin_n, tk=256):
    b, n, d = h3.shape
    tk = min(tk, n)
    out = jax.ShapeDtypeStruct((b, n, d), BF16)
    return pl.pallas_call(
        _pos_dft_kernel,
        grid=(b, n // tk),
        in_specs=[
            pl.BlockSpec((tk, n), lambda bi, i: (i, 0)),
            pl.BlockSpec((tk, n), lambda bi, i: (i, 0)),
            pl.BlockSpec((1, n, d), lambda bi, i: (bi, 0, 0), pipeline_mode=pl.Buffered(1)),
        ],
        out_specs=[pl.BlockSpec((1, tk, d), lambda bi, i: (bi, i, 0))] * 2,
        out_shape=[out, out],
        compiler_params=_params("parallel", "parallel"),
        name="pos_dft",
    )(cos_n, sin_n, h3)


def _chan_kernel(x_ref, zr_ref, zi_ref, tab_ref, w_ref, gt_ref, o_ref, f_ref):
    gd = tab_ref.shape[1]
    tab = tab_ref[...]
    for grp in range(FOURIER_GROUPS):
        cols = slice(grp * gd, (grp + 1) * gd)
        z = jnp.concatenate([zr_ref[:, cols], zi_ref[:, cols]], axis=-1)
        f_ref[:, cols] = jnp.dot(z, tab, preferred_element_type=F32).astype(BF16)
    y = jnp.dot(f_ref[...], w_ref[...], preferred_element_type=F32)
    o_ref[...] = x_ref[...] + gt_ref[0] * y


def _chan_mix(x2, zr2, zi2, tab, w_out, mods, k_gate, tm=512):
    rows, d = x2.shape
    tm = mods.tile(tm, rows)
    return pl.pallas_call(
        _chan_kernel,
        grid=(rows // tm,),
        in_specs=[
            pl.BlockSpec((tm, d), lambda i: (i, 0)),
            pl.BlockSpec((tm, d), lambda i: (i, 0)),
            pl.BlockSpec((tm, d), lambda i: (i, 0)),
            pl.BlockSpec(tab.shape, lambda i: (0, 0)),
            pl.BlockSpec((d, d), lambda i: (0, 0)),
            mods.spec(k_gate, tm, d),
        ],
        out_specs=pl.BlockSpec((tm, d), lambda i: (i, 0)),
        out_shape=jax.ShapeDtypeStruct((rows, d), F32),
        scratch_shapes=[pltpu.VMEM((tm, d), BF16)],
        compiler_params=_params("parallel"),
        name="chan_mix",
    )(x2, zr2, zi2, tab, w_out, mods.table)


def _fourier_layer(x3, norm_g, gi, mods, w_out, chan_tab):
    b, n, d = x3.shape
    x2 = x3.reshape(b * n, d)
    h = _modulated(x2, norm_g, gi, mods, 3).reshape(b, n, d)
    cos_n, sin_n = _dft_tables(n)
    zr, zi = _pos_dft(h, cos_n, sin_n)
    out = _chan_mix(x2, zr.reshape(b * n, d), zi.reshape(b * n, d), chan_tab, w_out, mods, 5)
    return out.reshape(b, n, d)


def _rope_tables(n):
    pos = jnp.arange(n, dtype=jnp.int32)
    row = (pos // GRID_W).astype(F32)
    col = (pos % GRID_W).astype(F32)
    inv_freq = ROPE_THETA ** (-jnp.arange(ROPE_FREQS, dtype=F32) / ROPE_FREQS)
    a_r = row[:, None] * inv_freq
    a_c = col[:, None] * inv_freq
    ang = jnp.concatenate([a_r, a_r, a_c, a_c], axis=-1)
    cos, sin = jnp.cos(ang), jnp.sin(ang)
    first_half = (jnp.arange(HEAD_DIM) % (2 * ROPE_FREQS)) < ROPE_FREQS
    sin_from_upper = jnp.where(first_half[None, :], -sin, 0.0)
    sin_from_lower = jnp.where(first_half[None, :], 0.0, sin)
    return cos, sin_from_upper, sin_from_lower


def _head_norm(t, g):
    ms = jnp.mean(t * t, axis=-1, keepdims=True)
    return t * lax.rsqrt(ms + EPS) * g


def _rope(t, cos, s_up, s_lo):
    up = pltpu.roll(t, HEAD_DIM - ROPE_FREQS, 1)
    lo = pltpu.roll(t, ROPE_FREQS, 1)
    return t * cos + up * s_up + lo * s_lo


def _qkv_kernel(x_ref, g_ref, sh_ref, sc_ref, w_ref, qg_ref, kg_ref, *refs,
                n_q_heads, use_rope, q_scale):
    if use_rope:
        cos, s_up, s_lo = (r[...] for r in refs[:3])
        refs = refs[3:]
    q_ref = refs[0] if n_q_heads else None
    k_ref, v_ref = refs[-2:]
    h = _modulate(x_ref[...], g_ref[0], sh_ref[0], sc_ref[0]).astype(BF16)
    qkv = jnp.dot(h, w_ref[...], preferred_element_type=F32)
    for hd in range(n_q_heads + N_KV_HEADS):
        t = qkv[:, hd * HEAD_DIM:(hd + 1) * HEAD_DIM]
        is_q = hd < n_q_heads
        t = _head_norm(t, qg_ref[...] if is_q else kg_ref[...])
        if use_rope:
            t = _rope(t, cos, s_up, s_lo)
        if is_q:
            q_ref[:, hd * HEAD_DIM:(hd + 1) * HEAD_DIM] = (t * q_scale).astype(BF16)
        else:
            kh = hd - n_q_heads
            k_ref[:, kh * HEAD_DIM:(kh + 1) * HEAD_DIM] = t.astype(BF16)
    v0 = (n_q_heads + N_KV_HEADS) * HEAD_DIM
    v_ref[...] = qkv[:, v0:].astype(BF16)


def _qkv_proj(x2, norm_g, gi, mods, k0, w, q_g, k_g, rope, n_q_heads, seq, tm=512):
    rows, d = x2.shape
    tm = mods.tile(tm, rows)
    kv_dim = N_KV_HEADS * HEAD_DIM
    q_dim = n_q_heads * HEAD_DIM
    width = q_dim + 2 * kv_dim
    col_block, rem = divmod(w.shape[1] - width, width)
    assert rem == 0
    use_rope = rope is not None
    rope = rope if use_rope else ()
    tiles_per_seq = seq // tm
    kern = functools.partial(_qkv_kernel, n_q_heads=n_q_heads, use_rope=use_rope,
                             q_scale=math.log2(math.e) / math.sqrt(HEAD_DIM))
    head_vec = pl.BlockSpec((1, HEAD_DIM), lambda i: (0, 0))
    rope_spec = pl.BlockSpec((tm, HEAD_DIM), lambda i: (i % tiles_per_seq, 0))
    out_widths = ([q_dim] if n_q_heads else []) + [kv_dim, kv_dim]
    return pl.pallas_call(
        kern,
        grid=(rows // tm,),
        in_specs=[
            pl.BlockSpec((tm, d), lambda i: (i, 0)),
            pl.BlockSpec((1, 1, d), lambda i: (gi, 0, 0)),
            mods.spec(k0, tm, d), mods.spec(k0 + 1, tm, d),
            pl.BlockSpec((d, width), lambda i: (0, col_block)),
            head_vec, head_vec,
        ] + [rope_spec] * len(rope),
        out_specs=[pl.BlockSpec((tm, wd), lambda i: (i, 0)) for wd in out_widths],
        out_shape=[jax.ShapeDtypeStruct((rows, wd), BF16) for wd in out_widths],
        compiler_params=_params("parallel"),
        name="qkv_proj",
    )(x2, norm_g, mods.table, mods.table, w,
      q_g.reshape(1, HEAD_DIM), k_g.reshape(1, HEAD_DIM), *rope)


def _attn_kernel(q_ref, kc_ref, vc_ref, kl_ref, vl_ref, o_ref, k_ref, v_ref, *, group):
    l = kc_ref.shape[1]

    @pl.when(pl.program_id(2) == 0)
    def _():
        k_ref[:l] = kc_ref[0]
        k_ref[l:] = kl_ref[0]
        v_ref[:l, :HEAD_DIM] = vc_ref[0]
        v_ref[l:, :HEAD_DIM] = vl_ref[0]
        v_ref[:, HEAD_DIM:] = jnp.ones((v_ref.shape[0], HEAD_DIM), BF16)

    nt = (((1,), (1,)), ((), ()))
    k, v = k_ref[...], v_ref[...]
    tq = q_ref.shape[1]
    sub = min(tq, ATTN_SUB_ROWS)
    for r0 in range(0, tq, sub):
        rows = slice(r0, r0 + sub)
        for gi in range(group):
            cols = slice(gi * HEAD_DIM, (gi + 1) * HEAD_DIM)
            s = lax.dot_general(q_ref[0, rows, cols], k, nt, preferred_element_type=F32)
            p = jnp.exp2(s - jnp.max(s, axis=-1, keepdims=True)).astype(BF16)
            oe = jnp.dot(p, v, preferred_element_type=F32)
            o_ref[0, rows, cols] = (oe[:, :HEAD_DIM] / oe[:, HEAD_DIM:HEAD_DIM + 1]).astype(BF16)


def _attention(q3, kc3, vc3, kl3, vl3, tq=1024):
    b, s, qd = q3.shape
    tq = min(tq, s)
    l = kc3.shape[1]
    group = qd // (N_KV_HEADS * HEAD_DIM)
    gw = group * HEAD_DIM
    kv_spec = lambda n: pl.BlockSpec((1, n, HEAD_DIM), lambda bi, kh, i: (bi, 0, kh))
    return pl.pallas_call(
        functools.partial(_attn_kernel, group=group),
        grid=(b, N_KV_HEADS, s // tq),
        in_specs=[
            pl.BlockSpec((1, tq, gw), lambda bi, kh, i: (bi, i, kh)),
            kv_spec(l), kv_spec(l), kv_spec(s), kv_spec(s),
        ],
        out_specs=pl.BlockSpec((1, tq, gw), lambda bi, kh, i: (bi, i, kh)),
        out_shape=jax.ShapeDtypeStruct((b, s, qd), BF16),
        scratch_shapes=[pltpu.VMEM((l + s, HEAD_DIM), BF16), pltpu.VMEM((l + s, 2 * HEAD_DIM), BF16)],
        compiler_params=_params("parallel", "parallel", "arbitrary"),
        name="attention",
    )(q3, kc3, vc3, kl3, vl3)


def _oproj_kernel(x_ref, a_ref, w_ref, gt_ref, o_ref):
    y = jnp.dot(a_ref[...], w_ref[...], preferred_element_type=F32)
    o_ref[...] = x_ref[...] + gt_ref[0] * y


def _out_proj(x2, a2, w, mods, k_gate, tm=512):
    rows, d = x2.shape
    tm = mods.tile(tm, rows)
    return pl.pallas_call(
        _oproj_kernel,
        grid=(rows // tm,),
        in_specs=[
            pl.BlockSpec((tm, d), lambda i: (i, 0)),
            pl.BlockSpec((tm, a2.shape[1]), lambda i: (i, 0)),
            pl.BlockSpec(w.shape, lambda i: (0, 0)),
            mods.spec(k_gate, tm, d),
        ],
        out_specs=pl.BlockSpec((tm, d), lambda i: (i, 0)),
        out_shape=jax.ShapeDtypeStruct((rows, d), F32),
        compiler_params=_params("parallel"),
        name="out_proj",
    )(x2, a2, w, mods.table)


def kernel(x, c, ctx, c_ctx, w_ada, b_ada, norm_g, w_ffn_in, w_ffn_out, w_fourier_out, w_qkv,
           q_norm_g, k_norm_g, w_attn_out):
    b, n, d = x.shape
    l = ctx.shape[1]
    depth = w_ada.shape[0]
    assert depth == 2 and b <= 8, "layer 0 = Fourier mixer, layer 1 = attention mixer"
    q_dim = w_attn_out.shape[1]
    n_heads = q_dim // HEAD_DIM

    s16 = jnp.zeros((MOD_ROWS, d), F32).at[:b].set(jax.nn.silu(c)).at[8].set(jax.nn.silu(c_ctx))
    table = _ada(s16.astype(BF16), w_ada, b_ada).reshape(depth * MOD_ROWS * N_MOD, 1, d)

    f = w_ffn_out.shape[2]
    w_in = w_ffn_in.astype(BF16).reshape(depth * 2, d, 2 * f)
    w_out = w_ffn_out.astype(BF16).reshape(depth * 2, f, d)
    ng = norm_g.reshape(depth * 3, 1, d)
    w_four = w_fourier_out.astype(BF16)
    w_qkv_b = w_qkv.astype(BF16)
    w_o = w_attn_out.astype(BF16)

    gd = d // FOURIER_GROUPS
    cos_c, sin_c = _dft_tables(gd)
    chan_tab = jnp.concatenate([cos_c, -sin_c], axis=0)

    xl = x.reshape(b * n, d)
    xc = ctx.reshape(b * l, d)

    ml = _Mods(table, 0, n, ctx=False)
    mc = _Mods(table, 0, l, ctx=True)
    xl = _ffn(xl, ng, 0, ml, 0, w_in, w_out, 0)
    xc = _ffn(xc, ng, 0, mc, 0, w_in, w_out, 0)
    xl = _fourier_layer(xl.reshape(b, n, d), ng, 1, ml, w_four[0], chan_tab).reshape(b * n, d)
    xc = _fourier_layer(xc.reshape(b, l, d), ng, 1, mc, w_four[0], chan_tab).reshape(b * l, d)
    xl = _ffn(xl, ng, 2, ml, 6, w_in, w_out, 1)
    xc = _ffn(xc, ng, 2, mc, 6, w_in, w_out, 1)

    ml = _Mods(table, 1, n, ctx=False)
    mc = _Mods(table, 1, l, ctx=True)
    xl = _ffn(xl, ng, 3, ml, 0, w_in, w_out, 2)
    xc = _ffn(xc, ng, 3, mc, 0, w_in, w_out, 2)
    rope = _rope_tables(n)
    q, kl, vl = _qkv_proj(xl, ng, 4, ml, 3, w_qkv_b[0], q_norm_g[0], k_norm_g[0], rope, n_heads, n)
    kc, vc = _qkv_proj(xc, ng, 4, mc, 3, w_qkv_b[0], q_norm_g[0], k_norm_g[0], None, 0, l)
    kv_dim = N_KV_HEADS * HEAD_DIM
    o = _attention(q.reshape(b, n, q_dim), kc.reshape(b, l, kv_dim), vc.reshape(b, l, kv_dim),
                   kl.reshape(b, n, kv_dim), vl.reshape(b, n, kv_dim))
    xl = _out_proj(xl, o.reshape(b * n, q_dim), w_o[0], ml, 5)
    xl = _ffn(xl, ng, 5, ml, 6, w_in, w_out, 3)
    return xl.reshape(b, n, d)
```

```python
import functools
import math

import jax
import jax.numpy as jnp
from jax import lax
from jax.experimental import pallas as pl
from jax.experimental.pallas import tpu as pltpu

F32 = jnp.float32
BF16 = jnp.bfloat16

GRID_W = 64
FOURIER_GROUPS = 8
HEAD_DIM = 128
N_KV_HEADS = 4
ROPE_FREQS = HEAD_DIM // 4
ROPE_THETA = 10000.0
N_MOD = 9
EPS = 1e-6
MOD_ROWS = 16
DFT_TABLE_SPLIT = 64
ATTN_SUB_ROWS = 256
MOD_CHUNK_ROWS = 16
FFT_CHUNK = 16
FFT_SLABS = 16

VMEM_LIMIT_BYTES = 62 * 1024 * 1024


def _params(*sem):
    return pltpu.CompilerParams(dimension_semantics=sem, vmem_limit_bytes=VMEM_LIMIT_BYTES)


def _modulate(x, g, shift, scale):
    ms = jnp.mean(x * x, axis=-1, keepdims=True)
    return (x * lax.rsqrt(ms + EPS)) * g * (1.0 + scale) + shift


def _ada_kernel(s_ref, w_ref, b_ref, o_ref):
    w = w_ref[0].astype(BF16)
    o_ref[0] = jnp.dot(s_ref[...], w, preferred_element_type=F32) + b_ref[0]


def _ada(s16, w_ada, b_ada, tn=1024):
    depth, d, n = w_ada.shape
    return pl.pallas_call(
        _ada_kernel,
        grid=(depth, n // tn),
        in_specs=[
            pl.BlockSpec((MOD_ROWS, d), lambda l, j: (0, 0)),
            pl.BlockSpec((1, d, tn), lambda l, j: (l, 0, j)),
            pl.BlockSpec((1, 1, tn), lambda l, j: (l, 0, j)),
        ],
        out_specs=pl.BlockSpec((1, MOD_ROWS, tn), lambda l, j: (l, 0, j)),
        out_shape=jax.ShapeDtypeStruct((depth, MOD_ROWS, n), F32),
        compiler_params=_params("parallel", "parallel"),
        name="ada",
    )(s16, w_ada, b_ada.reshape(depth, 1, n))


class _Mods:
    def __init__(self, table, layer, rows_per_mod, ctx):
        self.table, self.layer, self.rows_per_mod, self.ctx = table, layer, rows_per_mod, ctx

    def tile(self, tm, rows):
        tm = min(tm, rows if self.ctx else self.rows_per_mod)
        assert rows % tm == 0 and (self.ctx or self.rows_per_mod % tm == 0)
        return tm

    def spec(self, k, tm, d):
        layer, rpm, ctx = self.layer, self.rows_per_mod, self.ctx

        def index(i, *_):
            row = (MOD_ROWS - 8) if ctx else (i * tm) // rpm
            return ((layer * MOD_ROWS + row) * N_MOD + k, 0, 0)

        return pl.BlockSpec((1, 1, d), index)


def _ffn_kernel(x_ref, g_ref, sh_ref, sc_ref, gt_ref, wg_ref, wu_ref, wo_ref, o_ref, h_ref):
    @pl.when(pl.program_id(1) == 0)
    def _():
        gain = g_ref[0] * (1.0 + sc_ref[0])
        shift = sh_ref[0]

        def rows(c, carry):
            r = pl.ds(pl.multiple_of(c * MOD_CHUNK_ROWS, MOD_CHUNK_ROWS), MOD_CHUNK_ROWS)
            x = x_ref[r, :]
            ms = jnp.mean(x * x, axis=-1, keepdims=True)
            h_ref[r, :] = ((x * lax.rsqrt(ms + EPS)) * gain + shift).astype(BF16)
            o_ref[r, :] = x
            return carry

        lax.fori_loop(0, x_ref.shape[0] // MOD_CHUNK_ROWS, rows, 0, unroll=8)

    h = h_ref[...]
    gate = jnp.dot(h, wg_ref[...], preferred_element_type=F32)
    up = jnp.dot(h, wu_ref[...], preferred_element_type=F32)
    act = (jax.nn.silu(gate) * up).astype(BF16)
    o_ref[...] += (0.5 * gt_ref[0]) * jnp.dot(act, wo_ref[...], preferred_element_type=F32)


def _ffn(x2, norm_g, gi, mods, k0, w_in, w_out, wi, tm=1024, tf=512):
    rows, d = x2.shape
    f = w_out.shape[1]
    tm, tf = mods.tile(tm, rows), min(tf, f)
    nk = f // tf
    return pl.pallas_call(
        _ffn_kernel,
        grid=(rows // tm, nk),
        in_specs=[
            pl.BlockSpec((tm, d), lambda i, k: (i, 0)),
            pl.BlockSpec((1, 1, d), lambda i, k: (gi, 0, 0)),
            mods.spec(k0, tm, d), mods.spec(k0 + 1, tm, d), mods.spec(k0 + 2, tm, d),
            pl.BlockSpec((None, d, tf), lambda i, k: (wi, 0, k)),
            pl.BlockSpec((None, d, tf), lambda i, k: (wi, 0, nk + k)),
            pl.BlockSpec((None, tf, d), lambda i, k: (wi, k, 0)),
        ],
        out_specs=pl.BlockSpec((tm, d), lambda i, k: (i, 0)),
        out_shape=jax.ShapeDtypeStruct((rows, d), F32),
        scratch_shapes=[pltpu.VMEM((tm, d), BF16)],
        compiler_params=_params("parallel", "arbitrary"),
        name="ffn",
    )(x2, norm_g, mods.table, mods.table, mods.table, w_in, w_in, w_out)


def _mod_kernel(x_ref, g_ref, sh_ref, sc_ref, o_ref):
    o_ref[...] = _modulate(x_ref[...], g_ref[0], sh_ref[0], sc_ref[0]).astype(o_ref.dtype)


def _modulated(x2, norm_g, gi, mods, k0, tm=512):
    rows, d = x2.shape
    tm = mods.tile(tm, rows)
    return pl.pallas_call(
        _mod_kernel,
        grid=(rows // tm,),
        in_specs=[
            pl.BlockSpec((tm, d), lambda i: (i, 0)),
            pl.BlockSpec((1, 1, d), lambda i: (gi, 0, 0)),
            mods.spec(k0, tm, d), mods.spec(k0 + 1, tm, d),
        ],
        out_specs=pl.BlockSpec((tm, d), lambda i: (i, 0)),
        out_shape=jax.ShapeDtypeStruct((rows, d), BF16),
        compiler_params=_params("parallel"),
        name="modulate",
    )(x2, norm_g, mods.table, mods.table)


def _dft_tables(n):
    def unit(rows, cols):
        phase = (rows[:, None] * cols[None, :]) % n
        ang = phase.astype(F32) * (2.0 * math.pi / n)
        return jnp.cos(ang), jnp.sin(ang)

    idx = jnp.arange(n, dtype=jnp.int32)
    s = 1.0 / math.sqrt(n)
    m = DFT_TABLE_SPLIT
    if n <= m or n % m:
        c, sn = unit(idx, idx)
    else:
        ca, sa = unit(idx, jnp.arange(n // m, dtype=jnp.int32) * m)
        cb, sb = unit(idx, jnp.arange(m, dtype=jnp.int32))
        c = (ca[:, :, None] * cb[:, None, :] - sa[:, :, None] * sb[:, None, :]).reshape(n, n)
        sn = (sa[:, :, None] * cb[:, None, :] + ca[:, :, None] * sb[:, None, :]).reshape(n, n)
    return (c * s).astype(BF16), (sn * s).astype(BF16)


def _pos_dft_kernel(c_ref, s_ref, h_ref, zr_ref, zi_ref):
    h = h_ref[0]
    zr_ref[0] = jnp.dot(c_ref[...], h, preferred_element_type=F32).astype(BF16)
    zi_ref[0] = jnp.dot(s_ref[...], h, preferred_element_type=F32).astype(BF16)


def _pos_dft(h3, cos_n, sin_n, tk=256):
    b, n, d = h3.shape
    tk = min(tk, n)
    out = jax.ShapeDtypeStruct((b, n, d), BF16)
    return pl.pallas_call(
        _pos_dft_kernel,
        grid=(b, n // tk),
        in_specs=[
            pl.BlockSpec((tk, n), lambda bi, i: (i, 0)),
            pl.BlockSpec((tk, n), lambda bi, i: (i, 0)),
            pl.BlockSpec((1, n, d), lambda bi, i: (bi, 0, 0), pipeline_mode=pl.Buffered(1)),
        ],
        out_specs=[pl.BlockSpec((1, tk, d), lambda bi, i: (bi, i, 0))] * 2,
        out_shape=[out, out],
        compiler_params=_params("parallel", "parallel"),
        name="pos_dft",
    )(cos_n, sin_n, h3)


def _fft_tables(n):
    s1, r = FFT_SLABS, FFT_CHUNK
    s2 = n // s1
    assert s1 * s2 == n and s1 * r == s2 and s2 % r == 0
    ar = lambda m: jnp.arange(m, dtype=jnp.int32)
    j, k1, rr, n1 = ar(s2 // r)[:, None, None, None], ar(s1)[None, :, None, None], \
        ar(r)[None, None, :, None], ar(s1)[None, None, None, :]
    ang = ((k1 * (s2 * n1 + r * j + rr)) % n).astype(F32) * (2.0 * math.pi / n)
    eye = jnp.eye(r, dtype=F32)
    spread = lambda t: (t[..., None] * eye[None, None, :, None, :]).reshape(s2 // r, s1 * r, s1 * r)
    t1 = jnp.concatenate([spread(jnp.cos(ang)), spread(-jnp.sin(ang))], axis=1).astype(BF16)
    phi = ((ar(s2)[:, None] * ar(s2)[None, :]) % s2).astype(F32) * (2.0 * math.pi / s2)
    c, s = jnp.cos(phi) / math.sqrt(n), jnp.sin(phi) / math.sqrt(n)
    t2 = jnp.concatenate([jnp.concatenate([c, s], axis=1), jnp.concatenate([s, -c], axis=1)],
                         axis=0).astype(BF16)
    src = (ar(s1)[None, :] * r + ar(r)[:, None]).reshape(-1)
    perm = (src[:, None] == ar(s1 * r)[None, :]).astype(BF16)
    return t1, t2, perm


def _fft_kernel(t1_ref, t2_ref, p_ref, h_ref, zr_ref, zi_ref, b_ref, z_ref):
    s1, r = FFT_SLABS, FFT_CHUNK
    s2 = h_ref.shape[1] // s1
    for j in range(s2 // r):
        g = jnp.concatenate([h_ref[0, pl.ds(s2 * n1 + r * j, r), :] for n1 in range(s1)], axis=0)
        res = jnp.dot(t1_ref[j], g, preferred_element_type=F32).astype(BF16)
        for part in range(2):
            for k1 in range(s1):
                row = part * s2 + k1 * r
                b_ref[pl.ds(k1 * 2 * s2 + part * s2 + r * j, r), :] = res[row:row + r]
    for k1 in range(s1):
        rows = pl.ds(k1 * 2 * s2, 2 * s2)
        z_ref[rows, :] = jnp.dot(t2_ref[...], b_ref[rows, :], preferred_element_type=F32).astype(BF16)
    for part, out_ref in ((0, zr_ref), (1, zi_ref)):
        for jp in range(s2 // r):
            g = jnp.concatenate([z_ref[pl.ds(k1 * 2 * s2 + part * s2 + r * jp, r), :]
                                 for k1 in range(s1)], axis=0)
            out_ref[0, pl.ds(s1 * r * jp, s1 * r), :] = jnp.dot(
                p_ref[...], g, preferred_element_type=F32).astype(BF16)


def _pos_dft_two_stage(h3, tc=256):
    b, n, d = h3.shape
    tc = min(tc, d)
    t1, t2, perm = _fft_tables(n)
    const = lambda a: pl.BlockSpec(a.shape, lambda bi, ci: (0,) * a.ndim)
    io = pl.BlockSpec((1, n, tc), lambda bi, ci: (bi, 0, ci))
    out = jax.ShapeDtypeStruct((b, n, d), BF16)
    return pl.pallas_call(
        _fft_kernel,
        grid=(b, d // tc),
        in_specs=[const(t1), const(t2), const(perm), io],
        out_specs=[io, io],
        out_shape=[out, out],
        scratch_shapes=[pltpu.VMEM((2 * n, tc), BF16), pltpu.VMEM((2 * n, tc), BF16)],
        compiler_params=_params("parallel", "parallel"),
        name="pos_fft",
    )(t1, t2, perm, h3)


def _chan_kernel(x_ref, zr_ref, zi_ref, tab_ref, w_ref, gt_ref, o_ref, f_ref):
    gd = tab_ref.shape[1]
    tab = tab_ref[...]
    for grp in range(FOURIER_GROUPS):
        cols = slice(grp * gd, (grp + 1) * gd)
        z = jnp.concatenate([zr_ref[:, cols], zi_ref[:, cols]], axis=-1)
        f_ref[:, cols] = jnp.dot(z, tab, preferred_element_type=F32).astype(BF16)
    y = jnp.dot(f_ref[...], w_ref[...], preferred_element_type=F32)
    o_ref[...] = x_ref[...] + gt_ref[0] * y


def _chan_mix(x2, zr2, zi2, tab, w_out, mods, k_gate, tm=512):
    rows, d = x2.shape
    tm = mods.tile(tm, rows)
    return pl.pallas_call(
        _chan_kernel,
        grid=(rows // tm,),
        in_specs=[
            pl.BlockSpec((tm, d), lambda i: (i, 0)),
            pl.BlockSpec((tm, d), lambda i: (i, 0)),
            pl.BlockSpec((tm, d), lambda i: (i, 0)),
            pl.BlockSpec(tab.shape, lambda i: (0, 0)),
            pl.BlockSpec((d, d), lambda i: (0, 0)),
            mods.spec(k_gate, tm, d),
        ],
        out_specs=pl.BlockSpec((tm, d), lambda i: (i, 0)),
        out_shape=jax.ShapeDtypeStruct((rows, d), F32),
        scratch_shapes=[pltpu.VMEM((tm, d), BF16)],
        compiler_params=_params("parallel"),
        name="chan_mix",
    )(x2, zr2, zi2, tab, w_out, mods.table)


def _fourier_layer(x3, norm_g, gi, mods, w_out, chan_tab):
    b, n, d = x3.shape
    x2 = x3.reshape(b * n, d)
    h = _modulated(x2, norm_g, gi, mods, 3).reshape(b, n, d)
    if n == FFT_SLABS * FFT_SLABS * FFT_CHUNK:
        zr, zi = _pos_dft_two_stage(h)
    else:
        zr, zi = _pos_dft(h, *_dft_tables(n))
    out = _chan_mix(x2, zr.reshape(b * n, d), zi.reshape(b * n, d), chan_tab, w_out, mods, 5)
    return out.reshape(b, n, d)


def _rope_tables(n):
    pos = jnp.arange(n, dtype=jnp.int32)
    row = (pos // GRID_W).astype(F32)
    col = (pos % GRID_W).astype(F32)
    inv_freq = ROPE_THETA ** (-jnp.arange(ROPE_FREQS, dtype=F32) / ROPE_FREQS)
    a_r = row[:, None] * inv_freq
    a_c = col[:, None] * inv_freq
    ang = jnp.concatenate([a_r, a_r, a_c, a_c], axis=-1)
    cos, sin = jnp.cos(ang), jnp.sin(ang)
    first_half = (jnp.arange(HEAD_DIM) % (2 * ROPE_FREQS)) < ROPE_FREQS
    sin_from_upper = jnp.where(first_half[None, :], -sin, 0.0)
    sin_from_lower = jnp.where(first_half[None, :], 0.0, sin)
    return cos, sin_from_upper, sin_from_lower


def _head_norm(t, g):
    ms = jnp.mean(t * t, axis=-1, keepdims=True)
    return t * lax.rsqrt(ms + EPS) * g


def _rope(t, cos, s_up, s_lo):
    up = pltpu.roll(t, HEAD_DIM - ROPE_FREQS, 1)
    lo = pltpu.roll(t, ROPE_FREQS, 1)
    return t * cos + up * s_up + lo * s_lo


def _qkv_kernel(x_ref, g_ref, sh_ref, sc_ref, w_ref, qg_ref, kg_ref, *refs,
                n_q_heads, use_rope, q_scale):
    if use_rope:
        cos, s_up, s_lo = (r[...] for r in refs[:3])
        refs = refs[3:]
    q_ref = refs[0] if n_q_heads else None
    k_ref, v_ref = refs[-2:]
    h = _modulate(x_ref[...], g_ref[0], sh_ref[0], sc_ref[0]).astype(BF16)
    qkv = jnp.dot(h, w_ref[...], preferred_element_type=F32)
    for hd in range(n_q_heads + N_KV_HEADS):
        t = qkv[:, hd * HEAD_DIM:(hd + 1) * HEAD_DIM]
        is_q = hd < n_q_heads
        t = _head_norm(t, qg_ref[...] if is_q else kg_ref[...])
        if use_rope:
            t = _rope(t, cos, s_up, s_lo)
        if is_q:
            q_ref[:, hd * HEAD_DIM:(hd + 1) * HEAD_DIM] = (t * q_scale).astype(BF16)
        else:
            kh = hd - n_q_heads
            k_ref[:, kh * HEAD_DIM:(kh + 1) * HEAD_DIM] = t.astype(BF16)
    v0 = (n_q_heads + N_KV_HEADS) * HEAD_DIM
    v_ref[...] = qkv[:, v0:].astype(BF16)


def _qkv_proj(x2, norm_g, gi, mods, k0, w, q_g, k_g, rope, n_q_heads, seq, tm=512):
    rows, d = x2.shape
    tm = mods.tile(tm, rows)
    kv_dim = N_KV_HEADS * HEAD_DIM
    q_dim = n_q_heads * HEAD_DIM
    width = q_dim + 2 * kv_dim
    col_block, rem = divmod(w.shape[1] - width, width)
    assert rem == 0
    use_rope = rope is not None
    rope = rope if use_rope else ()
    tiles_per_seq = seq // tm
    kern = functools.partial(_qkv_kernel, n_q_heads=n_q_heads, use_rope=use_rope,
                             q_scale=math.log2(math.e) / math.sqrt(HEAD_DIM))
    head_vec = pl.BlockSpec((1, HEAD_DIM), lambda i: (0, 0))
    rope_spec = pl.BlockSpec((tm, HEAD_DIM), lambda i: (i % tiles_per_seq, 0))
    out_widths = ([q_dim] if n_q_heads else []) + [kv_dim, kv_dim]
    return pl.pallas_call(
        kern,
        grid=(rows // tm,),
        in_specs=[
            pl.BlockSpec((tm, d), lambda i: (i, 0)),
            pl.BlockSpec((1, 1, d), lambda i: (gi, 0, 0)),
            mods.spec(k0, tm, d), mods.spec(k0 + 1, tm, d),
            pl.BlockSpec((d, width), lambda i: (0, col_block)),
            head_vec, head_vec,
        ] + [rope_spec] * len(rope),
        out_specs=[pl.BlockSpec((tm, wd), lambda i: (i, 0)) for wd in out_widths],
        out_shape=[jax.ShapeDtypeStruct((rows, wd), BF16) for wd in out_widths],
        compiler_params=_params("parallel"),
        name="qkv_proj",
    )(x2, norm_g, mods.table, mods.table, w,
      q_g.reshape(1, HEAD_DIM), k_g.reshape(1, HEAD_DIM), *rope)


def _attn_kernel(q_ref, kc_ref, vc_ref, kl_ref, vl_ref, o_ref, k_ref, v_ref, *, group):
    l = kc_ref.shape[1]

    @pl.when(pl.program_id(2) == 0)
    def _():
        k_ref[:l] = kc_ref[0]
        k_ref[l:] = kl_ref[0]
        v_ref[:l, :HEAD_DIM] = vc_ref[0]
        v_ref[l:, :HEAD_DIM] = vl_ref[0]
        v_ref[:, HEAD_DIM:] = jnp.ones((v_ref.shape[0], HEAD_DIM), BF16)

    nt = (((1,), (1,)), ((), ()))
    k, v = k_ref[...], v_ref[...]
    tq = q_ref.shape[1]
    sub = min(tq, ATTN_SUB_ROWS)
    for r0 in range(0, tq, sub):
        rows = slice(r0, r0 + sub)
        for gi in range(group):
            cols = slice(gi * HEAD_DIM, (gi + 1) * HEAD_DIM)
            s = lax.dot_general(q_ref[0, rows, cols], k, nt, preferred_element_type=F32)
            p = jnp.exp2(s - jnp.max(s, axis=-1, keepdims=True)).astype(BF16)
            oe = jnp.dot(p, v, preferred_element_type=F32)
            o_ref[0, rows, cols] = (oe[:, :HEAD_DIM] / oe[:, HEAD_DIM:HEAD_DIM + 1]).astype(BF16)


def _attention(q3, kc3, vc3, kl3, vl3, tq=1024):
    b, s, qd = q3.shape
    tq = min(tq, s)
    l = kc3.shape[1]
    group = qd // (N_KV_HEADS * HEAD_DIM)
    gw = group * HEAD_DIM
    kv_spec = lambda n: pl.BlockSpec((1, n, HEAD_DIM), lambda bi, kh, i: (bi, 0, kh))
    return pl.pallas_call(
        functools.partial(_attn_kernel, group=group),
        grid=(b, N_KV_HEADS, s // tq),
        in_specs=[
            pl.BlockSpec((1, tq, gw), lambda bi, kh, i: (bi, i, kh)),
            kv_spec(l), kv_spec(l), kv_spec(s), kv_spec(s),
        ],
        out_specs=pl.BlockSpec((1, tq, gw), lambda bi, kh, i: (bi, i, kh)),
        out_shape=jax.ShapeDtypeStruct((b, s, qd), BF16),
        scratch_shapes=[pltpu.VMEM((l + s, HEAD_DIM), BF16), pltpu.VMEM((l + s, 2 * HEAD_DIM), BF16)],
        compiler_params=_params("parallel", "parallel", "arbitrary"),
        name="attention",
    )(q3, kc3, vc3, kl3, vl3)


def _oproj_kernel(x_ref, a_ref, w_ref, gt_ref, o_ref):
    y = jnp.dot(a_ref[...], w_ref[...], preferred_element_type=F32)
    o_ref[...] = x_ref[...] + gt_ref[0] * y


def _out_proj(x2, a2, w, mods, k_gate, tm=512):
    rows, d = x2.shape
    tm = mods.tile(tm, rows)
    return pl.pallas_call(
        _oproj_kernel,
        grid=(rows // tm,),
        in_specs=[
            pl.BlockSpec((tm, d), lambda i: (i, 0)),
            pl.BlockSpec((tm, a2.shape[1]), lambda i: (i, 0)),
            pl.BlockSpec(w.shape, lambda i: (0, 0)),
            mods.spec(k_gate, tm, d),
        ],
        out_specs=pl.BlockSpec((tm, d), lambda i: (i, 0)),
        out_shape=jax.ShapeDtypeStruct((rows, d), F32),
        compiler_params=_params("parallel"),
        name="out_proj",
    )(x2, a2, w, mods.table)


def kernel(x, c, ctx, c_ctx, w_ada, b_ada, norm_g, w_ffn_in, w_ffn_out, w_fourier_out, w_qkv,
           q_norm_g, k_norm_g, w_attn_out):
    b, n, d = x.shape
    l = ctx.shape[1]
    depth = w_ada.shape[0]
    assert depth == 2 and b <= 8, "layer 0 = Fourier mixer, layer 1 = attention mixer"
    q_dim = w_attn_out.shape[1]
    n_heads = q_dim // HEAD_DIM

    s16 = jnp.zeros((MOD_ROWS, d), F32).at[:b].set(jax.nn.silu(c)).at[8].set(jax.nn.silu(c_ctx))
    table = _ada(s16.astype(BF16), w_ada, b_ada).reshape(depth * MOD_ROWS * N_MOD, 1, d)

    f = w_ffn_out.shape[2]
    w_in = w_ffn_in.astype(BF16).reshape(depth * 2, d, 2 * f)
    w_out = w_ffn_out.astype(BF16).reshape(depth * 2, f, d)
    ng = norm_g.reshape(depth * 3, 1, d)
    w_four = w_fourier_out.astype(BF16)
    w_qkv_b = w_qkv.astype(BF16)
    w_o = w_attn_out.astype(BF16)

    gd = d // FOURIER_GROUPS
    cos_c, sin_c = _dft_tables(gd)
    chan_tab = jnp.concatenate([cos_c, -sin_c], axis=0)

    xl = x.reshape(b * n, d)
    xc = ctx.reshape(b * l, d)

    ml = _Mods(table, 0, n, ctx=False)
    mc = _Mods(table, 0, l, ctx=True)
    xl = _ffn(xl, ng, 0, ml, 0, w_in, w_out, 0)
    xc = _ffn(xc, ng, 0, mc, 0, w_in, w_out, 0)
    xl = _fourier_layer(xl.reshape(b, n, d), ng, 1, ml, w_four[0], chan_tab).reshape(b * n, d)
    xc = _fourier_layer(xc.reshape(b, l, d), ng, 1, mc, w_four[0], chan_tab).reshape(b * l, d)
    xl = _ffn(xl, ng, 2, ml, 6, w_in, w_out, 1)
    xc = _ffn(xc, ng, 2, mc, 6, w_in, w_out, 1)

    ml = _Mods(table, 1, n, ctx=False)
    mc = _Mods(table, 1, l, ctx=True)
    xl = _ffn(xl, ng, 3, ml, 0, w_in, w_out, 2)
    xc = _ffn(xc, ng, 3, mc, 0, w_in, w_out, 2)
    rope = _rope_tables(n)
    q, kl, vl = _qkv_proj(xl, ng, 4, ml, 3, w_qkv_b[0], q_norm_g[0], k_norm_g[0], rope, n_heads, n)
    kc, vc = _qkv_proj(xc, ng, 4, mc, 3, w_qkv_b[0], q_norm_g[0], k_norm_g[0], None, 0, l)
    kv_dim = N_KV_HEADS * HEAD_DIM
    o = _attention(q.reshape(b, n, q_dim), kc.reshape(b, l, kv_dim), vc.reshape(b, l, kv_dim),
                   kl.reshape(b, n, kv_dim), vl.reshape(b, n, kv_dim))
    xl = _out_proj(xl, o.reshape(b * n, q_dim), w_o[0], ml, 5)
    xl = _ffn(xl, ng, 5, ml, 6, w_in, w_out, 3)
    return xl.reshape(b, n, d)
```

```python
import functools
import math

import jax
import jax.numpy as jnp
from jax import lax
from jax.experimental import pallas as pl
from jax.experimental.pallas import tpu as pltpu

F32 = jnp.float32
BF16 = jnp.bfloat16

GRID_W = 64
FOURIER_GROUPS = 8
HEAD_DIM = 128
N_KV_HEADS = 4
ROPE_FREQS = HEAD_DIM // 4
ROPE_THETA = 10000.0
N_MOD = 9
EPS = 1e-6
MOD_ROWS = 16
DFT_TABLE_SPLIT = 64
ATTN_SUB_ROWS = 256
MOD_CHUNK_ROWS = 16
FFT_CHUNK = 16
FFT_SLABS = 16

VMEM_LIMIT_BYTES = 62 * 1024 * 1024


def _params(*sem):
    return pltpu.CompilerParams(dimension_semantics=sem, vmem_limit_bytes=VMEM_LIMIT_BYTES)


def _modulate(x, g, shift, scale):
    ms = jnp.mean(x * x, axis=-1, keepdims=True)
    return (x * lax.rsqrt(ms + EPS)) * g * (1.0 + scale) + shift


def _ada_kernel(s_ref, w_ref, b_ref, o_ref):
    w = w_ref[0].astype(BF16)
    o_ref[0] = jnp.dot(s_ref[...], w, preferred_element_type=F32) + b_ref[0]


def _ada(s16, w_ada, b_ada, tn=1024):
    depth, d, n = w_ada.shape
    return pl.pallas_call(
        _ada_kernel,
        grid=(depth, n // tn),
        in_specs=[
            pl.BlockSpec((MOD_ROWS, d), lambda l, j: (0, 0)),
            pl.BlockSpec((1, d, tn), lambda l, j: (l, 0, j)),
            pl.BlockSpec((1, 1, tn), lambda l, j: (l, 0, j)),
        ],
        out_specs=pl.BlockSpec((1, MOD_ROWS, tn), lambda l, j: (l, 0, j)),
        out_shape=jax.ShapeDtypeStruct((depth, MOD_ROWS, n), F32),
        compiler_params=_params("parallel", "parallel"),
        name="ada",
    )(s16, w_ada, b_ada.reshape(depth, 1, n))


class _Mods:
    def __init__(self, table, layer, rows_per_mod, ctx):
        self.table, self.layer, self.rows_per_mod, self.ctx = table, layer, rows_per_mod, ctx

    def tile(self, tm, rows):
        tm = min(tm, rows if self.ctx else self.rows_per_mod)
        assert rows % tm == 0 and (self.ctx or self.rows_per_mod % tm == 0)
        return tm

    def spec(self, k, tm, d):
        layer, rpm, ctx = self.layer, self.rows_per_mod, self.ctx

        def index(i, *_):
            row = (MOD_ROWS - 8) if ctx else (i * tm) // rpm
            return ((layer * MOD_ROWS + row) * N_MOD + k, 0, 0)

        return pl.BlockSpec((1, 1, d), index)


def _ffn_kernel(x_ref, g_ref, sh_ref, sc_ref, gt_ref, wg_ref, wu_ref, wo_ref, *refs):
    if len(refs) == 2:
        o_ref, h_ref = refs
    else:
        next_in_ref, next_out_ref, o_ref, next_in_bf_ref, next_out_bf_ref, h_ref = refs
        next_in_bf_ref[...] = next_in_ref[...].astype(BF16)
        next_out_bf_ref[...] = next_out_ref[...].astype(BF16)

    @pl.when(pl.program_id(1) == 0)
    def _():
        gain = g_ref[0] * (1.0 + sc_ref[0])
        shift = sh_ref[0]

        def rows(c, carry):
            r = pl.ds(pl.multiple_of(c * MOD_CHUNK_ROWS, MOD_CHUNK_ROWS), MOD_CHUNK_ROWS)
            x = x_ref[r, :]
            ms = jnp.mean(x * x, axis=-1, keepdims=True)
            h_ref[r, :] = ((x * lax.rsqrt(ms + EPS)) * gain + shift).astype(BF16)
            o_ref[r, :] = x
            return carry

        lax.fori_loop(0, x_ref.shape[0] // MOD_CHUNK_ROWS, rows, 0, unroll=8)

    h = h_ref[...]
    gate = jnp.dot(h, wg_ref[...], preferred_element_type=F32)
    up = jnp.dot(h, wu_ref[...], preferred_element_type=F32)
    act = (jax.nn.silu(gate) * up).astype(BF16)
    o_ref[...] += (0.5 * gt_ref[0]) * jnp.dot(act, wo_ref[...], preferred_element_type=F32)


def _ffn(x2, norm_g, gi, mods, k0, w_in, w_out, cast_next=None, tm=1024, tf=512):
    rows, d = x2.shape
    f = w_out.shape[0]
    tm, tf = mods.tile(tm, rows), min(tf, f)
    ni, nk = rows // tm, f // tf
    in_specs = [
        pl.BlockSpec((tm, d), lambda i, k: (i, 0)),
        pl.BlockSpec((1, 1, d), lambda i, k: (gi, 0, 0)),
        mods.spec(k0, tm, d), mods.spec(k0 + 1, tm, d), mods.spec(k0 + 2, tm, d),
        pl.BlockSpec((d, tf), lambda i, k: (0, k)),
        pl.BlockSpec((d, tf), lambda i, k: (0, nk + k)),
        pl.BlockSpec((tf, d), lambda i, k: (k, 0)),
    ]
    out_specs = [pl.BlockSpec((tm, d), lambda i, k: (i, 0))]
    out_shape = [jax.ShapeDtypeStruct((rows, d), F32)]
    operands = [x2, norm_g, mods.table, mods.table, mods.table, w_in, w_in, w_out]
    if cast_next is not None:
        w_in_all, w_out_all, slab = cast_next
        ri, ci, ro = d // ni, 2 * f // nk, f // (ni * nk)
        assert ri * ni == d and ci * nk == 2 * f and ro * ni * nk == f
        in_specs += [pl.BlockSpec((None, ri, ci), lambda i, k: (slab, i, k)),
                     pl.BlockSpec((None, ro, d), lambda i, k: (slab, i * nk + k, 0))]
        out_specs += [pl.BlockSpec((ri, ci), lambda i, k: (i, k)),
                      pl.BlockSpec((ro, d), lambda i, k: (i * nk + k, 0))]
        out_shape += [jax.ShapeDtypeStruct((d, 2 * f), BF16), jax.ShapeDtypeStruct((f, d), BF16)]
        operands += [w_in_all, w_out_all]
    res = pl.pallas_call(
        _ffn_kernel,
        grid=(ni, nk),
        in_specs=in_specs,
        out_specs=out_specs,
        out_shape=out_shape,
        scratch_shapes=[pltpu.VMEM((tm, d), BF16)],
        compiler_params=_params("parallel", "arbitrary"),
        name="ffn",
    )(*operands)
    return res[0] if cast_next is None else res


def _mod_kernel(x_ref, g_ref, sh_ref, sc_ref, o_ref):
    o_ref[...] = _modulate(x_ref[...], g_ref[0], sh_ref[0], sc_ref[0]).astype(o_ref.dtype)


def _modulated(x2, norm_g, gi, mods, k0, tm=512):
    rows, d = x2.shape
    tm = mods.tile(tm, rows)
    return pl.pallas_call(
        _mod_kernel,
        grid=(rows // tm,),
        in_specs=[
            pl.BlockSpec((tm, d), lambda i: (i, 0)),
            pl.BlockSpec((1, 1, d), lambda i: (gi, 0, 0)),
            mods.spec(k0, tm, d), mods.spec(k0 + 1, tm, d),
        ],
        out_specs=pl.BlockSpec((tm, d), lambda i: (i, 0)),
        out_shape=jax.ShapeDtypeStruct((rows, d), BF16),
        compiler_params=_params("parallel"),
        name="modulate",
    )(x2, norm_g, mods.table, mods.table)


def _dft_tables(n):
    def unit(rows, cols):
        phase = (rows[:, None] * cols[None, :]) % n
        ang = phase.astype(F32) * (2.0 * math.pi / n)
        return jnp.cos(ang), jnp.sin(ang)

    idx = jnp.arange(n, dtype=jnp.int32)
    s = 1.0 / math.sqrt(n)
    m = DFT_TABLE_SPLIT
    if n <= m or n % m:
        c, sn = unit(idx, idx)
    else:
        ca, sa = unit(idx, jnp.arange(n // m, dtype=jnp.int32) * m)
        cb, sb = unit(idx, jnp.arange(m, dtype=jnp.int32))
        c = (ca[:, :, None] * cb[:, None, :] - sa[:, :, None] * sb[:, None, :]).reshape(n, n)
        sn = (sa[:, :, None] * cb[:, None, :] + ca[:, :, None] * sb[:, None, :]).reshape(n, n)
    return (c * s).astype(BF16), (sn * s).astype(BF16)


def _pos_dft_kernel(c_ref, s_ref, h_ref, zr_ref, zi_ref):
    h = h_ref[0]
    zr_ref[0] = jnp.dot(c_ref[...], h, preferred_element_type=F32).astype(BF16)
    zi_ref[0] = jnp.dot(s_ref[...], h, preferred_element_type=F32).astype(BF16)


def _pos_dft(h3, cos_n, sin_n, tk=256):
    b, n, d = h3.shape
    tk = min(tk, n)
    out = jax.ShapeDtypeStruct((b, n, d), BF16)
    return pl.pallas_call(
        _pos_dft_kernel,
        grid=(b, n // tk),
        in_specs=[
            pl.BlockSpec((tk, n), lambda bi, i: (i, 0)),
            pl.BlockSpec((tk, n), lambda bi, i: (i, 0)),
            pl.BlockSpec((1, n, d), lambda bi, i: (bi, 0, 0), pipeline_mode=pl.Buffered(1)),
        ],
        out_specs=[pl.BlockSpec((1, tk, d), lambda bi, i: (bi, i, 0))] * 2,
        out_shape=[out, out],
        compiler_params=_params("parallel", "parallel"),
        name="pos_dft",
    )(cos_n, sin_n, h3)


def _fft_tables(n):
    s1, r = FFT_SLABS, FFT_CHUNK
    s2 = n // s1
    assert s1 * s2 == n and s1 * r == s2 and s2 % r == 0
    ar = lambda m: jnp.arange(m, dtype=jnp.int32)
    j, k1, rr, n1 = ar(s2 // r)[:, None, None, None], ar(s1)[None, :, None, None], \
        ar(r)[None, None, :, None], ar(s1)[None, None, None, :]
    ang = ((k1 * (s2 * n1 + r * j + rr)) % n).astype(F32) * (2.0 * math.pi / n)
    eye = jnp.eye(r, dtype=F32)
    spread = lambda t: (t[..., None] * eye[None, None, :, None, :]).reshape(s2 // r, s1 * r, s1 * r)
    t1 = jnp.concatenate([spread(jnp.cos(ang)), spread(-jnp.sin(ang))], axis=1).astype(BF16)
    phi = ((ar(s2)[:, None] * ar(s2)[None, :]) % s2).astype(F32) * (2.0 * math.pi / s2)
    c, s = jnp.cos(phi) / math.sqrt(n), jnp.sin(phi) / math.sqrt(n)
    t2 = jnp.concatenate([jnp.concatenate([c, s], axis=1), jnp.concatenate([s, -c], axis=1)],
                         axis=0).astype(BF16)
    src = (ar(s1)[None, :] * r + ar(r)[:, None]).reshape(-1)
    perm = (src[:, None] == ar(s1 * r)[None, :]).astype(BF16)
    return t1, t2, perm


def _fft_kernel(t1_ref, t2_ref, p_ref, h_ref, zr_ref, zi_ref, b_ref, z_ref):
    s1, r = FFT_SLABS, FFT_CHUNK
    s2 = h_ref.shape[1] // s1
    for j in range(s2 // r):
        g = jnp.concatenate([h_ref[0, pl.ds(s2 * n1 + r * j, r), :] for n1 in range(s1)], axis=0)
        res = jnp.dot(t1_ref[j], g, preferred_element_type=F32).astype(BF16)
        for part in range(2):
            for k1 in range(s1):
                row = part * s2 + k1 * r
                b_ref[pl.ds(k1 * 2 * s2 + part * s2 + r * j, r), :] = res[row:row + r]
    for k1 in range(s1):
        rows = pl.ds(k1 * 2 * s2, 2 * s2)
        z_ref[rows, :] = jnp.dot(t2_ref[...], b_ref[rows, :], preferred_element_type=F32).astype(BF16)
    for part, out_ref in ((0, zr_ref), (1, zi_ref)):
        for jp in range(s2 // r):
            g = jnp.concatenate([z_ref[pl.ds(k1 * 2 * s2 + part * s2 + r * jp, r), :]
                                 for k1 in range(s1)], axis=0)
            out_ref[0, pl.ds(s1 * r * jp, s1 * r), :] = jnp.dot(
                p_ref[...], g, preferred_element_type=F32).astype(BF16)


def _pos_dft_two_stage(h3, tc=256):
    b, n, d = h3.shape
    tc = min(tc, d)
    t1, t2, perm = _fft_tables(n)
    const = lambda a: pl.BlockSpec(a.shape, lambda bi, ci: (0,) * a.ndim)
    io = pl.BlockSpec((1, n, tc), lambda bi, ci: (bi, 0, ci))
    out = jax.ShapeDtypeStruct((b, n, d), BF16)
    return pl.pallas_call(
        _fft_kernel,
        grid=(b, d // tc),
        in_specs=[const(t1), const(t2), const(perm), io],
        out_specs=[io, io],
        out_shape=[out, out],
        scratch_shapes=[pltpu.VMEM((2 * n, tc), BF16), pltpu.VMEM((2 * n, tc), BF16)],
        compiler_params=_params("parallel", "parallel"),
        name="pos_fft",
    )(t1, t2, perm, h3)


def _chan_kernel(x_ref, zr_ref, zi_ref, tab_ref, w_ref, gt_ref, o_ref, f_ref):
    gd = tab_ref.shape[1]
    tab = tab_ref[...]
    for grp in range(FOURIER_GROUPS):
        cols = slice(grp * gd, (grp + 1) * gd)
        z = jnp.concatenate([zr_ref[:, cols], zi_ref[:, cols]], axis=-1)
        f_ref[:, cols] = jnp.dot(z, tab, preferred_element_type=F32).astype(BF16)
    y = jnp.dot(f_ref[...], w_ref[...], preferred_element_type=F32)
    o_ref[...] = x_ref[...] + gt_ref[0] * y


def _chan_mix(x2, zr2, zi2, tab, w_out, mods, k_gate, tm=512):
    rows, d = x2.shape
    tm = mods.tile(tm, rows)
    return pl.pallas_call(
        _chan_kernel,
        grid=(rows // tm,),
        in_specs=[
            pl.BlockSpec((tm, d), lambda i: (i, 0)),
            pl.BlockSpec((tm, d), lambda i: (i, 0)),
            pl.BlockSpec((tm, d), lambda i: (i, 0)),
            pl.BlockSpec(tab.shape, lambda i: (0, 0)),
            pl.BlockSpec((d, d), lambda i: (0, 0)),
            mods.spec(k_gate, tm, d),
        ],
        out_specs=pl.BlockSpec((tm, d), lambda i: (i, 0)),
        out_shape=jax.ShapeDtypeStruct((rows, d), F32),
        scratch_shapes=[pltpu.VMEM((tm, d), BF16)],
        compiler_params=_params("parallel"),
        name="chan_mix",
    )(x2, zr2, zi2, tab, w_out, mods.table)


def _fourier_layer(x3, norm_g, gi, mods, w_out, chan_tab):
    b, n, d = x3.shape
    x2 = x3.reshape(b * n, d)
    h = _modulated(x2, norm_g, gi, mods, 3).reshape(b, n, d)
    if n == FFT_SLABS * FFT_SLABS * FFT_CHUNK:
        zr, zi = _pos_dft_two_stage(h)
    else:
        zr, zi = _pos_dft(h, *_dft_tables(n))
    out = _chan_mix(x2, zr.reshape(b * n, d), zi.reshape(b * n, d), chan_tab, w_out, mods, 5)
    return out.reshape(b, n, d)


def _rope_tables(n):
    pos = jnp.arange(n, dtype=jnp.int32)
    row = (pos // GRID_W).astype(F32)
    col = (pos % GRID_W).astype(F32)
    inv_freq = ROPE_THETA ** (-jnp.arange(ROPE_FREQS, dtype=F32) / ROPE_FREQS)
    a_r = row[:, None] * inv_freq
    a_c = col[:, None] * inv_freq
    ang = jnp.concatenate([a_r, a_r, a_c, a_c], axis=-1)
    cos, sin = jnp.cos(ang), jnp.sin(ang)
    first_half = (jnp.arange(HEAD_DIM) % (2 * ROPE_FREQS)) < ROPE_FREQS
    sin_from_upper = jnp.where(first_half[None, :], -sin, 0.0)
    sin_from_lower = jnp.where(first_half[None, :], 0.0, sin)
    return cos, sin_from_upper, sin_from_lower


def _head_norm(t, g):
    ms = jnp.mean(t * t, axis=-1, keepdims=True)
    return t * lax.rsqrt(ms + EPS) * g


def _rope(t, cos, s_up, s_lo):
    up = pltpu.roll(t, HEAD_DIM - ROPE_FREQS, 1)
    lo = pltpu.roll(t, ROPE_FREQS, 1)
    return t * cos + up * s_up + lo * s_lo


def _qkv_kernel(x_ref, g_ref, sh_ref, sc_ref, w_ref, qg_ref, kg_ref, *refs,
                n_q_heads, use_rope, q_scale):
    if use_rope:
        cos, s_up, s_lo = (r[...] for r in refs[:3])
        refs = refs[3:]
    q_ref = refs[0] if n_q_heads else None
    k_ref, v_ref = refs[-2:]
    h = _modulate(x_ref[...], g_ref[0], sh_ref[0], sc_ref[0]).astype(BF16)
    gw = N_KV_HEADS * HEAD_DIM
    n_q_groups = n_q_heads // N_KV_HEADS
    for grp in range(n_q_groups + 2):
        y = jnp.dot(h, w_ref[:, grp * gw:(grp + 1) * gw], preferred_element_type=F32)
        if grp == n_q_groups + 1:
            v_ref[...] = y.astype(BF16)
            continue
        is_q = grp < n_q_groups
        gain = (qg_ref[...] * q_scale) if is_q else kg_ref[...]
        for hd in range(N_KV_HEADS):
            t = _head_norm(y[:, hd * HEAD_DIM:(hd + 1) * HEAD_DIM], gain)
            if use_rope:
                t = _rope(t, cos, s_up, s_lo)
            if is_q:
                c0 = grp * gw + hd * HEAD_DIM
                q_ref[:, c0:c0 + HEAD_DIM] = t.astype(BF16)
            else:
                k_ref[:, hd * HEAD_DIM:(hd + 1) * HEAD_DIM] = t.astype(BF16)


def _qkv_proj(x2, norm_g, gi, mods, k0, w, q_g, k_g, rope, n_q_heads, seq, tm=512):
    rows, d = x2.shape
    tm = mods.tile(tm, rows)
    kv_dim = N_KV_HEADS * HEAD_DIM
    q_dim = n_q_heads * HEAD_DIM
    width = q_dim + 2 * kv_dim
    col_block, rem = divmod(w.shape[1] - width, width)
    assert rem == 0
    use_rope = rope is not None
    rope = rope if use_rope else ()
    tiles_per_seq = seq // tm
    kern = functools.partial(_qkv_kernel, n_q_heads=n_q_heads, use_rope=use_rope,
                             q_scale=math.log2(math.e) / math.sqrt(HEAD_DIM))
    head_vec = pl.BlockSpec((1, HEAD_DIM), lambda i: (0, 0))
    rope_spec = pl.BlockSpec((tm, HEAD_DIM), lambda i: (i % tiles_per_seq, 0))
    out_widths = ([q_dim] if n_q_heads else []) + [kv_dim, kv_dim]
    return pl.pallas_call(
        kern,
        grid=(rows // tm,),
        in_specs=[
            pl.BlockSpec((tm, d), lambda i: (i, 0)),
            pl.BlockSpec((1, 1, d), lambda i: (gi, 0, 0)),
            mods.spec(k0, tm, d), mods.spec(k0 + 1, tm, d),
            pl.BlockSpec((d, width), lambda i: (0, col_block)),
            head_vec, head_vec,
        ] + [rope_spec] * len(rope),
        out_specs=[pl.BlockSpec((tm, wd), lambda i: (i, 0)) for wd in out_widths],
        out_shape=[jax.ShapeDtypeStruct((rows, wd), BF16) for wd in out_widths],
        compiler_params=_params("parallel"),
        name="qkv_proj",
    )(x2, norm_g, mods.table, mods.table, w,
      q_g.reshape(1, HEAD_DIM), k_g.reshape(1, HEAD_DIM), *rope)


def _attn_kernel(q_ref, kc_ref, vc_ref, kl_ref, vl_ref, o_ref, k_ref, v_ref, *, group):
    l = kc_ref.shape[1]

    @pl.when(pl.program_id(2) == 0)
    def _():
        k_ref[:l] = kc_ref[0]
        k_ref[l:] = kl_ref[0]
        v_ref[:l, :HEAD_DIM] = vc_ref[0]
        v_ref[l:, :HEAD_DIM] = vl_ref[0]
        v_ref[:, HEAD_DIM:] = jnp.ones((v_ref.shape[0], HEAD_DIM), BF16)

    nt = (((1,), (1,)), ((), ()))
    k, v = k_ref[...], v_ref[...]
    tq = q_ref.shape[1]
    sub = min(tq, ATTN_SUB_ROWS)
    for r0 in range(0, tq, sub):
        rows = slice(r0, r0 + sub)
        for gi in range(group):
            cols = slice(gi * HEAD_DIM, (gi + 1) * HEAD_DIM)
            s = lax.dot_general(q_ref[0, rows, cols], k, nt, preferred_element_type=F32)
            p = jnp.exp2(s - jnp.max(s, axis=-1, keepdims=True)).astype(BF16)
            oe = jnp.dot(p, v, preferred_element_type=F32)
            o_ref[0, rows, cols] = (oe[:, :HEAD_DIM] / oe[:, HEAD_DIM:HEAD_DIM + 1]).astype(BF16)


def _attention(q3, kc3, vc3, kl3, vl3, tq=2048):
    b, s, qd = q3.shape
    tq = min(tq, s)
    l = kc3.shape[1]
    group = qd // (N_KV_HEADS * HEAD_DIM)
    gw = group * HEAD_DIM
    kv_spec = lambda n: pl.BlockSpec((1, n, HEAD_DIM), lambda bi, kh, i: (bi, 0, kh))
    return pl.pallas_call(
        functools.partial(_attn_kernel, group=group),
        grid=(b, N_KV_HEADS, s // tq),
        in_specs=[
            pl.BlockSpec((1, tq, gw), lambda bi, kh, i: (bi, i, kh)),
            kv_spec(l), kv_spec(l), kv_spec(s), kv_spec(s),
        ],
        out_specs=pl.BlockSpec((1, tq, gw), lambda bi, kh, i: (bi, i, kh)),
        out_shape=jax.ShapeDtypeStruct((b, s, qd), BF16),
        scratch_shapes=[pltpu.VMEM((l + s, HEAD_DIM), BF16), pltpu.VMEM((l + s, 2 * HEAD_DIM), BF16)],
        compiler_params=_params("parallel", "parallel", "arbitrary"),
        name="attention",
    )(q3, kc3, vc3, kl3, vl3)


def _oproj_kernel(x_ref, a_ref, w_ref, gt_ref, o_ref):
    y = jnp.dot(a_ref[...], w_ref[...], preferred_element_type=F32)
    o_ref[...] = x_ref[...] + gt_ref[0] * y


def _out_proj(x2, a2, w, mods, k_gate, tm=512):
    rows, d = x2.shape
    tm = mods.tile(tm, rows)
    return pl.pallas_call(
        _oproj_kernel,
        grid=(rows // tm,),
        in_specs=[
            pl.BlockSpec((tm, d), lambda i: (i, 0)),
            pl.BlockSpec((tm, a2.shape[1]), lambda i: (i, 0)),
            pl.BlockSpec(w.shape, lambda i: (0, 0)),
            mods.spec(k_gate, tm, d),
        ],
        out_specs=pl.BlockSpec((tm, d), lambda i: (i, 0)),
        out_shape=jax.ShapeDtypeStruct((rows, d), F32),
        compiler_params=_params("parallel"),
        name="out_proj",
    )(x2, a2, w, mods.table)


def kernel(x, c, ctx, c_ctx, w_ada, b_ada, norm_g, w_ffn_in, w_ffn_out, w_fourier_out, w_qkv,
           q_norm_g, k_norm_g, w_attn_out):
    b, n, d = x.shape
    l = ctx.shape[1]
    depth = w_ada.shape[0]
    assert depth == 2 and b <= 8, "layer 0 = Fourier mixer, layer 1 = attention mixer"
    q_dim = w_attn_out.shape[1]
    n_heads = q_dim // HEAD_DIM

    s16 = jnp.zeros((MOD_ROWS, d), F32).at[:b].set(jax.nn.silu(c)).at[8].set(jax.nn.silu(c_ctx))
    table = _ada(s16.astype(BF16), w_ada, b_ada).reshape(depth * MOD_ROWS * N_MOD, 1, d)

    f = w_ffn_out.shape[2]
    w_in_all = w_ffn_in.reshape(depth * 2, d, 2 * f)
    w_out_all = w_ffn_out.reshape(depth * 2, f, d)
    w_in, w_out = w_in_all[0].astype(BF16), w_out_all[0].astype(BF16)
    ng = norm_g.reshape(depth * 3, 1, d)
    w_four = w_fourier_out.astype(BF16)
    w_qkv_b = w_qkv.astype(BF16)
    w_o = w_attn_out.astype(BF16)

    gd = d // FOURIER_GROUPS
    cos_c, sin_c = _dft_tables(gd)
    chan_tab = jnp.concatenate([cos_c, -sin_c], axis=0)

    xl = x.reshape(b * n, d)
    xc = ctx.reshape(b * l, d)

    ml = _Mods(table, 0, n, ctx=False)
    mc = _Mods(table, 0, l, ctx=True)
    xc = _ffn(xc, ng, 0, mc, 0, w_in, w_out)
    xl, w_in, w_out = _ffn(xl, ng, 0, ml, 0, w_in, w_out, cast_next=(w_in_all, w_out_all, 1))
    xl = _fourier_layer(xl.reshape(b, n, d), ng, 1, ml, w_four[0], chan_tab).reshape(b * n, d)
    xc = _fourier_layer(xc.reshape(b, l, d), ng, 1, mc, w_four[0], chan_tab).reshape(b * l, d)
    xc = _ffn(xc, ng, 2, mc, 6, w_in, w_out)
    xl, w_in, w_out = _ffn(xl, ng, 2, ml, 6, w_in, w_out, cast_next=(w_in_all, w_out_all, 2))

    ml = _Mods(table, 1, n, ctx=False)
    mc = _Mods(table, 1, l, ctx=True)
    xc = _ffn(xc, ng, 3, mc, 0, w_in, w_out)
    xl, w_in, w_out = _ffn(xl, ng, 3, ml, 0, w_in, w_out, cast_next=(w_in_all, w_out_all, 3))
    rope = _rope_tables(n)
    q, kl, vl = _qkv_proj(xl, ng, 4, ml, 3, w_qkv_b[0], q_norm_g[0], k_norm_g[0], rope, n_heads, n)
    kc, vc = _qkv_proj(xc, ng, 4, mc, 3, w_qkv_b[0], q_norm_g[0], k_norm_g[0], None, 0, l)
    kv_dim = N_KV_HEADS * HEAD_DIM
    o = _attention(q.reshape(b, n, q_dim), kc.reshape(b, l, kv_dim), vc.reshape(b, l, kv_dim),
                   kl.reshape(b, n, kv_dim), vl.reshape(b, n, kv_dim))
    xl = _out_proj(xl, o.reshape(b * n, q_dim), w_o[0], ml, 5)
    xl = _ffn(xl, ng, 5, ml, 6, w_in, w_out)
    return xl.reshape(b, n, d)
```

```python
import functools
import math

import jax
import jax.numpy as jnp
from jax import lax
from jax.experimental import pallas as pl
from jax.experimental.pallas import tpu as pltpu

F32 = jnp.float32
BF16 = jnp.bfloat16

GRID_W = 64
FOURIER_GROUPS = 8
HEAD_DIM = 128
N_KV_HEADS = 4
ROPE_FREQS = HEAD_DIM // 4
ROPE_THETA = 10000.0
N_MOD = 9
EPS = 1e-6
MOD_ROWS = 16
DFT_TABLE_SPLIT = 64
ATTN_SUB_ROWS = 256
MOD_CHUNK_ROWS = 16
FFT_CHUNK = 16
FFT_SLABS = 16

VMEM_LIMIT_BYTES = 62 * 1024 * 1024


def _params(*sem):
    return pltpu.CompilerParams(dimension_semantics=sem, vmem_limit_bytes=VMEM_LIMIT_BYTES)


def _modulate(x, g, shift, scale):
    ms = jnp.mean(x * x, axis=-1, keepdims=True)
    return (x * lax.rsqrt(ms + EPS)) * g * (1.0 + scale) + shift


def _ada_kernel(s_ref, w_ref, b_ref, o_ref):
    w = w_ref[0].astype(BF16)
    o_ref[0] = jnp.dot(s_ref[...], w, preferred_element_type=F32) + b_ref[0]


def _ada(s16, w_ada, b_ada, tn=1024):
    depth, d, n = w_ada.shape
    return pl.pallas_call(
        _ada_kernel,
        grid=(depth, n // tn),
        in_specs=[
            pl.BlockSpec((MOD_ROWS, d), lambda l, j: (0, 0)),
            pl.BlockSpec((1, d, tn), lambda l, j: (l, 0, j)),
            pl.BlockSpec((1, 1, tn), lambda l, j: (l, 0, j)),
        ],
        out_specs=pl.BlockSpec((1, MOD_ROWS, tn), lambda l, j: (l, 0, j)),
        out_shape=jax.ShapeDtypeStruct((depth, MOD_ROWS, n), F32),
        compiler_params=_params("parallel", "parallel"),
        name="ada",
    )(s16, w_ada, b_ada.reshape(depth, 1, n))


class _Mods:
    def __init__(self, table, layer, rows_per_mod, ctx):
        self.table, self.layer, self.rows_per_mod, self.ctx = table, layer, rows_per_mod, ctx

    def tile(self, tm, rows):
        tm = min(tm, rows if self.ctx else self.rows_per_mod)
        assert rows % tm == 0 and (self.ctx or self.rows_per_mod % tm == 0)
        return tm

    def spec(self, k, tm, d):
        layer, rpm, ctx = self.layer, self.rows_per_mod, self.ctx

        def index(i, *_):
            row = (MOD_ROWS - 8) if ctx else (i * tm) // rpm
            return ((layer * MOD_ROWS + row) * N_MOD + k, 0, 0)

        return pl.BlockSpec((1, 1, d), index)


def _ffn_kernel(x_ref, g_ref, sh_ref, sc_ref, gt_ref, wg_ref, wu_ref, wo_ref, *refs):
    if len(refs) == 2:
        o_ref, h_ref = refs
    else:
        next_in_ref, next_out_ref, o_ref, next_in_bf_ref, next_out_bf_ref, h_ref = refs
        next_in_bf_ref[...] = next_in_ref[...].astype(BF16)
        next_out_bf_ref[...] = next_out_ref[...].astype(BF16)

    @pl.when(pl.program_id(1) == 0)
    def _():
        gain = g_ref[0] * (1.0 + sc_ref[0])
        shift = sh_ref[0]

        def rows(c, carry):
            r = pl.ds(pl.multiple_of(c * MOD_CHUNK_ROWS, MOD_CHUNK_ROWS), MOD_CHUNK_ROWS)
            x = x_ref[r, :]
            ms = jnp.mean(x * x, axis=-1, keepdims=True)
            h_ref[r, :] = ((x * lax.rsqrt(ms + EPS)) * gain + shift).astype(BF16)
            o_ref[r, :] = x
            return carry

        lax.fori_loop(0, x_ref.shape[0] // MOD_CHUNK_ROWS, rows, 0, unroll=8)

    h = h_ref[...]
    gate = jnp.dot(h, wg_ref[...], preferred_element_type=F32)
    up = jnp.dot(h, wu_ref[...], preferred_element_type=F32)
    act = (jax.nn.silu(gate) * up).astype(BF16)
    o_ref[...] += (0.5 * gt_ref[0]) * jnp.dot(act, wo_ref[...], preferred_element_type=F32)


def _ffn(x2, norm_g, gi, mods, k0, w_in, w_out, cast_next=None, tm=1024, tf=512):
    rows, d = x2.shape
    f = w_out.shape[0]
    tm, tf = mods.tile(tm, rows), min(tf, f)
    ni, nk = rows // tm, f // tf
    in_specs = [
        pl.BlockSpec((tm, d), lambda i, k: (i, 0)),
        pl.BlockSpec((1, 1, d), lambda i, k: (gi, 0, 0)),
        mods.spec(k0, tm, d), mods.spec(k0 + 1, tm, d), mods.spec(k0 + 2, tm, d),
        pl.BlockSpec((d, tf), lambda i, k: (0, k)),
        pl.BlockSpec((d, tf), lambda i, k: (0, nk + k)),
        pl.BlockSpec((tf, d), lambda i, k: (k, 0)),
    ]
    out_specs = [pl.BlockSpec((tm, d), lambda i, k: (i, 0))]
    out_shape = [jax.ShapeDtypeStruct((rows, d), F32)]
    operands = [x2, norm_g, mods.table, mods.table, mods.table, w_in, w_in, w_out]
    if cast_next is not None:
        w_in_all, w_out_all, slab = cast_next
        ri, ci, ro = d // ni, 2 * f // nk, f // (ni * nk)
        assert ri * ni == d and ci * nk == 2 * f and ro * ni * nk == f
        in_specs += [pl.BlockSpec((None, ri, ci), lambda i, k: (slab, i, k)),
                     pl.BlockSpec((None, ro, d), lambda i, k: (slab, i * nk + k, 0))]
        out_specs += [pl.BlockSpec((ri, ci), lambda i, k: (i, k)),
                      pl.BlockSpec((ro, d), lambda i, k: (i * nk + k, 0))]
        out_shape += [jax.ShapeDtypeStruct((d, 2 * f), BF16), jax.ShapeDtypeStruct((f, d), BF16)]
        operands += [w_in_all, w_out_all]
    res = pl.pallas_call(
        _ffn_kernel,
        grid=(ni, nk),
        in_specs=in_specs,
        out_specs=out_specs,
        out_shape=out_shape,
        scratch_shapes=[pltpu.VMEM((tm, d), BF16)],
        compiler_params=_params("parallel", "arbitrary"),
        name="ffn",
    )(*operands)
    return res[0] if cast_next is None else res


def _mod_kernel(x_ref, g_ref, sh_ref, sc_ref, o_ref):
    o_ref[...] = _modulate(x_ref[...], g_ref[0], sh_ref[0], sc_ref[0]).astype(o_ref.dtype)


def _modulated(x2, norm_g, gi, mods, k0, tm=512):
    rows, d = x2.shape
    tm = mods.tile(tm, rows)
    return pl.pallas_call(
        _mod_kernel,
        grid=(rows // tm,),
        in_specs=[
            pl.BlockSpec((tm, d), lambda i: (i, 0)),
            pl.BlockSpec((1, 1, d), lambda i: (gi, 0, 0)),
            mods.spec(k0, tm, d), mods.spec(k0 + 1, tm, d),
        ],
        out_specs=pl.BlockSpec((tm, d), lambda i: (i, 0)),
        out_shape=jax.ShapeDtypeStruct((rows, d), BF16),
        compiler_params=_params("parallel"),
        name="modulate",
    )(x2, norm_g, mods.table, mods.table)


def _dft_tables(n):
    def unit(rows, cols):
        phase = (rows[:, None] * cols[None, :]) % n
        ang = phase.astype(F32) * (2.0 * math.pi / n)
        return jnp.cos(ang), jnp.sin(ang)

    idx = jnp.arange(n, dtype=jnp.int32)
    s = 1.0 / math.sqrt(n)
    m = DFT_TABLE_SPLIT
    if n <= m or n % m:
        c, sn = unit(idx, idx)
    else:
        ca, sa = unit(idx, jnp.arange(n // m, dtype=jnp.int32) * m)
        cb, sb = unit(idx, jnp.arange(m, dtype=jnp.int32))
        c = (ca[:, :, None] * cb[:, None, :] - sa[:, :, None] * sb[:, None, :]).reshape(n, n)
        sn = (sa[:, :, None] * cb[:, None, :] + ca[:, :, None] * sb[:, None, :]).reshape(n, n)
    return (c * s).astype(BF16), (sn * s).astype(BF16)


def _pos_dft_kernel(c_ref, s_ref, h_ref, zr_ref, zi_ref):
    h = h_ref[0]
    zr_ref[0] = jnp.dot(c_ref[...], h, preferred_element_type=F32).astype(BF16)
    zi_ref[0] = jnp.dot(s_ref[...], h, preferred_element_type=F32).astype(BF16)


def _pos_dft(h3, cos_n, sin_n, tk=256):
    b, n, d = h3.shape
    tk = min(tk, n)
    out = jax.ShapeDtypeStruct((b, n, d), BF16)
    return pl.pallas_call(
        _pos_dft_kernel,
        grid=(b, n // tk),
        in_specs=[
            pl.BlockSpec((tk, n), lambda bi, i: (i, 0)),
            pl.BlockSpec((tk, n), lambda bi, i: (i, 0)),
            pl.BlockSpec((1, n, d), lambda bi, i: (bi, 0, 0), pipeline_mode=pl.Buffered(1)),
        ],
        out_specs=[pl.BlockSpec((1, tk, d), lambda bi, i: (bi, i, 0))] * 2,
        out_shape=[out, out],
        compiler_params=_params("parallel", "parallel"),
        name="pos_dft",
    )(cos_n, sin_n, h3)


def _fft_tables(n):
    s1, r = FFT_SLABS, FFT_CHUNK
    s2 = n // s1
    assert s1 * s2 == n and s1 * r == s2 and s2 % r == 0
    ar = lambda m: jnp.arange(m, dtype=jnp.int32)
    j, k1, rr, n1 = ar(s2 // r)[:, None, None, None], ar(s1)[None, :, None, None], \
        ar(r)[None, None, :, None], ar(s1)[None, None, None, :]
    ang = ((k1 * (s2 * n1 + r * j + rr)) % n).astype(F32) * (2.0 * math.pi / n)
    eye = jnp.eye(r, dtype=F32)
    spread = lambda t: (t[..., None] * eye[None, None, :, None, :]).reshape(s2 // r, s1 * r, s1 * r)
    t1 = jnp.concatenate([spread(jnp.cos(ang)), spread(-jnp.sin(ang))], axis=1).astype(BF16)
    phi = ((ar(s2)[:, None] * ar(s2)[None, :]) % s2).astype(F32) * (2.0 * math.pi / s2)
    c, s = jnp.cos(phi) / math.sqrt(n), jnp.sin(phi) / math.sqrt(n)
    t2 = jnp.concatenate([jnp.concatenate([c, s], axis=1), jnp.concatenate([s, -c], axis=1)],
                         axis=0).astype(BF16)
    src = (ar(s1)[None, :] * r + ar(r)[:, None]).reshape(-1)
    perm = (src[:, None] == ar(s1 * r)[None, :]).astype(BF16)
    return t1, t2, perm


def _fft_kernel(t1_ref, t2_ref, p_ref, h_ref, zr_ref, zi_ref, b_ref, z_ref):
    s1, r = FFT_SLABS, FFT_CHUNK
    s2 = h_ref.shape[1] // s1
    for j in range(s2 // r):
        g = jnp.concatenate([h_ref[0, pl.ds(s2 * n1 + r * j, r), :] for n1 in range(s1)], axis=0)
        res = jnp.dot(t1_ref[j], g, preferred_element_type=F32).astype(BF16)
        for part in range(2):
            for k1 in range(s1):
                row = part * s2 + k1 * r
                b_ref[pl.ds(k1 * 2 * s2 + part * s2 + r * j, r), :] = res[row:row + r]
    for k1 in range(s1):
        rows = pl.ds(k1 * 2 * s2, 2 * s2)
        z_ref[rows, :] = jnp.dot(t2_ref[...], b_ref[rows, :], preferred_element_type=F32).astype(BF16)
    for part, out_ref in ((0, zr_ref), (1, zi_ref)):
        for jp in range(s2 // r):
            g = jnp.concatenate([z_ref[pl.ds(k1 * 2 * s2 + part * s2 + r * jp, r), :]
                                 for k1 in range(s1)], axis=0)
            out_ref[0, pl.ds(s1 * r * jp, s1 * r), :] = jnp.dot(
                p_ref[...], g, preferred_element_type=F32).astype(BF16)


def _pos_dft_two_stage(h3, tc=256):
    b, n, d = h3.shape
    tc = min(tc, d)
    t1, t2, perm = _fft_tables(n)
    const = lambda a: pl.BlockSpec(a.shape, lambda bi, ci: (0,) * a.ndim)
    io = pl.BlockSpec((1, n, tc), lambda bi, ci: (bi, 0, ci))
    out = jax.ShapeDtypeStruct((b, n, d), BF16)
    return pl.pallas_call(
        _fft_kernel,
        grid=(b, d // tc),
        in_specs=[const(t1), const(t2), const(perm), io],
        out_specs=[io, io],
        out_shape=[out, out],
        scratch_shapes=[pltpu.VMEM((2 * n, tc), BF16), pltpu.VMEM((2 * n, tc), BF16)],
        compiler_params=_params("parallel", "parallel"),
        name="pos_fft",
    )(t1, t2, perm, h3)


def _chan_kernel(x_ref, zr_ref, zi_ref, tab_ref, w_ref, gt_ref, o_ref, f_ref):
    gd = tab_ref.shape[1]
    tab = tab_ref[...]
    for grp in range(FOURIER_GROUPS):
        cols = slice(grp * gd, (grp + 1) * gd)
        z = jnp.concatenate([zr_ref[:, cols], zi_ref[:, cols]], axis=-1)
        f_ref[:, cols] = jnp.dot(z, tab, preferred_element_type=F32).astype(BF16)
    y = jnp.dot(f_ref[...], w_ref[...], preferred_element_type=F32)
    o_ref[...] = x_ref[...] + gt_ref[0] * y


def _chan_mix(x2, zr2, zi2, tab, w_out, mods, k_gate, tm=512):
    rows, d = x2.shape
    tm = mods.tile(tm, rows)
    return pl.pallas_call(
        _chan_kernel,
        grid=(rows // tm,),
        in_specs=[
            pl.BlockSpec((tm, d), lambda i: (i, 0)),
            pl.BlockSpec((tm, d), lambda i: (i, 0)),
            pl.BlockSpec((tm, d), lambda i: (i, 0)),
            pl.BlockSpec(tab.shape, lambda i: (0, 0)),
            pl.BlockSpec((d, d), lambda i: (0, 0)),
            mods.spec(k_gate, tm, d),
        ],
        out_specs=pl.BlockSpec((tm, d), lambda i: (i, 0)),
        out_shape=jax.ShapeDtypeStruct((rows, d), F32),
        scratch_shapes=[pltpu.VMEM((tm, d), BF16)],
        compiler_params=_params("parallel"),
        name="chan_mix",
    )(x2, zr2, zi2, tab, w_out, mods.table)


def _fourier_layer(x3, norm_g, gi, mods, w_out, chan_tab):
    b, n, d = x3.shape
    x2 = x3.reshape(b * n, d)
    h = _modulated(x2, norm_g, gi, mods, 3).reshape(b, n, d)
    if n == FFT_SLABS * FFT_SLABS * FFT_CHUNK:
        zr, zi = _pos_dft_two_stage(h)
    else:
        zr, zi = _pos_dft(h, *_dft_tables(n))
    out = _chan_mix(x2, zr.reshape(b * n, d), zi.reshape(b * n, d), chan_tab, w_out, mods, 5)
    return out.reshape(b, n, d)


def _rope_tables(n):
    pos = jnp.arange(n, dtype=jnp.int32)
    row = (pos // GRID_W).astype(F32)
    col = (pos % GRID_W).astype(F32)
    inv_freq = ROPE_THETA ** (-jnp.arange(ROPE_FREQS, dtype=F32) / ROPE_FREQS)
    a_r = row[:, None] * inv_freq
    a_c = col[:, None] * inv_freq
    ang = jnp.concatenate([a_r, a_r, a_c, a_c], axis=-1)
    cos, sin = jnp.cos(ang), jnp.sin(ang)
    first_half = (jnp.arange(HEAD_DIM) % (2 * ROPE_FREQS)) < ROPE_FREQS
    sin_from_upper = jnp.where(first_half[None, :], -sin, 0.0)
    sin_from_lower = jnp.where(first_half[None, :], 0.0, sin)
    return cos, sin_from_upper, sin_from_lower


def _head_norm(t, g):
    ms = jnp.mean(t * t, axis=-1, keepdims=True)
    return t * lax.rsqrt(ms + EPS) * g


def _rope(t, cos, s_up, s_lo):
    up = pltpu.roll(t, HEAD_DIM - ROPE_FREQS, 1)
    lo = pltpu.roll(t, ROPE_FREQS, 1)
    return t * cos + up * s_up + lo * s_lo


def _qkv_kernel(x_ref, g_ref, sh_ref, sc_ref, w_ref, qg_ref, kg_ref, *refs,
                n_q_heads, use_rope, q_scale):
    if use_rope:
        cos, s_up, s_lo = (r[...] for r in refs[:3])
        refs = refs[3:]
    q_ref = refs[0] if n_q_heads else None
    k_ref, v_ref = refs[-2:]
    h = _modulate(x_ref[...], g_ref[0], sh_ref[0], sc_ref[0]).astype(BF16)
    gw = N_KV_HEADS * HEAD_DIM
    n_q_groups = n_q_heads // N_KV_HEADS
    for grp in range(n_q_groups + 2):
        y = jnp.dot(h, w_ref[:, grp * gw:(grp + 1) * gw], preferred_element_type=F32)
        if grp == n_q_groups + 1:
            v_ref[...] = y.astype(BF16)
            continue
        is_q = grp < n_q_groups
        gain = (qg_ref[...] * q_scale) if is_q else kg_ref[...]
        for hd in range(N_KV_HEADS):
            t = _head_norm(y[:, hd * HEAD_DIM:(hd + 1) * HEAD_DIM], gain)
            if use_rope:
                t = _rope(t, cos, s_up, s_lo)
            if is_q:
                c0 = grp * gw + hd * HEAD_DIM
                q_ref[:, c0:c0 + HEAD_DIM] = t.astype(BF16)
            else:
                k_ref[:, hd * HEAD_DIM:(hd + 1) * HEAD_DIM] = t.astype(BF16)


def _qkv_proj(x2, norm_g, gi, mods, k0, w, q_g, k_g, rope, n_q_heads, seq, tm=512):
    rows, d = x2.shape
    tm = mods.tile(tm, rows)
    kv_dim = N_KV_HEADS * HEAD_DIM
    q_dim = n_q_heads * HEAD_DIM
    width = q_dim + 2 * kv_dim
    col_block, rem = divmod(w.shape[1] - width, width)
    assert rem == 0
    use_rope = rope is not None
    rope = rope if use_rope else ()
    tiles_per_seq = seq // tm
    kern = functools.partial(_qkv_kernel, n_q_heads=n_q_heads, use_rope=use_rope,
                             q_scale=math.log2(math.e) / math.sqrt(HEAD_DIM))
    head_vec = pl.BlockSpec((1, HEAD_DIM), lambda i: (0, 0))
    rope_spec = pl.BlockSpec((tm, HEAD_DIM), lambda i: (i % tiles_per_seq, 0))
    out_widths = ([q_dim] if n_q_heads else []) + [kv_dim, kv_dim]
    return pl.pallas_call(
        kern,
        grid=(rows // tm,),
        in_specs=[
            pl.BlockSpec((tm, d), lambda i: (i, 0)),
            pl.BlockSpec((1, 1, d), lambda i: (gi, 0, 0)),
            mods.spec(k0, tm, d), mods.spec(k0 + 1, tm, d),
            pl.BlockSpec((d, width), lambda i: (0, col_block)),
            head_vec, head_vec,
        ] + [rope_spec] * len(rope),
        out_specs=[pl.BlockSpec((tm, wd), lambda i: (i, 0)) for wd in out_widths],
        out_shape=[jax.ShapeDtypeStruct((rows, wd), BF16) for wd in out_widths],
        compiler_params=_params("parallel"),
        name="qkv_proj",
    )(x2, norm_g, mods.table, mods.table, w,
      q_g.reshape(1, HEAD_DIM), k_g.reshape(1, HEAD_DIM), *rope)


def _attn_kernel(q_ref, kc_ref, vc_ref, kl_ref, vl_ref, o_ref, k_ref, v_ref, *, group):
    l = kc_ref.shape[1]

    @pl.when(pl.program_id(2) == 0)
    def _():
        k_ref[:l] = kc_ref[0]
        k_ref[l:] = kl_ref[0]
        v_ref[:l, :HEAD_DIM] = vc_ref[0]
        v_ref[l:, :HEAD_DIM] = vl_ref[0]
        v_ref[:, HEAD_DIM:] = jnp.ones((v_ref.shape[0], HEAD_DIM), BF16)

    nt = (((1,), (1,)), ((), ()))
    k, v = k_ref[...], v_ref[...]
    tq = q_ref.shape[1]
    sub = min(tq, ATTN_SUB_ROWS)
    for r0 in range(0, tq, sub):
        rows = slice(r0, r0 + sub)
        for gi in range(group):
            cols = slice(gi * HEAD_DIM, (gi + 1) * HEAD_DIM)
            s = lax.dot_general(q_ref[0, rows, cols], k, nt, preferred_element_type=F32)
            p = jnp.exp2(s - jnp.max(s, axis=-1, keepdims=True)).astype(BF16)
            oe = jnp.dot(p, v, preferred_element_type=F32)
            o_ref[0, rows, cols] = (oe[:, :HEAD_DIM] / oe[:, HEAD_DIM:HEAD_DIM + 1]).astype(BF16)


def _attention(q3, kc3, vc3, kl3, vl3, tq=1024):
    b, s, qd = q3.shape
    tq = min(tq, s)
    l = kc3.shape[1]
    group = qd // (N_KV_HEADS * HEAD_DIM)
    gw = group * HEAD_DIM
    kv_spec = lambda n: pl.BlockSpec((1, n, HEAD_DIM), lambda bi, kh, i: (bi, 0, kh))
    return pl.pallas_call(
        functools.partial(_attn_kernel, group=group),
        grid=(b, N_KV_HEADS, s // tq),
        in_specs=[
            pl.BlockSpec((1, tq, gw), lambda bi, kh, i: (bi, i, kh)),
            kv_spec(l), kv_spec(l), kv_spec(s), kv_spec(s),
        ],
        out_specs=pl.BlockSpec((1, tq, gw), lambda bi, kh, i: (bi, i, kh)),
        out_shape=jax.ShapeDtypeStruct((b, s, qd), BF16),
        scratch_shapes=[pltpu.VMEM((l + s, HEAD_DIM), BF16), pltpu.VMEM((l + s, 2 * HEAD_DIM), BF16)],
        compiler_params=_params("parallel", "parallel", "arbitrary"),
        name="attention",
    )(q3, kc3, vc3, kl3, vl3)


def _oproj_kernel(x_ref, a_ref, w_ref, gt_ref, o_ref):
    y = jnp.dot(a_ref[...], w_ref[...], preferred_element_type=F32)
    o_ref[...] = x_ref[...] + gt_ref[0] * y


def _out_proj(x2, a2, w, mods, k_gate, tm=512):
    rows, d = x2.shape
    tm = mods.tile(tm, rows)
    return pl.pallas_call(
        _oproj_kernel,
        grid=(rows // tm,),
        in_specs=[
            pl.BlockSpec((tm, d), lambda i: (i, 0)),
            pl.BlockSpec((tm, a2.shape[1]), lambda i: (i, 0)),
            pl.BlockSpec(w.shape, lambda i: (0, 0)),
            mods.spec(k_gate, tm, d),
        ],
        out_specs=pl.BlockSpec((tm, d), lambda i: (i, 0)),
        out_shape=jax.ShapeDtypeStruct((rows, d), F32),
        compiler_params=_params("parallel"),
        name="out_proj",
    )(x2, a2, w, mods.table)


def kernel(x, c, ctx, c_ctx, w_ada, b_ada, norm_g, w_ffn_in, w_ffn_out, w_fourier_out, w_qkv,
           q_norm_g, k_norm_g, w_attn_out):
    b, n, d = x.shape
    l = ctx.shape[1]
    depth = w_ada.shape[0]
    assert depth == 2 and b <= 8, "layer 0 = Fourier mixer, layer 1 = attention mixer"
    q_dim = w_attn_out.shape[1]
    n_heads = q_dim // HEAD_DIM

    s16 = jnp.zeros((MOD_ROWS, d), F32).at[:b].set(jax.nn.silu(c)).at[8].set(jax.nn.silu(c_ctx))
    table = _ada(s16.astype(BF16), w_ada, b_ada).reshape(depth * MOD_ROWS * N_MOD, 1, d)

    f = w_ffn_out.shape[2]
    w_in_all = w_ffn_in.reshape(depth * 2, d, 2 * f)
    w_out_all = w_ffn_out.reshape(depth * 2, f, d)
    w_in, w_out = w_in_all[0].astype(BF16), w_out_all[0].astype(BF16)
    ng = norm_g.reshape(depth * 3, 1, d)
    w_four = w_fourier_out.astype(BF16)
    w_qkv_b = w_qkv.astype(BF16)
    w_o = w_attn_out.astype(BF16)

    gd = d // FOURIER_GROUPS
    cos_c, sin_c = _dft_tables(gd)
    chan_tab = jnp.concatenate([cos_c, -sin_c], axis=0)

    xl = x.reshape(b * n, d)
    xc = ctx.reshape(b * l, d)

    ml = _Mods(table, 0, n, ctx=False)
    mc = _Mods(table, 0, l, ctx=True)
    xc = _ffn(xc, ng, 0, mc, 0, w_in, w_out)
    xl, w_in, w_out = _ffn(xl, ng, 0, ml, 0, w_in, w_out, cast_next=(w_in_all, w_out_all, 1))
    xl = _fourier_layer(xl.reshape(b, n, d), ng, 1, ml, w_four[0], chan_tab).reshape(b * n, d)
    xc = _fourier_layer(xc.reshape(b, l, d), ng, 1, mc, w_four[0], chan_tab).reshape(b * l, d)
    xc = _ffn(xc, ng, 2, mc, 6, w_in, w_out)
    xl, w_in, w_out = _ffn(xl, ng, 2, ml, 6, w_in, w_out, cast_next=(w_in_all, w_out_all, 2))

    ml = _Mods(table, 1, n, ctx=False)
    mc = _Mods(table, 1, l, ctx=True)
    xc = _ffn(xc, ng, 3, mc, 0, w_in, w_out)
    xl, w_in, w_out = _ffn(xl, ng, 3, ml, 0, w_in, w_out, cast_next=(w_in_all, w_out_all, 3))
    rope = _rope_tables(n)
    q, kl, vl = _qkv_proj(xl, ng, 4, ml, 3, w_qkv_b[0], q_norm_g[0], k_norm_g[0], rope, n_heads, n)
    kc, vc = _qkv_proj(xc, ng, 4, mc, 3, w_qkv_b[0], q_norm_g[0], k_norm_g[0], None, 0, l)
    kv_dim = N_KV_HEADS * HEAD_DIM
    o = _attention(q.reshape(b, n, q_dim), kc.reshape(b, l, kv_dim), vc.reshape(b, l, kv_dim),
                   kl.reshape(b, n, kv_dim), vl.reshape(b, n, kv_dim))
    xl = _out_proj(xl, o.reshape(b * n, q_dim), w_o[0], ml, 5)
    xl = _ffn(xl, ng, 5, ml, 6, w_in, w_out)
    return xl.reshape(b, n, d)
```

```python
import functools
import math

import jax
import jax.numpy as jnp
from jax import lax
from jax.experimental import pallas as pl
from jax.experimental.pallas import tpu as pltpu

F32 = jnp.float32
BF16 = jnp.bfloat16

GRID_W = 64
FOURIER_GROUPS = 8
HEAD_DIM = 128
N_KV_HEADS = 4
ROPE_FREQS = HEAD_DIM // 4
ROPE_THETA = 10000.0
N_MOD = 9
EPS = 1e-6
MOD_ROWS = 16
DFT_TABLE_SPLIT = 64
ATTN_SUB_ROWS = 256
MOD_CHUNK_ROWS = 16
FFT_CHUNK = 16
FFT_SLABS = 16

VMEM_LIMIT_BYTES = 62 * 1024 * 1024


def _params(*sem):
    return pltpu.CompilerParams(dimension_semantics=sem, vmem_limit_bytes=VMEM_LIMIT_BYTES)


def _modulate(x, g, shift, scale):
    ms = jnp.mean(x * x, axis=-1, keepdims=True)
    return (x * lax.rsqrt(ms + EPS)) * g * (1.0 + scale) + shift


def _ada_kernel(s_ref, w_ref, b_ref, o_ref):
    w = w_ref[0].astype(BF16)
    o_ref[0] = jnp.dot(s_ref[...], w, preferred_element_type=F32) + b_ref[0]


def _ada(s16, w_ada, b_ada, tn=1024):
    depth, d, n = w_ada.shape
    return pl.pallas_call(
        _ada_kernel,
        grid=(depth, n // tn),
        in_specs=[
            pl.BlockSpec((MOD_ROWS, d), lambda l, j: (0, 0)),
            pl.BlockSpec((1, d, tn), lambda l, j: (l, 0, j)),
            pl.BlockSpec((1, 1, tn), lambda l, j: (l, 0, j)),
        ],
        out_specs=pl.BlockSpec((1, MOD_ROWS, tn), lambda l, j: (l, 0, j)),
        out_shape=jax.ShapeDtypeStruct((depth, MOD_ROWS, n), F32),
        compiler_params=_params("parallel", "parallel"),
        name="ada",
    )(s16, w_ada, b_ada.reshape(depth, 1, n))


class _Mods:
    def __init__(self, table, layer, rows_per_mod, ctx):
        self.table, self.layer, self.rows_per_mod, self.ctx = table, layer, rows_per_mod, ctx

    def tile(self, tm, rows):
        tm = min(tm, rows if self.ctx else self.rows_per_mod)
        assert rows % tm == 0 and (self.ctx or self.rows_per_mod % tm == 0)
        return tm

    def spec(self, k, tm, d, tile_of_step=lambda i: i):
        layer, rpm, ctx = self.layer, self.rows_per_mod, self.ctx

        def index(i, *_):
            row = (MOD_ROWS - 8) if ctx else (tile_of_step(i) * tm) // rpm
            return ((layer * MOD_ROWS + row) * N_MOD + k, 0, 0)

        return pl.BlockSpec((1, 1, d), index)


def _ffn_kernel(x_ref, g_ref, sh_ref, sc_ref, gt_ref, wg_ref, wu_ref, wo_ref, *refs):
    if len(refs) == 2:
        o_ref, h_ref = refs
    else:
        next_in_ref, next_out_ref, o_ref, next_in_bf_ref, next_out_bf_ref, h_ref = refs
        next_in_bf_ref[...] = next_in_ref[...].astype(BF16)
        next_out_bf_ref[...] = next_out_ref[...].astype(BF16)

    @pl.when(pl.program_id(1) == 0)
    def _():
        gain = g_ref[0] * (1.0 + sc_ref[0])
        shift = sh_ref[0]

        def rows(c, carry):
            r = pl.ds(pl.multiple_of(c * MOD_CHUNK_ROWS, MOD_CHUNK_ROWS), MOD_CHUNK_ROWS)
            x = x_ref[r, :]
            ms = jnp.mean(x * x, axis=-1, keepdims=True)
            h_ref[r, :] = ((x * lax.rsqrt(ms + EPS)) * gain + shift).astype(BF16)
            o_ref[r, :] = x
            return carry

        lax.fori_loop(0, x_ref.shape[0] // MOD_CHUNK_ROWS, rows, 0, unroll=8)

    h = h_ref[...]
    gate = jnp.dot(h, wg_ref[...], preferred_element_type=F32)
    up = jnp.dot(h, wu_ref[...], preferred_element_type=F32)
    act = (jax.nn.silu(gate) * up).astype(BF16)
    o_ref[...] += (0.5 * gt_ref[0]) * jnp.dot(act, wo_ref[...], preferred_element_type=F32)


def _ffn(x2, norm_g, gi, mods, k0, w_in, w_out, cast_next=None, tm=1024, tf=512):
    rows, d = x2.shape
    f = w_out.shape[0]
    tm, tf = mods.tile(tm, rows), min(tf, f)
    ni, nk = rows // tm, f // tf
    in_specs = [
        pl.BlockSpec((tm, d), lambda i, k: (i, 0)),
        pl.BlockSpec((1, 1, d), lambda i, k: (gi, 0, 0)),
        mods.spec(k0, tm, d), mods.spec(k0 + 1, tm, d), mods.spec(k0 + 2, tm, d),
        pl.BlockSpec((d, tf), lambda i, k: (0, k)),
        pl.BlockSpec((d, tf), lambda i, k: (0, nk + k)),
        pl.BlockSpec((tf, d), lambda i, k: (k, 0)),
    ]
    out_specs = [pl.BlockSpec((tm, d), lambda i, k: (i, 0))]
    out_shape = [jax.ShapeDtypeStruct((rows, d), F32)]
    operands = [x2, norm_g, mods.table, mods.table, mods.table, w_in, w_in, w_out]
    if cast_next is not None:
        w_in_all, w_out_all, slab = cast_next
        ri, ci, ro = d // ni, 2 * f // nk, f // (ni * nk)
        assert ri * ni == d and ci * nk == 2 * f and ro * ni * nk == f
        in_specs += [pl.BlockSpec((None, ri, ci), lambda i, k: (slab, i, k)),
                     pl.BlockSpec((None, ro, d), lambda i, k: (slab, i * nk + k, 0))]
        out_specs += [pl.BlockSpec((ri, ci), lambda i, k: (i, k)),
                      pl.BlockSpec((ro, d), lambda i, k: (i * nk + k, 0))]
        out_shape += [jax.ShapeDtypeStruct((d, 2 * f), BF16), jax.ShapeDtypeStruct((f, d), BF16)]
        operands += [w_in_all, w_out_all]
    res = pl.pallas_call(
        _ffn_kernel,
        grid=(ni, nk),
        in_specs=in_specs,
        out_specs=out_specs,
        out_shape=out_shape,
        scratch_shapes=[pltpu.VMEM((tm, d), BF16)],
        compiler_params=_params("parallel", "arbitrary"),
        name="ffn",
    )(*operands)
    return res[0] if cast_next is None else res


def _mod_kernel(x_ref, g_ref, sh_ref, sc_ref, o_ref):
    o_ref[...] = _modulate(x_ref[...], g_ref[0], sh_ref[0], sc_ref[0]).astype(o_ref.dtype)


def _modulated(x2, norm_g, gi, mods, k0, tm=512):
    rows, d = x2.shape
    tm = mods.tile(tm, rows)
    return pl.pallas_call(
        _mod_kernel,
        grid=(rows // tm,),
        in_specs=[
            pl.BlockSpec((tm, d), lambda i: (i, 0)),
            pl.BlockSpec((1, 1, d), lambda i: (gi, 0, 0)),
            mods.spec(k0, tm, d), mods.spec(k0 + 1, tm, d),
        ],
        out_specs=pl.BlockSpec((tm, d), lambda i: (i, 0)),
        out_shape=jax.ShapeDtypeStruct((rows, d), BF16),
        compiler_params=_params("parallel"),
        name="modulate",
    )(x2, norm_g, mods.table, mods.table)


def _dft_tables(n):
    def unit(rows, cols):
        phase = (rows[:, None] * cols[None, :]) % n
        ang = phase.astype(F32) * (2.0 * math.pi / n)
        return jnp.cos(ang), jnp.sin(ang)

    idx = jnp.arange(n, dtype=jnp.int32)
    s = 1.0 / math.sqrt(n)
    m = DFT_TABLE_SPLIT
    if n <= m or n % m:
        c, sn = unit(idx, idx)
    else:
        ca, sa = unit(idx, jnp.arange(n // m, dtype=jnp.int32) * m)
        cb, sb = unit(idx, jnp.arange(m, dtype=jnp.int32))
        c = (ca[:, :, None] * cb[:, None, :] - sa[:, :, None] * sb[:, None, :]).reshape(n, n)
        sn = (sa[:, :, None] * cb[:, None, :] + ca[:, :, None] * sb[:, None, :]).reshape(n, n)
    return (c * s).astype(BF16), (sn * s).astype(BF16)


def _pos_dft_kernel(c_ref, s_ref, h_ref, zr_ref, zi_ref):
    h = h_ref[0]
    zr_ref[0] = jnp.dot(c_ref[...], h, preferred_element_type=F32).astype(BF16)
    zi_ref[0] = jnp.dot(s_ref[...], h, preferred_element_type=F32).astype(BF16)


def _pos_dft(h3, cos_n, sin_n, tk=256):
    b, n, d = h3.shape
    tk = min(tk, n)
    out = jax.ShapeDtypeStruct((b, n, d), BF16)
    return pl.pallas_call(
        _pos_dft_kernel,
        grid=(b, n // tk),
        in_specs=[
            pl.BlockSpec((tk, n), lambda bi, i: (i, 0)),
            pl.BlockSpec((tk, n), lambda bi, i: (i, 0)),
            pl.BlockSpec((1, n, d), lambda bi, i: (bi, 0, 0), pipeline_mode=pl.Buffered(1)),
        ],
        out_specs=[pl.BlockSpec((1, tk, d), lambda bi, i: (bi, i, 0))] * 2,
        out_shape=[out, out],
        compiler_params=_params("parallel", "parallel"),
        name="pos_dft",
    )(cos_n, sin_n, h3)


def _fft_tables(n):
    s1, r = FFT_SLABS, FFT_CHUNK
    s2 = n // s1
    assert s1 * s2 == n and s1 * r == s2 and s2 % r == 0
    ar = lambda m: jnp.arange(m, dtype=jnp.int32)
    j, k1, rr, n1 = ar(s2 // r)[:, None, None, None], ar(s1)[None, :, None, None], \
        ar(r)[None, None, :, None], ar(s1)[None, None, None, :]
    ang = ((k1 * (s2 * n1 + r * j + rr)) % n).astype(F32) * (2.0 * math.pi / n)
    eye = jnp.eye(r, dtype=F32)
    spread = lambda t: (t[..., None] * eye[None, None, :, None, :]).reshape(s2 // r, s1 * r, s1 * r)
    t1 = jnp.concatenate([spread(jnp.cos(ang)), spread(-jnp.sin(ang))], axis=1).astype(BF16)
    phi = ((ar(s2)[:, None] * ar(s2)[None, :]) % s2).astype(F32) * (2.0 * math.pi / s2)
    c, s = jnp.cos(phi) / math.sqrt(n), jnp.sin(phi) / math.sqrt(n)
    t2 = jnp.concatenate([jnp.concatenate([c, s], axis=1), jnp.concatenate([s, -c], axis=1)],
                         axis=0).astype(BF16)
    src = (ar(s1)[None, :] * r + ar(r)[:, None]).reshape(-1)
    perm = (src[:, None] == ar(s1 * r)[None, :]).astype(BF16)
    return t1, t2, perm


def _fft_kernel(t1_ref, t2_ref, p_ref, h_ref, zr_ref, zi_ref, b_ref, z_ref):
    s1, r = FFT_SLABS, FFT_CHUNK
    s2 = h_ref.shape[1] // s1
    for j in range(s2 // r):
        g = jnp.concatenate([h_ref[0, pl.ds(s2 * n1 + r * j, r), :] for n1 in range(s1)], axis=0)
        res = jnp.dot(t1_ref[j], g, preferred_element_type=F32).astype(BF16)
        for part in range(2):
            for k1 in range(s1):
                row = part * s2 + k1 * r
                b_ref[pl.ds(k1 * 2 * s2 + part * s2 + r * j, r), :] = res[row:row + r]
    for k1 in range(s1):
        rows = pl.ds(k1 * 2 * s2, 2 * s2)
        z_ref[rows, :] = jnp.dot(t2_ref[...], b_ref[rows, :], preferred_element_type=F32).astype(BF16)
    for part, out_ref in ((0, zr_ref), (1, zi_ref)):
        for jp in range(s2 // r):
            g = jnp.concatenate([z_ref[pl.ds(k1 * 2 * s2 + part * s2 + r * jp, r), :]
                                 for k1 in range(s1)], axis=0)
            out_ref[0, pl.ds(s1 * r * jp, s1 * r), :] = jnp.dot(
                p_ref[...], g, preferred_element_type=F32).astype(BF16)


def _pos_dft_two_stage(h3, tc=256):
    b, n, d = h3.shape
    tc = min(tc, d)
    t1, t2, perm = _fft_tables(n)
    const = lambda a: pl.BlockSpec(a.shape, lambda bi, ci: (0,) * a.ndim)
    io = pl.BlockSpec((1, n, tc), lambda bi, ci: (bi, 0, ci))
    out = jax.ShapeDtypeStruct((b, n, d), BF16)
    return pl.pallas_call(
        _fft_kernel,
        grid=(b, d // tc),
        in_specs=[const(t1), const(t2), const(perm), io],
        out_specs=[io, io],
        out_shape=[out, out],
        scratch_shapes=[pltpu.VMEM((2 * n, tc), BF16), pltpu.VMEM((2 * n, tc), BF16)],
        compiler_params=_params("parallel", "parallel"),
        name="pos_fft",
    )(t1, t2, perm, h3)


def _chan_kernel(x_ref, zr_ref, zi_ref, tab_ref, w_ref, gt_ref, o_ref, f_ref):
    gd = tab_ref.shape[1]
    tab = tab_ref[...]
    for grp in range(FOURIER_GROUPS):
        cols = slice(grp * gd, (grp + 1) * gd)
        z = jnp.concatenate([zr_ref[:, cols], zi_ref[:, cols]], axis=-1)
        f_ref[:, cols] = jnp.dot(z, tab, preferred_element_type=F32).astype(BF16)
    y = jnp.dot(f_ref[...], w_ref[...], preferred_element_type=F32)
    o_ref[...] = x_ref[...] + gt_ref[0] * y


def _chan_mix(x2, zr2, zi2, tab, w_out, mods, k_gate, tm=512):
    rows, d = x2.shape
    tm = mods.tile(tm, rows)
    return pl.pallas_call(
        _chan_kernel,
        grid=(rows // tm,),
        in_specs=[
            pl.BlockSpec((tm, d), lambda i: (i, 0)),
            pl.BlockSpec((tm, d), lambda i: (i, 0)),
            pl.BlockSpec((tm, d), lambda i: (i, 0)),
            pl.BlockSpec(tab.shape, lambda i: (0, 0)),
            pl.BlockSpec((d, d), lambda i: (0, 0)),
            mods.spec(k_gate, tm, d),
        ],
        out_specs=pl.BlockSpec((tm, d), lambda i: (i, 0)),
        out_shape=jax.ShapeDtypeStruct((rows, d), F32),
        scratch_shapes=[pltpu.VMEM((tm, d), BF16)],
        compiler_params=_params("parallel"),
        name="chan_mix",
    )(x2, zr2, zi2, tab, w_out, mods.table)


def _fourier_layer(x3, norm_g, gi, mods, w_out, chan_tab):
    b, n, d = x3.shape
    x2 = x3.reshape(b * n, d)
    h = _modulated(x2, norm_g, gi, mods, 3).reshape(b, n, d)
    if n == FFT_SLABS * FFT_SLABS * FFT_CHUNK:
        zr, zi = _pos_dft_two_stage(h)
    else:
        zr, zi = _pos_dft(h, *_dft_tables(n))
    out = _chan_mix(x2, zr.reshape(b * n, d), zi.reshape(b * n, d), chan_tab, w_out, mods, 5)
    return out.reshape(b, n, d)


def _rope_tables(n):
    pos = jnp.arange(n, dtype=jnp.int32)
    row = (pos // GRID_W).astype(F32)
    col = (pos % GRID_W).astype(F32)
    inv_freq = ROPE_THETA ** (-jnp.arange(ROPE_FREQS, dtype=F32) / ROPE_FREQS)
    a_r = row[:, None] * inv_freq
    a_c = col[:, None] * inv_freq
    ang = jnp.concatenate([a_r, a_r, a_c, a_c], axis=-1)
    cos, sin = jnp.cos(ang), jnp.sin(ang)
    first_half = (jnp.arange(HEAD_DIM) % (2 * ROPE_FREQS)) < ROPE_FREQS
    sin_from_upper = jnp.where(first_half[None, :], -sin, 0.0)
    sin_from_lower = jnp.where(first_half[None, :], 0.0, sin)
    return cos, sin_from_upper, sin_from_lower


def _head_norm(t, g):
    ms = jnp.mean(t * t, axis=-1, keepdims=True)
    return t * lax.rsqrt(ms + EPS) * g


def _rope(t, cos, s_up, s_lo):
    up = pltpu.roll(t, HEAD_DIM - ROPE_FREQS, 1)
    lo = pltpu.roll(t, ROPE_FREQS, 1)
    return t * cos + up * s_up + lo * s_lo


def _qkv_kernel(x_ref, g_ref, sh_ref, sc_ref, w_ref, qg_ref, kg_ref, *refs,
                n_q_heads, use_rope, q_scale):
    if use_rope:
        cos, s_up, s_lo = (r[...] for r in refs[:3])
        refs = refs[3:]
    q_ref = refs[0] if n_q_heads else None
    k_ref, v_ref, raw_even_ref, raw_odd_ref = refs[-4:]
    step = pl.program_id(0)
    qk_cols = raw_even_ref.shape[1]

    @pl.when(step == 0)
    def _():
        raw_odd_ref[...] = jnp.zeros_like(raw_odd_ref)

    def run(raw_new_ref, raw_old_ref):
        h = _modulate(x_ref[...], g_ref[0], sh_ref[0], sc_ref[0]).astype(BF16)
        y = jnp.dot(h, w_ref[...], preferred_element_type=F32)
        raw_new_ref[...] = y[:, :qk_cols]
        v_ref[...] = y[:, qk_cols:].astype(BF16)
        for hd in range(n_q_heads + N_KV_HEADS):
            is_q = hd < n_q_heads
            gain = (qg_ref[...] * q_scale) if is_q else kg_ref[...]
            t = _head_norm(raw_old_ref[:, hd * HEAD_DIM:(hd + 1) * HEAD_DIM], gain)
            if use_rope:
                t = _rope(t, cos, s_up, s_lo)
            if is_q:
                q_ref[:, hd * HEAD_DIM:(hd + 1) * HEAD_DIM] = t.astype(BF16)
            else:
                kh = hd - n_q_heads
                k_ref[:, kh * HEAD_DIM:(kh + 1) * HEAD_DIM] = t.astype(BF16)

    @pl.when(step % 2 == 0)
    def _():
        run(raw_even_ref, raw_odd_ref)

    @pl.when(step % 2 == 1)
    def _():
        run(raw_odd_ref, raw_even_ref)


def _qkv_proj(x2, norm_g, gi, mods, k0, w, q_g, k_g, rope, n_q_heads, seq, tm=512):
    rows, d = x2.shape
    tm = mods.tile(tm, rows)
    n_tiles = rows // tm
    kv_dim = N_KV_HEADS * HEAD_DIM
    q_dim = n_q_heads * HEAD_DIM
    width = q_dim + 2 * kv_dim
    col_block, rem = divmod(w.shape[1] - width, width)
    assert rem == 0
    use_rope = rope is not None
    rope = rope if use_rope else ()
    tiles_per_seq = seq // tm
    kern = functools.partial(_qkv_kernel, n_q_heads=n_q_heads, use_rope=use_rope,
                             q_scale=math.log2(math.e) / math.sqrt(HEAD_DIM))
    cur = lambda i: jnp.minimum(i, n_tiles - 1)
    prev = lambda i: jnp.maximum(i - 1, 0)
    head_vec = pl.BlockSpec((1, HEAD_DIM), lambda i: (0, 0))
    rope_spec = pl.BlockSpec((tm, HEAD_DIM), lambda i: (prev(i) % tiles_per_seq, 0))
    out_widths = ([q_dim] if n_q_heads else []) + [kv_dim, kv_dim]
    qk_specs = [pl.BlockSpec((tm, wd), lambda i: (prev(i), 0)) for wd in out_widths[:-1]]
    return pl.pallas_call(
        kern,
        grid=(n_tiles + 1,),
        in_specs=[
            pl.BlockSpec((tm, d), lambda i: (cur(i), 0)),
            pl.BlockSpec((1, 1, d), lambda i: (gi, 0, 0)),
            mods.spec(k0, tm, d, cur), mods.spec(k0 + 1, tm, d, cur),
            pl.BlockSpec((d, width), lambda i: (0, col_block)),
            head_vec, head_vec,
        ] + [rope_spec] * len(rope),
        out_specs=qk_specs + [pl.BlockSpec((tm, kv_dim), lambda i: (cur(i), 0))],
        out_shape=[jax.ShapeDtypeStruct((rows, wd), BF16) for wd in out_widths],
        scratch_shapes=[pltpu.VMEM((tm, q_dim + kv_dim), F32)] * 2,
        compiler_params=_params("arbitrary"),
        name="qkv_proj",
    )(x2, norm_g, mods.table, mods.table, w,
      q_g.reshape(1, HEAD_DIM), k_g.reshape(1, HEAD_DIM), *rope)


def _attn_kernel(q_ref, kc_ref, vc_ref, kl_ref, vl_ref, o_ref, k_ref, v_ref, *, group):
    l = kc_ref.shape[1]

    @pl.when(pl.program_id(2) == 0)
    def _():
        k_ref[:l] = kc_ref[0]
        k_ref[l:] = kl_ref[0]
        v_ref[:l, :HEAD_DIM] = vc_ref[0]
        v_ref[l:, :HEAD_DIM] = vl_ref[0]
        v_ref[:, HEAD_DIM:] = jnp.ones((v_ref.shape[0], HEAD_DIM), BF16)

    nt = (((1,), (1,)), ((), ()))
    k, v = k_ref[...], v_ref[...]
    tq = q_ref.shape[1]
    sub = min(tq, ATTN_SUB_ROWS)
    for r0 in range(0, tq, sub):
        rows = slice(r0, r0 + sub)
        for gi in range(group):
            cols = slice(gi * HEAD_DIM, (gi + 1) * HEAD_DIM)
            s = lax.dot_general(q_ref[0, rows, cols], k, nt, preferred_element_type=F32)
            p = jnp.exp2(s - jnp.max(s, axis=-1, keepdims=True)).astype(BF16)
            oe = jnp.dot(p, v, preferred_element_type=F32)
            o_ref[0, rows, cols] = (oe[:, :HEAD_DIM] / oe[:, HEAD_DIM:HEAD_DIM + 1]).astype(BF16)


def _attention(q3, kc3, vc3, kl3, vl3, tq=1024):
    b, s, qd = q3.shape
    tq = min(tq, s)
    l = kc3.shape[1]
    group = qd // (N_KV_HEADS * HEAD_DIM)
    gw = group * HEAD_DIM
    kv_spec = lambda n: pl.BlockSpec((1, n, HEAD_DIM), lambda bi, kh, i: (bi, 0, kh))
    return pl.pallas_call(
        functools.partial(_attn_kernel, group=group),
        grid=(b, N_KV_HEADS, s // tq),
        in_specs=[
            pl.BlockSpec((1, tq, gw), lambda bi, kh, i: (bi, i, kh)),
            kv_spec(l), kv_spec(l), kv_spec(s), kv_spec(s),
        ],
        out_specs=pl.BlockSpec((1, tq, gw), lambda bi, kh, i: (bi, i, kh)),
        out_shape=jax.ShapeDtypeStruct((b, s, qd), BF16),
        scratch_shapes=[pltpu.VMEM((l + s, HEAD_DIM), BF16), pltpu.VMEM((l + s, 2 * HEAD_DIM), BF16)],
        compiler_params=_params("parallel", "parallel", "arbitrary"),
        name="attention",
    )(q3, kc3, vc3, kl3, vl3)


def _oproj_kernel(x_ref, a_ref, w_ref, gt_ref, o_ref):
    y = jnp.dot(a_ref[...], w_ref[...], preferred_element_type=F32)
    o_ref[...] = x_ref[...] + gt_ref[0] * y


def _out_proj(x2, a2, w, mods, k_gate, tm=512):
    rows, d = x2.shape
    tm = mods.tile(tm, rows)
    return pl.pallas_call(
        _oproj_kernel,
        grid=(rows // tm,),
        in_specs=[
            pl.BlockSpec((tm, d), lambda i: (i, 0)),
            pl.BlockSpec((tm, a2.shape[1]), lambda i: (i, 0)),
            pl.BlockSpec(w.shape, lambda i: (0, 0)),
            mods.spec(k_gate, tm, d),
        ],
        out_specs=pl.BlockSpec((tm, d), lambda i: (i, 0)),
        out_shape=jax.ShapeDtypeStruct((rows, d), F32),
        compiler_params=_params("parallel"),
        name="out_proj",
    )(x2, a2, w, mods.table)


def kernel(x, c, ctx, c_ctx, w_ada, b_ada, norm_g, w_ffn_in, w_ffn_out, w_fourier_out, w_qkv,
           q_norm_g, k_norm_g, w_attn_out):
    b, n, d = x.shape
    l = ctx.shape[1]
    depth = w_ada.shape[0]
    assert depth == 2 and b <= 8, "layer 0 = Fourier mixer, layer 1 = attention mixer"
    q_dim = w_attn_out.shape[1]
    n_heads = q_dim // HEAD_DIM

    s16 = jnp.zeros((MOD_ROWS, d), F32).at[:b].set(jax.nn.silu(c)).at[8].set(jax.nn.silu(c_ctx))
    table = _ada(s16.astype(BF16), w_ada, b_ada).reshape(depth * MOD_ROWS * N_MOD, 1, d)

    f = w_ffn_out.shape[2]
    w_in_all = w_ffn_in.reshape(depth * 2, d, 2 * f)
    w_out_all = w_ffn_out.reshape(depth * 2, f, d)
    w_in, w_out = w_in_all[0].astype(BF16), w_out_all[0].astype(BF16)
    ng = norm_g.reshape(depth * 3, 1, d)
    w_four = w_fourier_out.astype(BF16)
    w_qkv_b = w_qkv.astype(BF16)
    w_o = w_attn_out.astype(BF16)

    gd = d // FOURIER_GROUPS
    cos_c, sin_c = _dft_tables(gd)
    chan_tab = jnp.concatenate([cos_c, -sin_c], axis=0)

    xl = x.reshape(b * n, d)
    xc = ctx.reshape(b * l, d)

    ml = _Mods(table, 0, n, ctx=False)
    mc = _Mods(table, 0, l, ctx=True)
    xc = _ffn(xc, ng, 0, mc, 0, w_in, w_out)
    xl, w_in, w_out = _ffn(xl, ng, 0, ml, 0, w_in, w_out, cast_next=(w_in_all, w_out_all, 1))
    xl = _fourier_layer(xl.reshape(b, n, d), ng, 1, ml, w_four[0], chan_tab).reshape(b * n, d)
    xc = _fourier_layer(xc.reshape(b, l, d), ng, 1, mc, w_four[0], chan_tab).reshape(b * l, d)
    xc = _ffn(xc, ng, 2, mc, 6, w_in, w_out)
    xl, w_in, w_out = _ffn(xl, ng, 2, ml, 6, w_in, w_out, cast_next=(w_in_all, w_out_all, 2))

    ml = _Mods(table, 1, n, ctx=False)
    mc = _Mods(table, 1, l, ctx=True)
    xc = _ffn(xc, ng, 3, mc, 0, w_in, w_out)
    xl, w_in, w_out = _ffn(xl, ng, 3, ml, 0, w_in, w_out, cast_next=(w_in_all, w_out_all, 3))
    rope = _rope_tables(n)
    q, kl, vl = _qkv_proj(xl, ng, 4, ml, 3, w_qkv_b[0], q_norm_g[0], k_norm_g[0], rope, n_heads, n)
    kc, vc = _qkv_proj(xc, ng, 4, mc, 3, w_qkv_b[0], q_norm_g[0], k_norm_g[0], None, 0, l)
    kv_dim = N_KV_HEADS * HEAD_DIM
    o = _attention(q.reshape(b, n, q_dim), kc.reshape(b, l, kv_dim), vc.reshape(b, l, kv_dim),
                   kl.reshape(b, n, kv_dim), vl.reshape(b, n, kv_dim))
    xl = _out_proj(xl, o.reshape(b * n, q_dim), w_o[0], ml, 5)
    xl = _ffn(xl, ng, 5, ml, 6, w_in, w_out)
    return xl.reshape(b, n, d)
```

```python
import functools
import math

import jax
import jax.numpy as jnp
from jax import lax
from jax.experimental import pallas as pl
from jax.experimental.pallas import tpu as pltpu

F32 = jnp.float32
BF16 = jnp.bfloat16

GRID_W = 64
FOURIER_GROUPS = 8
HEAD_DIM = 128
N_KV_HEADS = 4
ROPE_FREQS = HEAD_DIM // 4
ROPE_THETA = 10000.0
N_MOD = 9
EPS = 1e-6
MOD_ROWS = 16
DFT_TABLE_SPLIT = 64
ATTN_SUB_ROWS = 256
FFN_FIRST_SUB_ROWS = 256
FFT_CHUNK = 16
FFT_SLABS = 16

VMEM_LIMIT_BYTES = 62 * 1024 * 1024


def _params(*sem):
    return pltpu.CompilerParams(dimension_semantics=sem, vmem_limit_bytes=VMEM_LIMIT_BYTES)


def _modulate(x, g, shift, scale):
    ms = jnp.mean(x * x, axis=-1, keepdims=True)
    return (x * lax.rsqrt(ms + EPS)) * g * (1.0 + scale) + shift


def _ada_kernel(s_ref, w_ref, b_ref, o_ref):
    w = w_ref[0].astype(BF16)
    o_ref[0] = jnp.dot(s_ref[...], w, preferred_element_type=F32) + b_ref[0]


def _ada(s16, w_ada, b_ada, tn=1024):
    depth, d, n = w_ada.shape
    return pl.pallas_call(
        _ada_kernel,
        grid=(depth, n // tn),
        in_specs=[
            pl.BlockSpec((MOD_ROWS, d), lambda l, j: (0, 0)),
            pl.BlockSpec((1, d, tn), lambda l, j: (l, 0, j)),
            pl.BlockSpec((1, 1, tn), lambda l, j: (l, 0, j)),
        ],
        out_specs=pl.BlockSpec((1, MOD_ROWS, tn), lambda l, j: (l, 0, j)),
        out_shape=jax.ShapeDtypeStruct((depth, MOD_ROWS, n), F32),
        compiler_params=_params("parallel", "parallel"),
        name="ada",
    )(s16, w_ada, b_ada.reshape(depth, 1, n))


class _Mods:
    def __init__(self, table, layer, rows_per_mod, ctx):
        self.table, self.layer, self.rows_per_mod, self.ctx = table, layer, rows_per_mod, ctx

    def tile(self, tm, rows):
        tm = min(tm, rows if self.ctx else self.rows_per_mod)
        assert rows % tm == 0 and (self.ctx or self.rows_per_mod % tm == 0)
        return tm

    def spec(self, k, tm, d, tile_of_step=lambda i: i):
        layer, rpm, ctx = self.layer, self.rows_per_mod, self.ctx

        def index(i, *_):
            row = (MOD_ROWS - 8) if ctx else (tile_of_step(i) * tm) // rpm
            return ((layer * MOD_ROWS + row) * N_MOD + k, 0, 0)

        return pl.BlockSpec((1, 1, d), index)


def _ffn_kernel(x_ref, g_ref, sh_ref, sc_ref, gt_ref, wg_ref, wu_ref, wo_ref, *refs):
    if len(refs) == 2:
        o_ref, h_ref = refs
    else:
        next_in_ref, next_out_ref, o_ref, next_in_bf_ref, next_out_bf_ref, h_ref = refs
        next_in_bf_ref[...] = next_in_ref[...].astype(BF16)
        next_out_bf_ref[...] = next_out_ref[...].astype(BF16)

    def update(h):
        gate = jnp.dot(h, wg_ref[...], preferred_element_type=F32)
        up = jnp.dot(h, wu_ref[...], preferred_element_type=F32)
        act = (jax.nn.silu(gate) * up).astype(BF16)
        return (0.5 * gt_ref[0]) * jnp.dot(act, wo_ref[...], preferred_element_type=F32)

    @pl.when(pl.program_id(1) == 0)
    def _():
        gain = g_ref[0] * (1.0 + sc_ref[0])
        shift = sh_ref[0]
        sub = min(FFN_FIRST_SUB_ROWS, x_ref.shape[0])
        for r0 in range(0, x_ref.shape[0], sub):
            rows = slice(r0, r0 + sub)
            x = x_ref[rows, :]
            ms = jnp.mean(x * x, axis=-1, keepdims=True)
            h = ((x * lax.rsqrt(ms + EPS)) * gain + shift).astype(BF16)
            h_ref[rows, :] = h
            o_ref[rows, :] = x + update(h)

    @pl.when(pl.program_id(1) > 0)
    def _():
        o_ref[...] += update(h_ref[...])


def _ffn(x2, norm_g, gi, mods, k0, w_in, w_out, cast_next=None, tm=1024, tf=512):
    rows, d = x2.shape
    f = w_out.shape[0]
    tm, tf = mods.tile(tm, rows), min(tf, f)
    ni, nk = rows // tm, f // tf
    in_specs = [
        pl.BlockSpec((tm, d), lambda i, k: (i, 0)),
        pl.BlockSpec((1, 1, d), lambda i, k: (gi, 0, 0)),
        mods.spec(k0, tm, d), mods.spec(k0 + 1, tm, d), mods.spec(k0 + 2, tm, d),
        pl.BlockSpec((d, tf), lambda i, k: (0, k)),
        pl.BlockSpec((d, tf), lambda i, k: (0, nk + k)),
        pl.BlockSpec((tf, d), lambda i, k: (k, 0)),
    ]
    out_specs = [pl.BlockSpec((tm, d), lambda i, k: (i, 0))]
    out_shape = [jax.ShapeDtypeStruct((rows, d), F32)]
    operands = [x2, norm_g, mods.table, mods.table, mods.table, w_in, w_in, w_out]
    if cast_next is not None:
        w_in_all, w_out_all, slab = cast_next
        ri, ci, ro = d // ni, 2 * f // nk, f // (ni * nk)
        assert ri * ni == d and ci * nk == 2 * f and ro * ni * nk == f
        in_specs += [pl.BlockSpec((None, ri, ci), lambda i, k: (slab, i, k)),
                     pl.BlockSpec((None, ro, d), lambda i, k: (slab, i * nk + k, 0))]
        out_specs += [pl.BlockSpec((ri, ci), lambda i, k: (i, k)),
                      pl.BlockSpec((ro, d), lambda i, k: (i * nk + k, 0))]
        out_shape += [jax.ShapeDtypeStruct((d, 2 * f), BF16), jax.ShapeDtypeStruct((f, d), BF16)]
        operands += [w_in_all, w_out_all]
    res = pl.pallas_call(
        _ffn_kernel,
        grid=(ni, nk),
        in_specs=in_specs,
        out_specs=out_specs,
        out_shape=out_shape,
        scratch_shapes=[pltpu.VMEM((tm, d), BF16)],
        compiler_params=_params("parallel", "arbitrary"),
        name="ffn",
    )(*operands)
    return res[0] if cast_next is None else res


def _mod_kernel(x_ref, g_ref, sh_ref, sc_ref, o_ref):
    o_ref[...] = _modulate(x_ref[...], g_ref[0], sh_ref[0], sc_ref[0]).astype(o_ref.dtype)


def _modulated(x2, norm_g, gi, mods, k0, tm=512):
    rows, d = x2.shape
    tm = mods.tile(tm, rows)
    return pl.pallas_call(
        _mod_kernel,
        grid=(rows // tm,),
        in_specs=[
            pl.BlockSpec((tm, d), lambda i: (i, 0)),
            pl.BlockSpec((1, 1, d), lambda i: (gi, 0, 0)),
            mods.spec(k0, tm, d), mods.spec(k0 + 1, tm, d),
        ],
        out_specs=pl.BlockSpec((tm, d), lambda i: (i, 0)),
        out_shape=jax.ShapeDtypeStruct((rows, d), BF16),
        compiler_params=_params("parallel"),
        name="modulate",
    )(x2, norm_g, mods.table, mods.table)


def _dft_tables(n):
    def unit(rows, cols):
        phase = (rows[:, None] * cols[None, :]) % n
        ang = phase.astype(F32) * (2.0 * math.pi / n)
        return jnp.cos(ang), jnp.sin(ang)

    idx = jnp.arange(n, dtype=jnp.int32)
    s = 1.0 / math.sqrt(n)
    m = DFT_TABLE_SPLIT
    if n <= m or n % m:
        c, sn = unit(idx, idx)
    else:
        ca, sa = unit(idx, jnp.arange(n // m, dtype=jnp.int32) * m)
        cb, sb = unit(idx, jnp.arange(m, dtype=jnp.int32))
        c = (ca[:, :, None] * cb[:, None, :] - sa[:, :, None] * sb[:, None, :]).reshape(n, n)
        sn = (sa[:, :, None] * cb[:, None, :] + ca[:, :, None] * sb[:, None, :]).reshape(n, n)
    return (c * s).astype(BF16), (sn * s).astype(BF16)


def _pos_dft_kernel(c_ref, s_ref, h_ref, zr_ref, zi_ref):
    h = h_ref[0]
    zr_ref[0] = jnp.dot(c_ref[...], h, preferred_element_type=F32).astype(BF16)
    zi_ref[0] = jnp.dot(s_ref[...], h, preferred_element_type=F32).astype(BF16)


def _pos_dft(h3, cos_n, sin_n, tk=256):
    b, n, d = h3.shape
    tk = min(tk, n)
    out = jax.ShapeDtypeStruct((b, n, d), BF16)
    return pl.pallas_call(
        _pos_dft_kernel,
        grid=(b, n // tk),
        in_specs=[
            pl.BlockSpec((tk, n), lambda bi, i: (i, 0)),
            pl.BlockSpec((tk, n), lambda bi, i: (i, 0)),
            pl.BlockSpec((1, n, d), lambda bi, i: (bi, 0, 0), pipeline_mode=pl.Buffered(1)),
        ],
        out_specs=[pl.BlockSpec((1, tk, d), lambda bi, i: (bi, i, 0))] * 2,
        out_shape=[out, out],
        compiler_params=_params("parallel", "parallel"),
        name="pos_dft",
    )(cos_n, sin_n, h3)


def _fft_tables(n):
    s1, r = FFT_SLABS, FFT_CHUNK
    s2 = n // s1
    assert s1 * s2 == n and s1 * r == s2 and s2 % r == 0
    ar = lambda m: jnp.arange(m, dtype=jnp.int32)
    j, k1, rr, n1 = ar(s2 // r)[:, None, None, None], ar(s1)[None, :, None, None], \
        ar(r)[None, None, :, None], ar(s1)[None, None, None, :]
    ang = ((k1 * (s2 * n1 + r * j + rr)) % n).astype(F32) * (2.0 * math.pi / n)
    eye = jnp.eye(r, dtype=F32)
    spread = lambda t: (t[..., None] * eye[None, None, :, None, :]).reshape(s2 // r, s1 * r, s1 * r)
    t1 = jnp.concatenate([spread(jnp.cos(ang)), spread(-jnp.sin(ang))], axis=1).astype(BF16)
    phi = ((ar(s2)[:, None] * ar(s2)[None, :]) % s2).astype(F32) * (2.0 * math.pi / s2)
    c, s = jnp.cos(phi) / math.sqrt(n), jnp.sin(phi) / math.sqrt(n)
    t2 = jnp.concatenate([jnp.concatenate([c, s], axis=1), jnp.concatenate([s, -c], axis=1)],
                         axis=0).astype(BF16)
    src = (ar(s1)[None, :] * r + ar(r)[:, None]).reshape(-1)
    perm = (src[:, None] == ar(s1 * r)[None, :]).astype(BF16)
    return t1, t2, perm


def _fft_kernel(t1_ref, t2_ref, p_ref, h_ref, zr_ref, zi_ref, b_ref, z_ref):
    s1, r = FFT_SLABS, FFT_CHUNK
    s2 = h_ref.shape[1] // s1
    for j in range(s2 // r):
        g = jnp.concatenate([h_ref[0, pl.ds(s2 * n1 + r * j, r), :] for n1 in range(s1)], axis=0)
        res = jnp.dot(t1_ref[j], g, preferred_element_type=F32).astype(BF16)
        for part in range(2):
            for k1 in range(s1):
                row = part * s2 + k1 * r
                b_ref[pl.ds(k1 * 2 * s2 + part * s2 + r * j, r), :] = res[row:row + r]
    for k1 in range(s1):
        rows = pl.ds(k1 * 2 * s2, 2 * s2)
        z_ref[rows, :] = jnp.dot(t2_ref[...], b_ref[rows, :], preferred_element_type=F32).astype(BF16)
    for part, out_ref in ((0, zr_ref), (1, zi_ref)):
        for jp in range(s2 // r):
            g = jnp.concatenate([z_ref[pl.ds(k1 * 2 * s2 + part * s2 + r * jp, r), :]
                                 for k1 in range(s1)], axis=0)
            out_ref[0, pl.ds(s1 * r * jp, s1 * r), :] = jnp.dot(
                p_ref[...], g, preferred_element_type=F32).astype(BF16)


def _pos_dft_two_stage(h3, tc=256):
    b, n, d = h3.shape
    tc = min(tc, d)
    t1, t2, perm = _fft_tables(n)
    const = lambda a: pl.BlockSpec(a.shape, lambda bi, ci: (0,) * a.ndim)
    io = pl.BlockSpec((1, n, tc), lambda bi, ci: (bi, 0, ci))
    out = jax.ShapeDtypeStruct((b, n, d), BF16)
    return pl.pallas_call(
        _fft_kernel,
        grid=(b, d // tc),
        in_specs=[const(t1), const(t2), const(perm), io],
        out_specs=[io, io],
        out_shape=[out, out],
        scratch_shapes=[pltpu.VMEM((2 * n, tc), BF16), pltpu.VMEM((2 * n, tc), BF16)],
        compiler_params=_params("parallel", "parallel"),
        name="pos_fft",
    )(t1, t2, perm, h3)


def _chan_kernel(x_ref, zr_ref, zi_ref, tab_ref, w_ref, gt_ref, o_ref, f_ref):
    gd = tab_ref.shape[1]
    tab = tab_ref[...]
    for grp in range(FOURIER_GROUPS):
        cols = slice(grp * gd, (grp + 1) * gd)
        z = jnp.concatenate([zr_ref[:, cols], zi_ref[:, cols]], axis=-1)
        f_ref[:, cols] = jnp.dot(z, tab, preferred_element_type=F32).astype(BF16)
    y = jnp.dot(f_ref[...], w_ref[...], preferred_element_type=F32)
    o_ref[...] = x_ref[...] + gt_ref[0] * y


def _chan_mix(x2, zr2, zi2, tab, w_out, mods, k_gate, tm=512):
    rows, d = x2.shape
    tm = mods.tile(tm, rows)
    return pl.pallas_call(
        _chan_kernel,
        grid=(rows // tm,),
        in_specs=[
            pl.BlockSpec((tm, d), lambda i: (i, 0)),
            pl.BlockSpec((tm, d), lambda i: (i, 0)),
            pl.BlockSpec((tm, d), lambda i: (i, 0)),
            pl.BlockSpec(tab.shape, lambda i: (0, 0)),
            pl.BlockSpec((d, d), lambda i: (0, 0)),
            mods.spec(k_gate, tm, d),
        ],
        out_specs=pl.BlockSpec((tm, d), lambda i: (i, 0)),
        out_shape=jax.ShapeDtypeStruct((rows, d), F32),
        scratch_shapes=[pltpu.VMEM((tm, d), BF16)],
        compiler_params=_params("parallel"),
        name="chan_mix",
    )(x2, zr2, zi2, tab, w_out, mods.table)


def _fourier_layer(x3, norm_g, gi, mods, w_out, chan_tab):
    b, n, d = x3.shape
    x2 = x3.reshape(b * n, d)
    h = _modulated(x2, norm_g, gi, mods, 3).reshape(b, n, d)
    if n == FFT_SLABS * FFT_SLABS * FFT_CHUNK:
        zr, zi = _pos_dft_two_stage(h)
    else:
        zr, zi = _pos_dft(h, *_dft_tables(n))
    out = _chan_mix(x2, zr.reshape(b * n, d), zi.reshape(b * n, d), chan_tab, w_out, mods, 5)
    return out.reshape(b, n, d)


def _rope_tables(n):
    pos = jnp.arange(n, dtype=jnp.int32)
    row = (pos // GRID_W).astype(F32)
    col = (pos % GRID_W).astype(F32)
    inv_freq = ROPE_THETA ** (-jnp.arange(ROPE_FREQS, dtype=F32) / ROPE_FREQS)
    a_r = row[:, None] * inv_freq
    a_c = col[:, None] * inv_freq
    ang = jnp.concatenate([a_r, a_r, a_c, a_c], axis=-1)
    cos, sin = jnp.cos(ang), jnp.sin(ang)
    first_half = (jnp.arange(HEAD_DIM) % (2 * ROPE_FREQS)) < ROPE_FREQS
    sin_from_upper = jnp.where(first_half[None, :], -sin, 0.0)
    sin_from_lower = jnp.where(first_half[None, :], 0.0, sin)
    return cos, sin_from_upper, sin_from_lower


def _head_norm(t, g):
    ms = jnp.mean(t * t, axis=-1, keepdims=True)
    return t * lax.rsqrt(ms + EPS) * g


def _rope(t, cos, s_up, s_lo):
    up = pltpu.roll(t, HEAD_DIM - ROPE_FREQS, 1)
    lo = pltpu.roll(t, ROPE_FREQS, 1)
    return t * cos + up * s_up + lo * s_lo


def _qkv_kernel(x_ref, g_ref, sh_ref, sc_ref, w_ref, qg_ref, kg_ref, *refs,
                n_q_heads, use_rope, q_scale):
    if use_rope:
        cos, s_up, s_lo = (r[...] for r in refs[:3])
        refs = refs[3:]
    q_ref = refs[0] if n_q_heads else None
    k_ref, v_ref, raw_even_ref, raw_odd_ref = refs[-4:]
    step = pl.program_id(0)
    qk_cols = raw_even_ref.shape[1]

    @pl.when(step == 0)
    def _():
        raw_odd_ref[...] = jnp.zeros_like(raw_odd_ref)

    def run(raw_new_ref, raw_old_ref):
        h = _modulate(x_ref[...], g_ref[0], sh_ref[0], sc_ref[0]).astype(BF16)
        y = jnp.dot(h, w_ref[...], preferred_element_type=F32)
        raw_new_ref[...] = y[:, :qk_cols]
        v_ref[...] = y[:, qk_cols:].astype(BF16)
        for hd in range(n_q_heads + N_KV_HEADS):
            is_q = hd < n_q_heads
            gain = (qg_ref[...] * q_scale) if is_q else kg_ref[...]
            t = _head_norm(raw_old_ref[:, hd * HEAD_DIM:(hd + 1) * HEAD_DIM], gain)
            if use_rope:
                t = _rope(t, cos, s_up, s_lo)
            if is_q:
                q_ref[:, hd * HEAD_DIM:(hd + 1) * HEAD_DIM] = t.astype(BF16)
            else:
                kh = hd - n_q_heads
                k_ref[:, kh * HEAD_DIM:(kh + 1) * HEAD_DIM] = t.astype(BF16)

    @pl.when(step % 2 == 0)
    def _():
        run(raw_even_ref, raw_odd_ref)

    @pl.when(step % 2 == 1)
    def _():
        run(raw_odd_ref, raw_even_ref)


def _qkv_proj(x2, norm_g, gi, mods, k0, w, q_g, k_g, rope, n_q_heads, seq, tm=512):
    rows, d = x2.shape
    tm = mods.tile(tm, rows)
    n_tiles = rows // tm
    kv_dim = N_KV_HEADS * HEAD_DIM
    q_dim = n_q_heads * HEAD_DIM
    width = q_dim + 2 * kv_dim
    col_block, rem = divmod(w.shape[1] - width, width)
    assert rem == 0
    use_rope = rope is not None
    rope = rope if use_rope else ()
    tiles_per_seq = seq // tm
    kern = functools.partial(_qkv_kernel, n_q_heads=n_q_heads, use_rope=use_rope,
                             q_scale=math.log2(math.e) / math.sqrt(HEAD_DIM))
    cur = lambda i: jnp.minimum(i, n_tiles - 1)
    prev = lambda i: jnp.maximum(i - 1, 0)
    head_vec = pl.BlockSpec((1, HEAD_DIM), lambda i: (0, 0))
    rope_spec = pl.BlockSpec((tm, HEAD_DIM), lambda i: (prev(i) % tiles_per_seq, 0))
    out_widths = ([q_dim] if n_q_heads else []) + [kv_dim, kv_dim]
    qk_specs = [pl.BlockSpec((tm, wd), lambda i: (prev(i), 0)) for wd in out_widths[:-1]]
    return pl.pallas_call(
        kern,
        grid=(n_tiles + 1,),
        in_specs=[
            pl.BlockSpec((tm, d), lambda i: (cur(i), 0)),
            pl.BlockSpec((1, 1, d), lambda i: (gi, 0, 0)),
            mods.spec(k0, tm, d, cur), mods.spec(k0 + 1, tm, d, cur),
            pl.BlockSpec((d, width), lambda i: (0, col_block)),
            head_vec, head_vec,
        ] + [rope_spec] * len(rope),
        out_specs=qk_specs + [pl.BlockSpec((tm, kv_dim), lambda i: (cur(i), 0))],
        out_shape=[jax.ShapeDtypeStruct((rows, wd), BF16) for wd in out_widths],
        scratch_shapes=[pltpu.VMEM((tm, q_dim + kv_dim), F32)] * 2,
        compiler_params=_params("arbitrary"),
        name="qkv_proj",
    )(x2, norm_g, mods.table, mods.table, w,
      q_g.reshape(1, HEAD_DIM), k_g.reshape(1, HEAD_DIM), *rope)


def _attn_kernel(q_ref, kc_ref, vc_ref, kl_ref, vl_ref, o_ref, k_ref, v_ref, *, group):
    l = kc_ref.shape[1]

    @pl.when(pl.program_id(2) == 0)
    def _():
        k_ref[:l] = kc_ref[0]
        k_ref[l:] = kl_ref[0]
        v_ref[:l, :HEAD_DIM] = vc_ref[0]
        v_ref[l:, :HEAD_DIM] = vl_ref[0]
        v_ref[:, HEAD_DIM:] = jnp.ones((v_ref.shape[0], HEAD_DIM), BF16)

    nt = (((1,), (1,)), ((), ()))
    k, v = k_ref[...], v_ref[...]
    tq = q_ref.shape[1]
    sub = min(tq, ATTN_SUB_ROWS)
    items = [(r0, sub, gi) for r0 in range(0, tq, sub) for gi in range(group)]
    halve = lambda it: [(it[0], it[1] // 2, it[2]), (it[0] + it[1] // 2, it[1] // 2, it[2])]
    if len(items) > 1:
        items = halve(items[0]) + items[1:-1] + halve(items[-1])
    for r0, nrows, gi in items:
        rows = slice(r0, r0 + nrows)
        cols = slice(gi * HEAD_DIM, (gi + 1) * HEAD_DIM)
        s = lax.dot_general(q_ref[0, rows, cols], k, nt, preferred_element_type=F32)
        p = jnp.exp2(s - jnp.max(s, axis=-1, keepdims=True)).astype(BF16)
        oe = jnp.dot(p, v, preferred_element_type=F32)
        o_ref[0, rows, cols] = (oe[:, :HEAD_DIM] / oe[:, HEAD_DIM:HEAD_DIM + 1]).astype(BF16)


def _attention(q3, kc3, vc3, kl3, vl3, tq=1024):
    b, s, qd = q3.shape
    tq = min(tq, s)
    l = kc3.shape[1]
    group = qd // (N_KV_HEADS * HEAD_DIM)
    gw = group * HEAD_DIM
    kv_spec = lambda n: pl.BlockSpec((1, n, HEAD_DIM), lambda bi, kh, i: (bi, 0, kh))
    return pl.pallas_call(
        functools.partial(_attn_kernel, group=group),
        grid=(b, N_KV_HEADS, s // tq),
        in_specs=[
            pl.BlockSpec((1, tq, gw), lambda bi, kh, i: (bi, i, kh)),
            kv_spec(l), kv_spec(l), kv_spec(s), kv_spec(s),
        ],
        out_specs=pl.BlockSpec((1, tq, gw), lambda bi, kh, i: (bi, i, kh)),
        out_shape=jax.ShapeDtypeStruct((b, s, qd), BF16),
        scratch_shapes=[pltpu.VMEM((l + s, HEAD_DIM), BF16), pltpu.VMEM((l + s, 2 * HEAD_DIM), BF16)],
        compiler_params=_params("parallel", "parallel", "arbitrary"),
        name="attention",
    )(q3, kc3, vc3, kl3, vl3)


def _oproj_kernel(x_ref, a_ref, w_ref, gt_ref, o_ref):
    y = jnp.dot(a_ref[...], w_ref[...], preferred_element_type=F32)
    o_ref[...] = x_ref[...] + gt_ref[0] * y


def _out_proj(x2, a2, w, mods, k_gate, tm=512):
    rows, d = x2.shape
    tm = mods.tile(tm, rows)
    return pl.pallas_call(
        _oproj_kernel,
        grid=(rows // tm,),
        in_specs=[
            pl.BlockSpec((tm, d), lambda i: (i, 0)),
            pl.BlockSpec((tm, a2.shape[1]), lambda i: (i, 0)),
            pl.BlockSpec(w.shape, lambda i: (0, 0)),
            mods.spec(k_gate, tm, d),
        ],
        out_specs=pl.BlockSpec((tm, d), lambda i: (i, 0)),
        out_shape=jax.ShapeDtypeStruct((rows, d), F32),
        compiler_params=_params("parallel"),
        name="out_proj",
    )(x2, a2, w, mods.table)


def kernel(x, c, ctx, c_ctx, w_ada, b_ada, norm_g, w_ffn_in, w_ffn_out, w_fourier_out, w_qkv,
           q_norm_g, k_norm_g, w_attn_out):
    b, n, d = x.shape
    l = ctx.shape[1]
    depth = w_ada.shape[0]
    assert depth == 2 and b <= 8, "layer 0 = Fourier mixer, layer 1 = attention mixer"
    q_dim = w_attn_out.shape[1]
    n_heads = q_dim // HEAD_DIM

    s16 = jnp.zeros((MOD_ROWS, d), F32).at[:b].set(jax.nn.silu(c)).at[8].set(jax.nn.silu(c_ctx))
    table = _ada(s16.astype(BF16), w_ada, b_ada).reshape(depth * MOD_ROWS * N_MOD, 1, d)

    f = w_ffn_out.shape[2]
    w_in_all = w_ffn_in.reshape(depth * 2, d, 2 * f)
    w_out_all = w_ffn_out.reshape(depth * 2, f, d)
    w_in, w_out = w_in_all[0].astype(BF16), w_out_all[0].astype(BF16)
    ng = norm_g.reshape(depth * 3, 1, d)
    w_four = w_fourier_out.astype(BF16)
    w_qkv_b = w_qkv.astype(BF16)
    w_o = w_attn_out.astype(BF16)

    gd = d // FOURIER_GROUPS
    cos_c, sin_c = _dft_tables(gd)
    chan_tab = jnp.concatenate([cos_c, -sin_c], axis=0)

    xl = x.reshape(b * n, d)
    xc = ctx.reshape(b * l, d)

    ml = _Mods(table, 0, n, ctx=False)
    mc = _Mods(table, 0, l, ctx=True)
    xc = _ffn(xc, ng, 0, mc, 0, w_in, w_out)
    xl, w_in, w_out = _ffn(xl, ng, 0, ml, 0, w_in, w_out, cast_next=(w_in_all, w_out_all, 1))
    xl = _fourier_layer(xl.reshape(b, n, d), ng, 1, ml, w_four[0], chan_tab).reshape(b * n, d)
    xc = _fourier_layer(xc.reshape(b, l, d), ng, 1, mc, w_four[0], chan_tab).reshape(b * l, d)
    xc = _ffn(xc, ng, 2, mc, 6, w_in, w_out)
    xl, w_in, w_out = _ffn(xl, ng, 2, ml, 6, w_in, w_out, cast_next=(w_in_all, w_out_all, 2))

    ml = _Mods(table, 1, n, ctx=False)
    mc = _Mods(table, 1, l, ctx=True)
    xc = _ffn(xc, ng, 3, mc, 0, w_in, w_out)
    xl, w_in, w_out = _ffn(xl, ng, 3, ml, 0, w_in, w_out, cast_next=(w_in_all, w_out_all, 3))
    rope = _rope_tables(n)
    q, kl, vl = _qkv_proj(xl, ng, 4, ml, 3, w_qkv_b[0], q_norm_g[0], k_norm_g[0], rope, n_heads, n)
    kc, vc = _qkv_proj(xc, ng, 4, mc, 3, w_qkv_b[0], q_norm_g[0], k_norm_g[0], None, 0, l)
    kv_dim = N_KV_HEADS * HEAD_DIM
    o = _attention(q.reshape(b, n, q_dim), kc.reshape(b, l, kv_dim), vc.reshape(b, l, kv_dim),
                   kl.reshape(b, n, kv_dim), vl.reshape(b, n, kv_dim))
    xl = _out_proj(xl, o.reshape(b * n, q_dim), w_o[0], ml, 5)
    xl = _ffn(xl, ng, 5, ml, 6, w_in, w_out)
    return xl.reshape(b, n, d)
```

```python
import functools
import math

import jax
import jax.numpy as jnp
from jax import lax
from jax.experimental import pallas as pl
from jax.experimental.pallas import tpu as pltpu

F32 = jnp.float32
BF16 = jnp.bfloat16

GRID_W = 64
FOURIER_GROUPS = 8
HEAD_DIM = 128
N_KV_HEADS = 4
ROPE_FREQS = HEAD_DIM // 4
ROPE_THETA = 10000.0
N_MOD = 9
EPS = 1e-6
MOD_ROWS = 16
DFT_TABLE_SPLIT = 64
ATTN_SUB_ROWS = 256
FFN_FIRST_SUB_ROWS = 256
BF16_SUBLANES = 16
FFT_CHUNK = 16
FFT_SLABS = 16

VMEM_LIMIT_BYTES = 64 * 1024 * 1024


def _params(*sem):
    return pltpu.CompilerParams(dimension_semantics=sem, vmem_limit_bytes=VMEM_LIMIT_BYTES)


def _modulate(x, g, shift, scale):
    ms = jnp.mean(x * x, axis=-1, keepdims=True)
    return (x * lax.rsqrt(ms + EPS)) * g * (1.0 + scale) + shift


def _ada_kernel(s_ref, w_ref, b_ref, o_ref):
    w = w_ref[0].astype(BF16)
    o_ref[0] = jnp.dot(s_ref[...], w, preferred_element_type=F32) + b_ref[0]


def _ada(s16, w_ada, b_ada, tn=1024):
    depth, d, n = w_ada.shape
    return pl.pallas_call(
        _ada_kernel,
        grid=(depth, n // tn),
        in_specs=[
            pl.BlockSpec((MOD_ROWS, d), lambda l, j: (0, 0)),
            pl.BlockSpec((1, d, tn), lambda l, j: (l, 0, j)),
            pl.BlockSpec((1, 1, tn), lambda l, j: (l, 0, j)),
        ],
        out_specs=pl.BlockSpec((1, MOD_ROWS, tn), lambda l, j: (l, 0, j)),
        out_shape=jax.ShapeDtypeStruct((depth, MOD_ROWS, n), F32),
        compiler_params=_params("parallel", "parallel"),
        name="ada",
    )(s16, w_ada, b_ada.reshape(depth, 1, n))


class _Mods:
    def __init__(self, table, layer, rows_per_mod, ctx):
        self.table, self.layer, self.rows_per_mod, self.ctx = table, layer, rows_per_mod, ctx

    def tile(self, tm, rows):
        tm = min(tm, rows if self.ctx else self.rows_per_mod)
        assert rows % tm == 0 and (self.ctx or self.rows_per_mod % tm == 0)
        return tm

    def spec(self, k, tm, d, tile_of_step=lambda i: i):
        layer, rpm, ctx = self.layer, self.rows_per_mod, self.ctx

        def index(i, *_):
            row = (MOD_ROWS - 8) if ctx else (tile_of_step(i) * tm) // rpm
            return ((layer * MOD_ROWS + row) * N_MOD + k, 0, 0)

        return pl.BlockSpec((1, 1, d), index)


def _ffn_kernel(x_hbm_ref, g_ref, sh_ref, sc_ref, gt_ref, wg_a_ref, wu_a_ref, wo_a_ref,
                wg_b_ref, wu_b_ref, wo_b_ref, *refs):
    if len(refs) == 4:
        o_ref, h_ref, x_ref, x_sem = refs
    else:
        next_in_ref, o_ref, next_in_bf_ref, h_ref, x_ref, x_sem = refs
        next_in_bf_ref[...] = next_in_ref[...].astype(BF16)
    i, k = pl.program_id(0), pl.program_id(1)
    tm = x_ref.shape[0]

    def x_copy(tile):
        return pltpu.make_async_copy(x_hbm_ref.at[pl.ds(tile * tm, tm), :], x_ref, x_sem)

    def swiglu(h, wg_ref, wu_ref, wo_ref):
        gate = jnp.dot(h, wg_ref[...], preferred_element_type=F32)
        up = jnp.dot(h, wu_ref[...], preferred_element_type=F32)
        act = (jax.nn.silu(gate) * up).astype(BF16)
        return jnp.dot(act, wo_ref[...], preferred_element_type=F32)

    half_gate = 0.5 * gt_ref[0]

    @pl.when((k == 0) & (i == 0))
    def _():
        x_copy(0).start()

    @pl.when(k == 0)
    def _():
        x_copy(i).wait()
        gain = g_ref[0] * (1.0 + sc_ref[0])
        shift = sh_ref[0]
        sub = min(FFN_FIRST_SUB_ROWS, tm)
        for r0 in range(0, tm, sub):
            rows = slice(r0, r0 + sub)
            x = x_ref[rows, :]
            ms = jnp.mean(x * x, axis=-1, keepdims=True)
            h = ((x * lax.rsqrt(ms + EPS)) * gain + shift).astype(BF16)
            h_ref[rows, :] = h
            o_ref[rows, :] = x + half_gate * swiglu(h, wg_a_ref, wu_a_ref, wo_a_ref)

    @pl.when((k == 1) & (i + 1 < pl.num_programs(0)))
    def _():
        x_copy(i + 1).start()

    @pl.when(k > 0)
    def _():
        h = h_ref[...]
        o_ref[...] += half_gate * (swiglu(h, wg_a_ref, wu_a_ref, wo_a_ref)
                                   + swiglu(h, wg_b_ref, wu_b_ref, wo_b_ref))


def _ffn(x2, norm_g, gi, mods, k0, w_in, w_out, wo, cast_next=None, tm=1024, tf=512):
    rows, d = x2.shape
    f = w_out.shape[1]
    tm, tf = mods.tile(tm, rows), min(tf, f)
    ni, nk = rows // tm, f // tf
    assert nk % 2 == 1 and nk >= 3, "step 0 takes one hidden chunk, every later step two"
    nsteps = (nk + 1) // 2
    first = lambda k: jnp.maximum(2 * k - 1, 0)
    second = lambda k: 2 * k
    in_specs = [
        pl.BlockSpec(memory_space=pl.ANY),
        pl.BlockSpec((1, 1, d), lambda i, k: (gi, 0, 0)),
        mods.spec(k0, tm, d), mods.spec(k0 + 1, tm, d), mods.spec(k0 + 2, tm, d),
    ]
    for chunk in (first, second):
        in_specs += [
            pl.BlockSpec((d, tf), lambda i, k, c=chunk: (0, c(k))),
            pl.BlockSpec((d, tf), lambda i, k, c=chunk: (0, nk + c(k))),
            pl.BlockSpec((None, tf, d), lambda i, k, c=chunk: (wo, c(k), 0)),
        ]
    out_specs = [pl.BlockSpec((tm, d), lambda i, k: (i, 0))]
    out_shape = [jax.ShapeDtypeStruct((rows, d), F32)]
    operands = [x2, norm_g, mods.table, mods.table, mods.table] + [w_in, w_in, w_out] * 2
    if cast_next is not None:
        w_in_all, slab = cast_next
        units = d // ni // BF16_SUBLANES
        assert units * ni * BF16_SUBLANES == d
        nsub = max(s for s in range(1, nsteps) if units % s == 0)
        rb = d // ni // nsub
        sub_of = lambda i, k: i * nsub + jnp.clip(k - 1, 0, nsub - 1)
        in_specs += [pl.BlockSpec((None, rb, 2 * f), lambda i, k: (slab, sub_of(i, k), 0))]
        out_specs += [pl.BlockSpec((rb, 2 * f), lambda i, k: (sub_of(i, k), 0))]
        out_shape += [jax.ShapeDtypeStruct((d, 2 * f), BF16)]
        operands += [w_in_all]
    res = pl.pallas_call(
        _ffn_kernel,
        grid=(ni, nsteps),
        in_specs=in_specs,
        out_specs=out_specs,
        out_shape=out_shape,
        scratch_shapes=[pltpu.VMEM((tm, d), BF16), pltpu.VMEM((tm, d), F32),
                        pltpu.SemaphoreType.DMA(())],
        compiler_params=_params("arbitrary", "arbitrary"),
        name="ffn",
    )(*operands)
    return res[0] if cast_next is None else res


def _mod_kernel(x_ref, g_ref, sh_ref, sc_ref, o_ref):
    o_ref[...] = _modulate(x_ref[...], g_ref[0], sh_ref[0], sc_ref[0]).astype(o_ref.dtype)


def _modulated(x2, norm_g, gi, mods, k0, tm=512):
    rows, d = x2.shape
    tm = mods.tile(tm, rows)
    return pl.pallas_call(
        _mod_kernel,
        grid=(rows // tm,),
        in_specs=[
            pl.BlockSpec((tm, d), lambda i: (i, 0)),
            pl.BlockSpec((1, 1, d), lambda i: (gi, 0, 0)),
            mods.spec(k0, tm, d), mods.spec(k0 + 1, tm, d),
        ],
        out_specs=pl.BlockSpec((tm, d), lambda i: (i, 0)),
        out_shape=jax.ShapeDtypeStruct((rows, d), BF16),
        compiler_params=_params("parallel"),
        name="modulate",
    )(x2, norm_g, mods.table, mods.table)


def _dft_tables(n):
    def unit(rows, cols):
        phase = (rows[:, None] * cols[None, :]) % n
        ang = phase.astype(F32) * (2.0 * math.pi / n)
        return jnp.cos(ang), jnp.sin(ang)

    idx = jnp.arange(n, dtype=jnp.int32)
    s = 1.0 / math.sqrt(n)
    m = DFT_TABLE_SPLIT
    if n <= m or n % m:
        c, sn = unit(idx, idx)
    else:
        ca, sa = unit(idx, jnp.arange(n // m, dtype=jnp.int32) * m)
        cb, sb = unit(idx, jnp.arange(m, dtype=jnp.int32))
        c = (ca[:, :, None] * cb[:, None, :] - sa[:, :, None] * sb[:, None, :]).reshape(n, n)
        sn = (sa[:, :, None] * cb[:, None, :] + ca[:, :, None] * sb[:, None, :]).reshape(n, n)
    return (c * s).astype(BF16), (sn * s).astype(BF16)


def _pos_dft_kernel(c_ref, s_ref, h_ref, zr_ref, zi_ref):
    h = h_ref[0]
    zr_ref[0] = jnp.dot(c_ref[...], h, preferred_element_type=F32).astype(BF16)
    zi_ref[0] = jnp.dot(s_ref[...], h, preferred_element_type=F32).astype(BF16)


def _pos_dft(h3, cos_n, sin_n, tk=256):
    b, n, d = h3.shape
    tk = min(tk, n)
    out = jax.ShapeDtypeStruct((b, n, d), BF16)
    return pl.pallas_call(
        _pos_dft_kernel,
        grid=(b, n // tk),
        in_specs=[
            pl.BlockSpec((tk, n), lambda bi, i: (i, 0)),
            pl.BlockSpec((tk, n), lambda bi, i: (i, 0)),
            pl.BlockSpec((1, n, d), lambda bi, i: (bi, 0, 0), pipeline_mode=pl.Buffered(1)),
        ],
        out_specs=[pl.BlockSpec((1, tk, d), lambda bi, i: (bi, i, 0))] * 2,
        out_shape=[out, out],
        compiler_params=_params("parallel", "parallel"),
        name="pos_dft",
    )(cos_n, sin_n, h3)


def _fft_tables(n):
    s1, r = FFT_SLABS, FFT_CHUNK
    s2 = n // s1
    assert s1 * s2 == n and s1 * r == s2 and s2 % r == 0
    ar = lambda m: jnp.arange(m, dtype=jnp.int32)
    shape = (s2 // r, 2 * s2, s1 * r)
    j, p, q = (lax.broadcasted_iota(jnp.int32, shape, a) for a in range(3))
    part, k1, rr, n1, rr2 = p // s2, (p % s2) // r, p % r, q // r, q % r
    ang = ((k1 * (s2 * n1 + r * j + rr)) % n).astype(F32) * (2.0 * math.pi / n)
    t1 = jnp.where(rr == rr2, jnp.where(part == 0, jnp.cos(ang), -jnp.sin(ang)), 0.0).astype(BF16)
    phi = ((ar(s2)[:, None] * ar(s2)[None, :]) % s2).astype(F32) * (2.0 * math.pi / s2)
    c, s = jnp.cos(phi) / math.sqrt(n), jnp.sin(phi) / math.sqrt(n)
    t2 = jnp.concatenate([jnp.concatenate([c, s], axis=1), jnp.concatenate([s, -c], axis=1)],
                         axis=0).astype(BF16)
    src = (ar(s1)[None, :] * r + ar(r)[:, None]).reshape(-1)
    perm = (src[:, None] == ar(s1 * r)[None, :]).astype(BF16)
    return t1, t2, perm


def _fft_kernel(t1_ref, t2_ref, p_ref, h_ref, zr_ref, zi_ref, b_ref, z_ref):
    s1, r = FFT_SLABS, FFT_CHUNK
    s2 = h_ref.shape[1] // s1
    for j in range(s2 // r):
        g = jnp.concatenate([h_ref[0, pl.ds(s2 * n1 + r * j, r), :] for n1 in range(s1)], axis=0)
        res = jnp.dot(t1_ref[j], g, preferred_element_type=F32).astype(BF16)
        for part in range(2):
            for k1 in range(s1):
                row = part * s2 + k1 * r
                b_ref[pl.ds(k1 * 2 * s2 + part * s2 + r * j, r), :] = res[row:row + r]
    for k1 in range(s1):
        rows = pl.ds(k1 * 2 * s2, 2 * s2)
        z_ref[rows, :] = jnp.dot(t2_ref[...], b_ref[rows, :], preferred_element_type=F32).astype(BF16)
    for part, out_ref in ((0, zr_ref), (1, zi_ref)):
        for jp in range(s2 // r):
            g = jnp.concatenate([z_ref[pl.ds(k1 * 2 * s2 + part * s2 + r * jp, r), :]
                                 for k1 in range(s1)], axis=0)
            out_ref[0, pl.ds(s1 * r * jp, s1 * r), :] = jnp.dot(
                p_ref[...], g, preferred_element_type=F32).astype(BF16)


def _pos_dft_two_stage(h3, tc=256):
    b, n, d = h3.shape
    tc = min(tc, d)
    t1, t2, perm = _fft_tables(n)
    const = lambda a: pl.BlockSpec(a.shape, lambda bi, ci: (0,) * a.ndim)
    io = pl.BlockSpec((1, n, tc), lambda bi, ci: (bi, 0, ci))
    out = jax.ShapeDtypeStruct((b, n, d), BF16)
    return pl.pallas_call(
        _fft_kernel,
        grid=(b, d // tc),
        in_specs=[const(t1), const(t2), const(perm), io],
        out_specs=[io, io],
        out_shape=[out, out],
        scratch_shapes=[pltpu.VMEM((2 * n, tc), BF16), pltpu.VMEM((2 * n, tc), BF16)],
        compiler_params=_params("parallel", "parallel"),
        name="pos_fft",
    )(t1, t2, perm, h3)


def _chan_kernel(x_ref, zr_ref, zi_ref, tab_ref, w_ref, gt_ref, o_ref, f_ref):
    gd = tab_ref.shape[1]
    tab = tab_ref[...]
    for grp in range(FOURIER_GROUPS):
        cols = slice(grp * gd, (grp + 1) * gd)
        z = jnp.concatenate([zr_ref[:, cols], zi_ref[:, cols]], axis=-1)
        f_ref[:, cols] = jnp.dot(z, tab, preferred_element_type=F32).astype(BF16)
    y = jnp.dot(f_ref[...], w_ref[...], preferred_element_type=F32)
    o_ref[...] = x_ref[...] + gt_ref[0] * y


def _chan_mix(x2, zr2, zi2, tab, w_out, mods, k_gate, tm=512):
    rows, d = x2.shape
    tm = mods.tile(tm, rows)
    return pl.pallas_call(
        _chan_kernel,
        grid=(rows // tm,),
        in_specs=[
            pl.BlockSpec((tm, d), lambda i: (i, 0)),
            pl.BlockSpec((tm, d), lambda i: (i, 0)),
            pl.BlockSpec((tm, d), lambda i: (i, 0)),
            pl.BlockSpec(tab.shape, lambda i: (0, 0)),
            pl.BlockSpec((d, d), lambda i: (0, 0)),
            mods.spec(k_gate, tm, d),
        ],
        out_specs=pl.BlockSpec((tm, d), lambda i: (i, 0)),
        out_shape=jax.ShapeDtypeStruct((rows, d), F32),
        scratch_shapes=[pltpu.VMEM((tm, d), BF16)],
        compiler_params=_params("parallel"),
        name="chan_mix",
    )(x2, zr2, zi2, tab, w_out, mods.table)


def _fourier_layer(x3, norm_g, gi, mods, w_out, chan_tab):
    b, n, d = x3.shape
    x2 = x3.reshape(b * n, d)
    h = _modulated(x2, norm_g, gi, mods, 3).reshape(b, n, d)
    if n == FFT_SLABS * FFT_SLABS * FFT_CHUNK:
        zr, zi = _pos_dft_two_stage(h)
    else:
        zr, zi = _pos_dft(h, *_dft_tables(n))
    out = _chan_mix(x2, zr.reshape(b * n, d), zi.reshape(b * n, d), chan_tab, w_out, mods, 5)
    return out.reshape(b, n, d)


def _rope_tables(n):
    pos = jnp.arange(n, dtype=jnp.int32)
    row = (pos // GRID_W).astype(F32)
    col = (pos % GRID_W).astype(F32)
    inv_freq = ROPE_THETA ** (-jnp.arange(ROPE_FREQS, dtype=F32) / ROPE_FREQS)
    a_r = row[:, None] * inv_freq
    a_c = col[:, None] * inv_freq
    ang = jnp.concatenate([a_r, a_r, a_c, a_c], axis=-1)
    cos, sin = jnp.cos(ang), jnp.sin(ang)
    first_half = (jnp.arange(HEAD_DIM) % (2 * ROPE_FREQS)) < ROPE_FREQS
    sin_from_upper = jnp.where(first_half[None, :], -sin, 0.0)
    sin_from_lower = jnp.where(first_half[None, :], 0.0, sin)
    return cos, sin_from_upper, sin_from_lower


def _head_norm(t, g):
    ms = jnp.mean(t * t, axis=-1, keepdims=True)
    return t * lax.rsqrt(ms + EPS) * g


def _rope(t, cos, s_up, s_lo):
    up = pltpu.roll(t, HEAD_DIM - ROPE_FREQS, 1)
    lo = pltpu.roll(t, ROPE_FREQS, 1)
    return t * cos + up * s_up + lo * s_lo


def _qkv_kernel(x_ref, g_ref, sh_ref, sc_ref, w_ref, qg_ref, kg_ref, *refs,
                n_q_heads, use_rope, q_scale):
    if use_rope:
        cos, s_up, s_lo = (r[...] for r in refs[:3])
        refs = refs[3:]
    q_ref = refs[0] if n_q_heads else None
    k_ref, v_ref, raw_even_ref, raw_odd_ref = refs[-4:]
    step = pl.program_id(0)
    qk_cols = raw_even_ref.shape[1]

    @pl.when(step == 0)
    def _():
        raw_odd_ref[...] = jnp.zeros_like(raw_odd_ref)

    def run(raw_new_ref, raw_old_ref):
        h = _modulate(x_ref[...], g_ref[0], sh_ref[0], sc_ref[0]).astype(BF16)
        y = jnp.dot(h, w_ref[...], preferred_element_type=F32)
        raw_new_ref[...] = y[:, :qk_cols]
        v_ref[...] = y[:, qk_cols:].astype(BF16)
        for hd in range(n_q_heads + N_KV_HEADS):
            is_q = hd < n_q_heads
            gain = (qg_ref[...] * q_scale) if is_q else kg_ref[...]
            t = _head_norm(raw_old_ref[:, hd * HEAD_DIM:(hd + 1) * HEAD_DIM], gain)
            if use_rope:
                t = _rope(t, cos, s_up, s_lo)
            if is_q:
                q_ref[:, hd * HEAD_DIM:(hd + 1) * HEAD_DIM] = t.astype(BF16)
            else:
                kh = hd - n_q_heads
                k_ref[:, kh * HEAD_DIM:(kh + 1) * HEAD_DIM] = t.astype(BF16)

    @pl.when(step % 2 == 0)
    def _():
        run(raw_even_ref, raw_odd_ref)

    @pl.when(step % 2 == 1)
    def _():
        run(raw_odd_ref, raw_even_ref)


def _qkv_proj(x2, norm_g, gi, mods, k0, w, q_g, k_g, rope, n_q_heads, seq, tm=512):
    rows, d = x2.shape
    tm = mods.tile(tm, rows)
    n_tiles = rows // tm
    kv_dim = N_KV_HEADS * HEAD_DIM
    q_dim = n_q_heads * HEAD_DIM
    width = q_dim + 2 * kv_dim
    col_block, rem = divmod(w.shape[1] - width, width)
    assert rem == 0
    use_rope = rope is not None
    rope = rope if use_rope else ()
    tiles_per_seq = seq // tm
    kern = functools.partial(_qkv_kernel, n_q_heads=n_q_heads, use_rope=use_rope,
                             q_scale=math.log2(math.e) / math.sqrt(HEAD_DIM))
    cur = lambda i: jnp.minimum(i, n_tiles - 1)
    prev = lambda i: jnp.maximum(i - 1, 0)
    head_vec = pl.BlockSpec((1, HEAD_DIM), lambda i: (0, 0))
    rope_spec = pl.BlockSpec((tm, HEAD_DIM), lambda i: (prev(i) % tiles_per_seq, 0))
    out_widths = ([q_dim] if n_q_heads else []) + [kv_dim, kv_dim]
    qk_specs = [pl.BlockSpec((tm, wd), lambda i: (prev(i), 0)) for wd in out_widths[:-1]]
    return pl.pallas_call(
        kern,
        grid=(n_tiles + 1,),
        in_specs=[
            pl.BlockSpec((tm, d), lambda i: (cur(i), 0)),
            pl.BlockSpec((1, 1, d), lambda i: (gi, 0, 0)),
            mods.spec(k0, tm, d, cur), mods.spec(k0 + 1, tm, d, cur),
            pl.BlockSpec((d, width), lambda i: (0, col_block)),
            head_vec, head_vec,
        ] + [rope_spec] * len(rope),
        out_specs=qk_specs + [pl.BlockSpec((tm, kv_dim), lambda i: (cur(i), 0))],
        out_shape=[jax.ShapeDtypeStruct((rows, wd), BF16) for wd in out_widths],
        scratch_shapes=[pltpu.VMEM((tm, q_dim + kv_dim), F32)] * 2,
        compiler_params=_params("arbitrary"),
        name="qkv_proj",
    )(x2, norm_g, mods.table, mods.table, w,
      q_g.reshape(1, HEAD_DIM), k_g.reshape(1, HEAD_DIM), *rope)


def _attn_kernel(q_ref, kc_ref, vc_ref, kl_ref, vl_ref, o_ref, k_ref, v_ref, *, group):
    l = kc_ref.shape[1]

    @pl.when(pl.program_id(2) == 0)
    def _():
        k_ref[:l] = kc_ref[0]
        k_ref[l:] = kl_ref[0]
        v_ref[:l, :HEAD_DIM] = vc_ref[0]
        v_ref[l:, :HEAD_DIM] = vl_ref[0]
        v_ref[:, HEAD_DIM:] = jnp.ones((v_ref.shape[0], HEAD_DIM), BF16)

    nt = (((1,), (1,)), ((), ()))
    k, v = k_ref[...], v_ref[...]
    tq = q_ref.shape[1]
    sub = min(tq, ATTN_SUB_ROWS)
    items = [(r0, sub, gi) for r0 in range(0, tq, sub) for gi in range(group)]
    halve = lambda it: [(it[0], it[1] // 2, it[2]), (it[0] + it[1] // 2, it[1] // 2, it[2])]
    if len(items) > 1:
        items = halve(items[0]) + items[1:-1] + halve(items[-1])
    for r0, nrows, gi in items:
        rows = slice(r0, r0 + nrows)
        cols = slice(gi * HEAD_DIM, (gi + 1) * HEAD_DIM)
        s = lax.dot_general(q_ref[0, rows, cols], k, nt, preferred_element_type=F32)
        p = jnp.exp2(s - jnp.max(s, axis=-1, keepdims=True)).astype(BF16)
        oe = jnp.dot(p, v, preferred_element_type=F32)
        o_ref[0, rows, cols] = (oe[:, :HEAD_DIM] / oe[:, HEAD_DIM:HEAD_DIM + 1]).astype(BF16)


def _attention(q3, kc3, vc3, kl3, vl3, tq=1024):
    b, s, qd = q3.shape
    tq = min(tq, s)
    l = kc3.shape[1]
    group = qd // (N_KV_HEADS * HEAD_DIM)
    gw = group * HEAD_DIM
    kv_spec = lambda n: pl.BlockSpec((1, n, HEAD_DIM), lambda bi, kh, i: (bi, 0, kh))
    return pl.pallas_call(
        functools.partial(_attn_kernel, group=group),
        grid=(b, N_KV_HEADS, s // tq),
        in_specs=[
            pl.BlockSpec((1, tq, gw), lambda bi, kh, i: (bi, i, kh)),
            kv_spec(l), kv_spec(l), kv_spec(s), kv_spec(s),
        ],
        out_specs=pl.BlockSpec((1, tq, gw), lambda bi, kh, i: (bi, i, kh)),
        out_shape=jax.ShapeDtypeStruct((b, s, qd), BF16),
        scratch_shapes=[pltpu.VMEM((l + s, HEAD_DIM), BF16), pltpu.VMEM((l + s, 2 * HEAD_DIM), BF16)],
        compiler_params=_params("parallel", "parallel", "arbitrary"),
        name="attention",
    )(q3, kc3, vc3, kl3, vl3)


def _oproj_kernel(x_ref, a_ref, w_ref, gt_ref, o_ref):
    y = jnp.dot(a_ref[...], w_ref[...], preferred_element_type=F32)
    o_ref[...] = x_ref[...] + gt_ref[0] * y


def _out_proj(x2, a2, w, mods, k_gate, tm=512):
    rows, d = x2.shape
    tm = mods.tile(tm, rows)
    return pl.pallas_call(
        _oproj_kernel,
        grid=(rows // tm,),
        in_specs=[
            pl.BlockSpec((tm, d), lambda i: (i, 0)),
            pl.BlockSpec((tm, a2.shape[1]), lambda i: (i, 0)),
            pl.BlockSpec(w.shape, lambda i: (0, 0)),
            mods.spec(k_gate, tm, d),
        ],
        out_specs=pl.BlockSpec((tm, d), lambda i: (i, 0)),
        out_shape=jax.ShapeDtypeStruct((rows, d), F32),
        compiler_params=_params("parallel"),
        name="out_proj",
    )(x2, a2, w, mods.table)


def kernel(x, c, ctx, c_ctx, w_ada, b_ada, norm_g, w_ffn_in, w_ffn_out, w_fourier_out, w_qkv,
           q_norm_g, k_norm_g, w_attn_out):
    b, n, d = x.shape
    l = ctx.shape[1]
    depth = w_ada.shape[0]
    assert depth == 2 and b <= 8, "layer 0 = Fourier mixer, layer 1 = attention mixer"
    q_dim = w_attn_out.shape[1]
    n_heads = q_dim // HEAD_DIM

    s16 = jnp.zeros((MOD_ROWS, d), F32).at[:b].set(jax.nn.silu(c)).at[8].set(jax.nn.silu(c_ctx))
    table = _ada(s16.astype(BF16), w_ada, b_ada).reshape(depth * MOD_ROWS * N_MOD, 1, d)

    f = w_ffn_out.shape[2]
    w_in_all = w_ffn_in.reshape(depth * 2, d, 2 * f)
    w_out_all = w_ffn_out.reshape(depth * 2, f, d).astype(BF16)
    w_in = w_in_all[0].astype(BF16)
    ng = norm_g.reshape(depth * 3, 1, d)
    w_four = w_fourier_out.astype(BF16)
    w_qkv_b = w_qkv.astype(BF16)
    w_o = w_attn_out.astype(BF16)

    gd = d // FOURIER_GROUPS
    cos_c, sin_c = _dft_tables(gd)
    chan_tab = jnp.concatenate([cos_c, -sin_c], axis=0)

    xl = x.reshape(b * n, d)
    xc = ctx.reshape(b * l, d)

    ml = _Mods(table, 0, n, ctx=False)
    mc = _Mods(table, 0, l, ctx=True)
    xc = _ffn(xc, ng, 0, mc, 0, w_in, w_out_all, 0)
    xl, w_in_next = _ffn(xl, ng, 0, ml, 0, w_in, w_out_all, 0, cast_next=(w_in_all, 1))
    xl = _fourier_layer(xl.reshape(b, n, d), ng, 1, ml, w_four[0], chan_tab).reshape(b * n, d)
    xc = _fourier_layer(xc.reshape(b, l, d), ng, 1, mc, w_four[0], chan_tab).reshape(b * l, d)
    w_in = w_in_next
    xc = _ffn(xc, ng, 2, mc, 6, w_in, w_out_all, 1)
    xl, w_in_next = _ffn(xl, ng, 2, ml, 6, w_in, w_out_all, 1, cast_next=(w_in_all, 2))

    ml = _Mods(table, 1, n, ctx=False)
    mc = _Mods(table, 1, l, ctx=True)
    w_in = w_in_next
    xc = _ffn(xc, ng, 3, mc, 0, w_in, w_out_all, 2)
    xl, w_in_next = _ffn(xl, ng, 3, ml, 0, w_in, w_out_all, 2, cast_next=(w_in_all, 3))
    rope = _rope_tables(n)
    q, kl, vl = _qkv_proj(xl, ng, 4, ml, 3, w_qkv_b[0], q_norm_g[0], k_norm_g[0], rope, n_heads, n)
    kc, vc = _qkv_proj(xc, ng, 4, mc, 3, w_qkv_b[0], q_norm_g[0], k_norm_g[0], None, 0, l)
    kv_dim = N_KV_HEADS * HEAD_DIM
    o = _attention(q.reshape(b, n, q_dim), kc.reshape(b, l, kv_dim), vc.reshape(b, l, kv_dim),
                   kl.reshape(b, n, kv_dim), vl.reshape(b, n, kv_dim))
    xl = _out_proj(xl, o.reshape(b * n, q_dim), w_o[0], ml, 5)
    xl = _ffn(xl, ng, 5, ml, 6, w_in_next, w_out_all, 3)
    return xl.reshape(b, n, d)
```

```python
import functools
import math

import jax
import jax.numpy as jnp
import numpy as np
from jax import lax
from jax.experimental import pallas as pl
from jax.experimental.pallas import tpu as pltpu

F32 = jnp.float32
BF16 = jnp.bfloat16

GRID_W = 64
FOURIER_GROUPS = 8
HEAD_DIM = 128
N_KV_HEADS = 4
ROPE_FREQS = HEAD_DIM // 4
ROPE_THETA = 10000.0
N_MOD = 9
EPS = 1e-6
MOD_ROWS = 16
ATTN_SUB_ROWS = 256
FFN_FIRST_SUB_ROWS = 256
BF16_SUBLANES = 16
FFT_CHUNK = 16
FFT_SLABS = 16

VMEM_LIMIT_BYTES = 64 * 1024 * 1024


def _const(table, dtype):
    return jnp.asarray(np.asarray(table, np.float32).astype(dtype))


def _params(*sem):
    return pltpu.CompilerParams(dimension_semantics=sem, vmem_limit_bytes=VMEM_LIMIT_BYTES)


def _modulate(x, g, shift, scale):
    ms = jnp.mean(x * x, axis=-1, keepdims=True)
    return (x * lax.rsqrt(ms + EPS)) * g * (1.0 + scale) + shift


def _ada_kernel(s_ref, w_ref, b_ref, o_ref):
    w = w_ref[0].astype(BF16)
    o_ref[0] = jnp.dot(s_ref[...], w, preferred_element_type=F32) + b_ref[0]


def _ada(s16, w_ada, b_ada, tn=1024):
    depth, d, n = w_ada.shape
    return pl.pallas_call(
        _ada_kernel,
        grid=(depth, n // tn),
        in_specs=[
            pl.BlockSpec((MOD_ROWS, d), lambda l, j: (0, 0)),
            pl.BlockSpec((1, d, tn), lambda l, j: (l, 0, j)),
            pl.BlockSpec((1, 1, tn), lambda l, j: (l, 0, j)),
        ],
        out_specs=pl.BlockSpec((1, MOD_ROWS, tn), lambda l, j: (l, 0, j)),
        out_shape=jax.ShapeDtypeStruct((depth, MOD_ROWS, n), F32),
        compiler_params=_params("parallel", "parallel"),
        name="ada",
    )(s16, w_ada, b_ada.reshape(depth, 1, n))


class _Mods:
    def __init__(self, table, layer, rows_per_mod, ctx):
        self.table, self.layer, self.rows_per_mod, self.ctx = table, layer, rows_per_mod, ctx

    def tile(self, tm, rows):
        tm = min(tm, rows if self.ctx else self.rows_per_mod)
        assert rows % tm == 0 and (self.ctx or self.rows_per_mod % tm == 0)
        return tm

    def spec(self, k, tm, d, tile_of_step=lambda i: i):
        layer, rpm, ctx = self.layer, self.rows_per_mod, self.ctx

        def index(i, *_):
            row = (MOD_ROWS - 8) if ctx else (tile_of_step(i) * tm) // rpm
            return ((layer * MOD_ROWS + row) * N_MOD + k, 0, 0)

        return pl.BlockSpec((1, 1, d), index)


def _ffn_kernel(x_hbm_ref, g_ref, sh_ref, sc_ref, gt_ref, wg_a_ref, wu_a_ref, wo_a_ref,
                wg_b_ref, wu_b_ref, wo_b_ref, *refs):
    if len(refs) == 4:
        o_ref, h_ref, x_ref, x_sem = refs
    else:
        next_in_ref, o_ref, next_in_bf_ref, h_ref, x_ref, x_sem = refs
        next_in_bf_ref[...] = next_in_ref[...].astype(BF16)
    i, k = pl.program_id(0), pl.program_id(1)
    tm = x_ref.shape[0]

    def x_copy(tile):
        return pltpu.make_async_copy(x_hbm_ref.at[pl.ds(tile * tm, tm), :], x_ref, x_sem)

    def swiglu(h, wg_ref, wu_ref, wo_ref):
        gate = jnp.dot(h, wg_ref[...], preferred_element_type=F32)
        up = jnp.dot(h, wu_ref[...], preferred_element_type=F32)
        act = (jax.nn.silu(gate) * up).astype(BF16)
        return jnp.dot(act, wo_ref[...], preferred_element_type=F32)

    half_gate = 0.5 * gt_ref[0]

    @pl.when((k == 0) & (i == 0))
    def _():
        x_copy(0).start()

    @pl.when(k == 0)
    def _():
        x_copy(i).wait()
        gain = g_ref[0] * (1.0 + sc_ref[0])
        shift = sh_ref[0]
        sub = min(FFN_FIRST_SUB_ROWS, tm)
        for r0 in range(0, tm, sub):
            rows = slice(r0, r0 + sub)
            x = x_ref[rows, :]
            ms = jnp.mean(x * x, axis=-1, keepdims=True)
            h = ((x * lax.rsqrt(ms + EPS)) * gain + shift).astype(BF16)
            h_ref[rows, :] = h
            o_ref[rows, :] = x + half_gate * swiglu(h, wg_a_ref, wu_a_ref, wo_a_ref)

    @pl.when((k == 1) & (i + 1 < pl.num_programs(0)))
    def _():
        x_copy(i + 1).start()

    @pl.when(k > 0)
    def _():
        h = h_ref[...]
        o_ref[...] += half_gate * (swiglu(h, wg_a_ref, wu_a_ref, wo_a_ref)
                                   + swiglu(h, wg_b_ref, wu_b_ref, wo_b_ref))


def _ffn(x2, norm_g, gi, mods, k0, w_in, w_out, wo, cast_next=None, tm=1024, tf=512):
    rows, d = x2.shape
    f = w_out.shape[1]
    tm, tf = mods.tile(tm, rows), min(tf, f)
    ni, nk = rows // tm, f // tf
    assert nk % 2 == 1 and nk >= 3, "step 0 takes one hidden chunk, every later step two"
    nsteps = (nk + 1) // 2
    first = lambda k: jnp.maximum(2 * k - 1, 0)
    second = lambda k: 2 * k
    in_specs = [
        pl.BlockSpec(memory_space=pl.ANY),
        pl.BlockSpec((1, 1, d), lambda i, k: (gi, 0, 0)),
        mods.spec(k0, tm, d), mods.spec(k0 + 1, tm, d), mods.spec(k0 + 2, tm, d),
    ]
    for chunk in (first, second):
        in_specs += [
            pl.BlockSpec((d, tf), lambda i, k, c=chunk: (0, c(k))),
            pl.BlockSpec((d, tf), lambda i, k, c=chunk: (0, nk + c(k))),
            pl.BlockSpec((None, tf, d), lambda i, k, c=chunk: (wo, c(k), 0)),
        ]
    out_specs = [pl.BlockSpec((tm, d), lambda i, k: (i, 0))]
    out_shape = [jax.ShapeDtypeStruct((rows, d), F32)]
    operands = [x2, norm_g, mods.table, mods.table, mods.table] + [w_in, w_in, w_out] * 2
    if cast_next is not None:
        w_in_all, slab = cast_next
        units = d // ni // BF16_SUBLANES
        assert units * ni * BF16_SUBLANES == d
        nsub = max(s for s in range(1, nsteps) if units % s == 0)
        rb = d // ni // nsub
        sub_of = lambda i, k: i * nsub + jnp.clip(k - 1, 0, nsub - 1)
        in_specs += [pl.BlockSpec((None, rb, 2 * f), lambda i, k: (slab, sub_of(i, k), 0))]
        out_specs += [pl.BlockSpec((rb, 2 * f), lambda i, k: (sub_of(i, k), 0))]
        out_shape += [jax.ShapeDtypeStruct((d, 2 * f), BF16)]
        operands += [w_in_all]
    res = pl.pallas_call(
        _ffn_kernel,
        grid=(ni, nsteps),
        in_specs=in_specs,
        out_specs=out_specs,
        out_shape=out_shape,
        scratch_shapes=[pltpu.VMEM((tm, d), BF16), pltpu.VMEM((tm, d), F32),
                        pltpu.SemaphoreType.DMA(())],
        compiler_params=_params("arbitrary", "arbitrary"),
        name="ffn",
    )(*operands)
    return res[0] if cast_next is None else res


def _mod_kernel(x_ref, g_ref, sh_ref, sc_ref, o_ref):
    o_ref[...] = _modulate(x_ref[...], g_ref[0], sh_ref[0], sc_ref[0]).astype(o_ref.dtype)


def _modulated(x2, norm_g, gi, mods, k0, tm=512):
    rows, d = x2.shape
    tm = mods.tile(tm, rows)
    return pl.pallas_call(
        _mod_kernel,
        grid=(rows // tm,),
        in_specs=[
            pl.BlockSpec((tm, d), lambda i: (i, 0)),
            pl.BlockSpec((1, 1, d), lambda i: (gi, 0, 0)),
            mods.spec(k0, tm, d), mods.spec(k0 + 1, tm, d),
        ],
        out_specs=pl.BlockSpec((tm, d), lambda i: (i, 0)),
        out_shape=jax.ShapeDtypeStruct((rows, d), BF16),
        compiler_params=_params("parallel"),
        name="modulate",
    )(x2, norm_g, mods.table, mods.table)


def _dft_tables(n):
    idx = np.arange(n, dtype=np.int64)
    ang = ((idx[:, None] * idx[None, :]) % n) * (2.0 * math.pi / n)
    s = 1.0 / math.sqrt(n)
    return _const(np.cos(ang) * s, BF16), _const(np.sin(ang) * s, BF16)


def _pos_dft_kernel(c_ref, s_ref, h_ref, zr_ref, zi_ref):
    h = h_ref[0]
    zr_ref[0] = jnp.dot(c_ref[...], h, preferred_element_type=F32).astype(BF16)
    zi_ref[0] = jnp.dot(s_ref[...], h, preferred_element_type=F32).astype(BF16)


def _pos_dft(h3, cos_n, sin_n, tk=256):
    b, n, d = h3.shape
    tk = min(tk, n)
    out = jax.ShapeDtypeStruct((b, n, d), BF16)
    return pl.pallas_call(
        _pos_dft_kernel,
        grid=(b, n // tk),
        in_specs=[
            pl.BlockSpec((tk, n), lambda bi, i: (i, 0)),
            pl.BlockSpec((tk, n), lambda bi, i: (i, 0)),
            pl.BlockSpec((1, n, d), lambda bi, i: (bi, 0, 0), pipeline_mode=pl.Buffered(1)),
        ],
        out_specs=[pl.BlockSpec((1, tk, d), lambda bi, i: (bi, i, 0))] * 2,
        out_shape=[out, out],
        compiler_params=_params("parallel", "parallel"),
        name="pos_dft",
    )(cos_n, sin_n, h3)


def _fft_tables(n):
    s1, r = FFT_SLABS, FFT_CHUNK
    s2 = n // s1
    assert s1 * s2 == n and s1 * r == s2 and s2 % r == 0
    ar = lambda m: np.arange(m, dtype=np.int64)
    j, p, q = np.ogrid[:s2 // r, :2 * s2, :s1 * r]
    part, k1, rr, n1, rr2 = p // s2, (p % s2) // r, p % r, q // r, q % r
    ang = ((k1 * (s2 * n1 + r * j + rr)) % n) * (2.0 * math.pi / n)
    t1 = np.where(rr == rr2, np.where(part == 0, np.cos(ang), -np.sin(ang)), 0.0)
    phi = ((ar(s2)[:, None] * ar(s2)[None, :]) % s2) * (2.0 * math.pi / s2)
    c, s = np.cos(phi) / math.sqrt(n), np.sin(phi) / math.sqrt(n)
    t2 = np.block([[c, s], [s, -c]])
    src = (ar(s1)[None, :] * r + ar(r)[:, None]).reshape(-1)
    perm = src[:, None] == ar(s1 * r)[None, :]
    return _const(t1, BF16), _const(t2, BF16), _const(perm, BF16)


def _fft_kernel(t1_ref, t2_ref, p_ref, h_ref, *refs):
    if len(refs) == 4:
        zr_ref, zi_ref, b_ref, z_ref = refs
    else:
        w_f32_ref, zr_ref, zi_ref, w_bf_ref, b_ref, z_ref = refs
        w_bf_ref[...] = w_f32_ref[...].astype(BF16)
    s1, r = FFT_SLABS, FFT_CHUNK
    s2 = h_ref.shape[1] // s1
    for j in range(s2 // r):
        g = jnp.concatenate([h_ref[0, pl.ds(s2 * n1 + r * j, r), :] for n1 in range(s1)], axis=0)
        res = jnp.dot(t1_ref[j], g, preferred_element_type=F32).astype(BF16)
        for part in range(2):
            for k1 in range(s1):
                row = part * s2 + k1 * r
                b_ref[pl.ds(k1 * 2 * s2 + part * s2 + r * j, r), :] = res[row:row + r]
    for k1 in range(s1):
        rows = pl.ds(k1 * 2 * s2, 2 * s2)
        z_ref[rows, :] = jnp.dot(t2_ref[...], b_ref[rows, :], preferred_element_type=F32).astype(BF16)
    for part, out_ref in ((0, zr_ref), (1, zi_ref)):
        for jp in range(s2 // r):
            g = jnp.concatenate([z_ref[pl.ds(k1 * 2 * s2 + part * s2 + r * jp, r), :]
                                 for k1 in range(s1)], axis=0)
            out_ref[0, pl.ds(s1 * r * jp, s1 * r), :] = jnp.dot(
                p_ref[...], g, preferred_element_type=F32).astype(BF16)


def _pos_dft_two_stage(h3, cast_rest=None, tc=256):
    b, n, d = h3.shape
    tc = min(tc, d)
    nc = d // tc
    t1, t2, perm = _fft_tables(n)
    const = lambda a: pl.BlockSpec(a.shape, lambda bi, ci: (0,) * a.ndim)
    io = pl.BlockSpec((1, n, tc), lambda bi, ci: (bi, 0, ci))
    out = jax.ShapeDtypeStruct((b, n, d), BF16)
    in_specs, out_specs, out_shape, operands = [const(t1), const(t2), const(perm), io], [io, io], \
        [out, out], [t1, t2, perm, h3]
    if cast_rest is not None:
        slabs, wr, wc = cast_rest.shape
        units = wr // BF16_SUBLANES
        per_slab = min(s for s in range(1, units + 1)
                       if units % s == 0 and (slabs - 1) * (units // s) <= b * nc)
        blocks, rb = units // per_slab, per_slab * BF16_SUBLANES
        nblk = (slabs - 1) * blocks
        blk = lambda bi, ci: jnp.minimum(bi * nc + ci, nblk - 1)
        in_specs.append(pl.BlockSpec((None, rb, wc), lambda bi, ci: (1 + blk(bi, ci) // blocks,
                                                                     blk(bi, ci) % blocks, 0)))
        out_specs.append(pl.BlockSpec((None, rb, wc), lambda bi, ci: (blk(bi, ci) // blocks,
                                                                      blk(bi, ci) % blocks, 0)))
        out_shape.append(jax.ShapeDtypeStruct((slabs - 1, wr, wc), BF16))
        operands.append(cast_rest)
    return pl.pallas_call(
        _fft_kernel,
        grid=(b, nc),
        in_specs=in_specs,
        out_specs=out_specs,
        out_shape=out_shape,
        scratch_shapes=[pltpu.VMEM((2 * n, tc), BF16), pltpu.VMEM((2 * n, tc), BF16)],
        compiler_params=_params("arbitrary", "arbitrary"),
        name="pos_fft",
    )(*operands)


def _chan_kernel(x_ref, zr_ref, zi_ref, tab_ref, w_ref, gt_ref, o_ref, f_ref):
    gd = tab_ref.shape[1]
    tab = tab_ref[...]
    for grp in range(FOURIER_GROUPS):
        cols = slice(grp * gd, (grp + 1) * gd)
        z = jnp.concatenate([zr_ref[:, cols], zi_ref[:, cols]], axis=-1)
        f_ref[:, cols] = jnp.dot(z, tab, preferred_element_type=F32).astype(BF16)
    y = jnp.dot(f_ref[...], w_ref[...], preferred_element_type=F32)
    o_ref[...] = x_ref[...] + gt_ref[0] * y


def _chan_mix(x2, zr2, zi2, tab, w_out, mods, k_gate, tm=512):
    rows, d = x2.shape
    tm = mods.tile(tm, rows)
    return pl.pallas_call(
        _chan_kernel,
        grid=(rows // tm,),
        in_specs=[
            pl.BlockSpec((tm, d), lambda i: (i, 0)),
            pl.BlockSpec((tm, d), lambda i: (i, 0)),
            pl.BlockSpec((tm, d), lambda i: (i, 0)),
            pl.BlockSpec(tab.shape, lambda i: (0, 0)),
            pl.BlockSpec((d, d), lambda i: (0, 0)),
            mods.spec(k_gate, tm, d),
        ],
        out_specs=pl.BlockSpec((tm, d), lambda i: (i, 0)),
        out_shape=jax.ShapeDtypeStruct((rows, d), F32),
        scratch_shapes=[pltpu.VMEM((tm, d), BF16)],
        compiler_params=_params("parallel"),
        name="chan_mix",
    )(x2, zr2, zi2, tab, w_out, mods.table)


def _fourier_layer(x3, norm_g, gi, mods, w_out, chan_tab, cast_rest=None):
    b, n, d = x3.shape
    x2 = x3.reshape(b * n, d)
    h = _modulated(x2, norm_g, gi, mods, 3).reshape(b, n, d)
    rest = None
    if n == FFT_SLABS * FFT_SLABS * FFT_CHUNK:
        zr, zi, *rest = _pos_dft_two_stage(h, cast_rest)
        rest = rest[0] if rest else None
    else:
        zr, zi = _pos_dft(h, *_dft_tables(n))
    if cast_rest is not None and rest is None:
        rest = cast_rest[1:].astype(BF16)
    out = _chan_mix(x2, zr.reshape(b * n, d), zi.reshape(b * n, d), chan_tab, w_out, mods, 5)
    out = out.reshape(b, n, d)
    return out if cast_rest is None else (out, rest)


def _rope_tables(n):
    pos = np.arange(n)
    row = (pos // GRID_W).astype(np.float32)
    col = (pos % GRID_W).astype(np.float32)
    inv_freq = np.float32(ROPE_THETA) ** (-np.arange(ROPE_FREQS, dtype=np.float32) / ROPE_FREQS)
    a_r = row[:, None] * inv_freq
    a_c = col[:, None] * inv_freq
    ang = np.concatenate([a_r, a_r, a_c, a_c], axis=-1).astype(np.float64)
    cos, sin = np.cos(ang), np.sin(ang)
    first_half = (np.arange(HEAD_DIM) % (2 * ROPE_FREQS)) < ROPE_FREQS
    sin_from_upper = np.where(first_half[None, :], -sin, 0.0)
    sin_from_lower = np.where(first_half[None, :], 0.0, sin)
    return _const(cos, F32), _const(sin_from_upper, F32), _const(sin_from_lower, F32)


def _head_norm(t, g):
    ms = jnp.mean(t * t, axis=-1, keepdims=True)
    return t * lax.rsqrt(ms + EPS) * g


def _rope(t, cos, s_up, s_lo):
    up = pltpu.roll(t, HEAD_DIM - ROPE_FREQS, 1)
    lo = pltpu.roll(t, ROPE_FREQS, 1)
    return t * cos + up * s_up + lo * s_lo


def _qkv_kernel(x_ref, g_ref, sh_ref, sc_ref, w_ref, qg_ref, kg_ref, *refs,
                n_q_heads, use_rope, q_scale):
    if use_rope:
        cos, s_up, s_lo = (r[...] for r in refs[:3])
        refs = refs[3:]
    q_ref = refs[0] if n_q_heads else None
    k_ref, v_ref, raw_even_ref, raw_odd_ref = refs[-4:]
    step = pl.program_id(0)
    qk_cols = raw_even_ref.shape[1]

    @pl.when(step == 0)
    def _():
        raw_odd_ref[...] = jnp.zeros_like(raw_odd_ref)

    def run(raw_new_ref, raw_old_ref):
        h = _modulate(x_ref[...], g_ref[0], sh_ref[0], sc_ref[0]).astype(BF16)
        y = jnp.dot(h, w_ref[...], preferred_element_type=F32)
        raw_new_ref[...] = y[:, :qk_cols]
        v_ref[...] = y[:, qk_cols:].astype(BF16)
        for hd in range(n_q_heads + N_KV_HEADS):
            is_q = hd < n_q_heads
            gain = (qg_ref[...] * q_scale) if is_q else kg_ref[...]
            t = _head_norm(raw_old_ref[:, hd * HEAD_DIM:(hd + 1) * HEAD_DIM], gain)
            if use_rope:
                t = _rope(t, cos, s_up, s_lo)
            if is_q:
                q_ref[:, hd * HEAD_DIM:(hd + 1) * HEAD_DIM] = t.astype(BF16)
            else:
                kh = hd - n_q_heads
                k_ref[:, kh * HEAD_DIM:(kh + 1) * HEAD_DIM] = t.astype(BF16)

    @pl.when(step % 2 == 0)
    def _():
        run(raw_even_ref, raw_odd_ref)

    @pl.when(step % 2 == 1)
    def _():
        run(raw_odd_ref, raw_even_ref)


def _qkv_proj(x2, norm_g, gi, mods, k0, w, q_g, k_g, rope, n_q_heads, seq, tm=512):
    rows, d = x2.shape
    tm = mods.tile(tm, rows)
    n_tiles = rows // tm
    kv_dim = N_KV_HEADS * HEAD_DIM
    q_dim = n_q_heads * HEAD_DIM
    width = q_dim + 2 * kv_dim
    col_block, rem = divmod(w.shape[1] - width, width)
    assert rem == 0
    use_rope = rope is not None
    rope = rope if use_rope else ()
    tiles_per_seq = seq // tm
    kern = functools.partial(_qkv_kernel, n_q_heads=n_q_heads, use_rope=use_rope,
                             q_scale=math.log2(math.e) / math.sqrt(HEAD_DIM))
    cur = lambda i: jnp.minimum(i, n_tiles - 1)
    prev = lambda i: jnp.maximum(i - 1, 0)
    head_vec = pl.BlockSpec((1, HEAD_DIM), lambda i: (0, 0))
    rope_spec = pl.BlockSpec((tm, HEAD_DIM), lambda i: (prev(i) % tiles_per_seq, 0))
    out_widths = ([q_dim] if n_q_heads else []) + [kv_dim, kv_dim]
    qk_specs = [pl.BlockSpec((tm, wd), lambda i: (prev(i), 0)) for wd in out_widths[:-1]]
    return pl.pallas_call(
        kern,
        grid=(n_tiles + 1,),
        in_specs=[
            pl.BlockSpec((tm, d), lambda i: (cur(i), 0)),
            pl.BlockSpec((1, 1, d), lambda i: (gi, 0, 0)),
            mods.spec(k0, tm, d, cur), mods.spec(k0 + 1, tm, d, cur),
            pl.BlockSpec((d, width), lambda i: (0, col_block)),
            head_vec, head_vec,
        ] + [rope_spec] * len(rope),
        out_specs=qk_specs + [pl.BlockSpec((tm, kv_dim), lambda i: (cur(i), 0))],
        out_shape=[jax.ShapeDtypeStruct((rows, wd), BF16) for wd in out_widths],
        scratch_shapes=[pltpu.VMEM((tm, q_dim + kv_dim), F32)] * 2,
        compiler_params=_params("arbitrary"),
        name="qkv_proj",
    )(x2, norm_g, mods.table, mods.table, w,
      q_g.reshape(1, HEAD_DIM), k_g.reshape(1, HEAD_DIM), *rope)


def _attn_kernel(q_ref, kc_ref, vc_ref, kl_ref, vl_ref, o_ref, k_ref, v_ref, *, group):
    l = kc_ref.shape[1]

    @pl.when(pl.program_id(2) == 0)
    def _():
        k_ref[:l] = kc_ref[0]
        k_ref[l:] = kl_ref[0]
        v_ref[:l, :HEAD_DIM] = vc_ref[0]
        v_ref[l:, :HEAD_DIM] = vl_ref[0]
        v_ref[:, HEAD_DIM:] = jnp.ones((v_ref.shape[0], HEAD_DIM), BF16)

    nt = (((1,), (1,)), ((), ()))
    k, v = k_ref[...], v_ref[...]
    tq = q_ref.shape[1]
    sub = min(tq, ATTN_SUB_ROWS)
    items = [(r0, sub, gi) for r0 in range(0, tq, sub) for gi in range(group)]
    halve = lambda it: [(it[0], it[1] // 2, it[2]), (it[0] + it[1] // 2, it[1] // 2, it[2])]
    if len(items) > 1:
        items = halve(items[0]) + items[1:-1] + halve(items[-1])
    for r0, nrows, gi in items:
        rows = slice(r0, r0 + nrows)
        cols = slice(gi * HEAD_DIM, (gi + 1) * HEAD_DIM)
        s = lax.dot_general(q_ref[0, rows, cols], k, nt, preferred_element_type=F32)
        p = jnp.exp2(s - jnp.max(s, axis=-1, keepdims=True)).astype(BF16)
        oe = jnp.dot(p, v, preferred_element_type=F32)
        o_ref[0, rows, cols] = (oe[:, :HEAD_DIM] / oe[:, HEAD_DIM:HEAD_DIM + 1]).astype(BF16)


def _attention(q3, kc3, vc3, kl3, vl3, tq=1024):
    b, s, qd = q3.shape
    tq = min(tq, s)
    l = kc3.shape[1]
    group = qd // (N_KV_HEADS * HEAD_DIM)
    gw = group * HEAD_DIM
    kv_spec = lambda n: pl.BlockSpec((1, n, HEAD_DIM), lambda bi, kh, i: (bi, 0, kh))
    return pl.pallas_call(
        functools.partial(_attn_kernel, group=group),
        grid=(b, N_KV_HEADS, s // tq),
        in_specs=[
            pl.BlockSpec((1, tq, gw), lambda bi, kh, i: (bi, i, kh)),
            kv_spec(l), kv_spec(l), kv_spec(s), kv_spec(s),
        ],
        out_specs=pl.BlockSpec((1, tq, gw), lambda bi, kh, i: (bi, i, kh)),
        out_shape=jax.ShapeDtypeStruct((b, s, qd), BF16),
        scratch_shapes=[pltpu.VMEM((l + s, HEAD_DIM), BF16), pltpu.VMEM((l + s, 2 * HEAD_DIM), BF16)],
        compiler_params=_params("parallel", "parallel", "arbitrary"),
        name="attention",
    )(q3, kc3, vc3, kl3, vl3)


def _oproj_kernel(x_ref, a_ref, w_ref, gt_ref, o_ref):
    y = jnp.dot(a_ref[...], w_ref[...], preferred_element_type=F32)
    o_ref[...] = x_ref[...] + gt_ref[0] * y


def _out_proj(x2, a2, w, mods, k_gate, tm=512):
    rows, d = x2.shape
    tm = mods.tile(tm, rows)
    return pl.pallas_call(
        _oproj_kernel,
        grid=(rows // tm,),
        in_specs=[
            pl.BlockSpec((tm, d), lambda i: (i, 0)),
            pl.BlockSpec((tm, a2.shape[1]), lambda i: (i, 0)),
            pl.BlockSpec(w.shape, lambda i: (0, 0)),
            mods.spec(k_gate, tm, d),
        ],
        out_specs=pl.BlockSpec((tm, d), lambda i: (i, 0)),
        out_shape=jax.ShapeDtypeStruct((rows, d), F32),
        compiler_params=_params("parallel"),
        name="out_proj",
    )(x2, a2, w, mods.table)


def kernel(x, c, ctx, c_ctx, w_ada, b_ada, norm_g, w_ffn_in, w_ffn_out, w_fourier_out, w_qkv,
           q_norm_g, k_norm_g, w_attn_out):
    b, n, d = x.shape
    l = ctx.shape[1]
    depth = w_ada.shape[0]
    assert depth == 2 and b <= 8, "layer 0 = Fourier mixer, layer 1 = attention mixer"
    q_dim = w_attn_out.shape[1]
    n_heads = q_dim // HEAD_DIM

    s16 = jnp.zeros((MOD_ROWS, d), F32).at[:b].set(jax.nn.silu(c)).at[8].set(jax.nn.silu(c_ctx))
    table = _ada(s16.astype(BF16), w_ada, b_ada).reshape(depth * MOD_ROWS * N_MOD, 1, d)

    f = w_ffn_out.shape[2]
    w_in_all = w_ffn_in.reshape(depth * 2, d, 2 * f)
    w_out_all = w_ffn_out.reshape(depth * 2, f, d)
    w_in, w_out_first = w_in_all[0].astype(BF16), w_out_all[:1].astype(BF16)
    ng = norm_g.reshape(depth * 3, 1, d)
    w_four = w_fourier_out.astype(BF16)
    w_qkv_b = w_qkv.astype(BF16)
    w_o = w_attn_out.astype(BF16)

    gd = d // FOURIER_GROUPS
    cos_c, sin_c = _dft_tables(gd)
    chan_tab = jnp.concatenate([cos_c, -sin_c], axis=0)

    xl = x.reshape(b * n, d)
    xc = ctx.reshape(b * l, d)

    ml = _Mods(table, 0, n, ctx=False)
    mc = _Mods(table, 0, l, ctx=True)
    xc = _ffn(xc, ng, 0, mc, 0, w_in, w_out_first, 0)
    xl, w_in_next = _ffn(xl, ng, 0, ml, 0, w_in, w_out_first, 0, cast_next=(w_in_all, 1))
    xl, w_out_rest = _fourier_layer(xl.reshape(b, n, d), ng, 1, ml, w_four[0], chan_tab,
                                    cast_rest=w_out_all)
    xl = xl.reshape(b * n, d)
    xc = _fourier_layer(xc.reshape(b, l, d), ng, 1, mc, w_four[0], chan_tab).reshape(b * l, d)
    w_in = w_in_next
    xc = _ffn(xc, ng, 2, mc, 6, w_in, w_out_rest, 0)
    xl, w_in_next = _ffn(xl, ng, 2, ml, 6, w_in, w_out_rest, 0, cast_next=(w_in_all, 2))

    ml = _Mods(table, 1, n, ctx=False)
    mc = _Mods(table, 1, l, ctx=True)
    w_in = w_in_next
    xc = _ffn(xc, ng, 3, mc, 0, w_in, w_out_rest, 1)
    xl, w_in_next = _ffn(xl, ng, 3, ml, 0, w_in, w_out_rest, 1, cast_next=(w_in_all, 3))
    rope = _rope_tables(n)
    q, kl, vl = _qkv_proj(xl, ng, 4, ml, 3, w_qkv_b[0], q_norm_g[0], k_norm_g[0], rope, n_heads, n)
    kc, vc = _qkv_proj(xc, ng, 4, mc, 3, w_qkv_b[0], q_norm_g[0], k_norm_g[0], None, 0, l)
    kv_dim = N_KV_HEADS * HEAD_DIM
    o = _attention(q.reshape(b, n, q_dim), kc.reshape(b, l, kv_dim), vc.reshape(b, l, kv_dim),
                   kl.reshape(b, n, kv_dim), vl.reshape(b, n, kv_dim))
    xl = _out_proj(xl, o.reshape(b * n, q_dim), w_o[0], ml, 5)
    xl = _ffn(xl, ng, 5, ml, 6, w_in_next, w_out_rest, 2)
    return xl.reshape(b, n, d)
```

```python
import functools
import math

import jax
import jax.numpy as jnp
import numpy as np
from jax import lax
from jax.experimental import pallas as pl
from jax.experimental.pallas import tpu as pltpu

F32 = jnp.float32
BF16 = jnp.bfloat16

GRID_W = 64
FOURIER_GROUPS = 8
HEAD_DIM = 128
N_KV_HEADS = 4
ROPE_FREQS = HEAD_DIM // 4
ROPE_THETA = 10000.0
N_MOD = 9
EPS = 1e-6
MOD_ROWS = 16
ATTN_SUB_ROWS = 512
FFN_FIRST_SUB_ROWS = 256
BF16_SUBLANES = 16
MXU_DEPTH = 256
FFT_CHUNK = BF16_SUBLANES
FFT_SLABS = MXU_DEPTH // FFT_CHUNK

VMEM_LIMIT_BYTES = 64 * 1024 * 1024


def _const(table, dtype):
    return jnp.asarray(np.asarray(table, np.float32).astype(dtype))


def _params(*sem):
    return pltpu.CompilerParams(dimension_semantics=sem, vmem_limit_bytes=VMEM_LIMIT_BYTES)


def _modulate(x, g, shift, scale):
    ms = jnp.mean(x * x, axis=-1, keepdims=True)
    return (x * lax.rsqrt(ms + EPS)) * g * (1.0 + scale) + shift


def _ada_kernel(s_ref, w_ref, b_ref, o_ref):
    w = w_ref[0].astype(BF16)
    o_ref[0] = jnp.dot(s_ref[...], w, preferred_element_type=F32) + b_ref[0]


def _ada(s16, w_ada, b_ada, tn=1024):
    depth, d, n = w_ada.shape
    return pl.pallas_call(
        _ada_kernel,
        grid=(depth, n // tn),
        in_specs=[
            pl.BlockSpec((MOD_ROWS, d), lambda l, j: (0, 0)),
            pl.BlockSpec((1, d, tn), lambda l, j: (l, 0, j)),
            pl.BlockSpec((1, 1, tn), lambda l, j: (l, 0, j)),
        ],
        out_specs=pl.BlockSpec((1, MOD_ROWS, tn), lambda l, j: (l, 0, j)),
        out_shape=jax.ShapeDtypeStruct((depth, MOD_ROWS, n), F32),
        compiler_params=_params("parallel", "parallel"),
        name="ada",
    )(s16, w_ada, b_ada.reshape(depth, 1, n))


class _Mods:
    def __init__(self, table, layer, rows_per_mod, ctx):
        self.table, self.layer, self.rows_per_mod, self.ctx = table, layer, rows_per_mod, ctx

    def tile(self, tm, rows):
        tm = min(tm, rows if self.ctx else self.rows_per_mod)
        assert rows % tm == 0 and (self.ctx or self.rows_per_mod % tm == 0)
        return tm

    def spec(self, k, tm, d, tile_of_step=lambda i: i):
        layer, rpm, ctx = self.layer, self.rows_per_mod, self.ctx

        def index(i, *_):
            row = (MOD_ROWS - 8) if ctx else (tile_of_step(i) * tm) // rpm
            return ((layer * MOD_ROWS + row) * N_MOD + k, 0, 0)

        return pl.BlockSpec((1, 1, d), index)


def _ffn_kernel(x_hbm_ref, g_ref, sh_ref, sc_ref, gt_ref, wg_a_ref, wu_a_ref, wo_a_ref,
                wg_b_ref, wu_b_ref, wo_b_ref, *refs):
    if len(refs) == 4:
        o_ref, h_ref, x_ref, x_sem = refs
    else:
        next_in_ref, o_ref, next_in_bf_ref, h_ref, x_ref, x_sem = refs
        next_in_bf_ref[...] = next_in_ref[...].astype(BF16)
    i, k = pl.program_id(0), pl.program_id(1)
    tm = x_ref.shape[0]

    def x_copy(tile):
        return pltpu.make_async_copy(x_hbm_ref.at[pl.ds(tile * tm, tm), :], x_ref, x_sem)

    def swiglu(h, wg_ref, wu_ref, wo_ref):
        gate = jnp.dot(h, wg_ref[...], preferred_element_type=F32)
        up = jnp.dot(h, wu_ref[...], preferred_element_type=F32)
        act = (jax.nn.silu(gate) * up).astype(BF16)
        return jnp.dot(act, wo_ref[...], preferred_element_type=F32)

    half_gate = 0.5 * gt_ref[0]

    @pl.when((k == 0) & (i == 0))
    def _():
        x_copy(0).start()

    @pl.when(k == 0)
    def _():
        x_copy(i).wait()
        gain = g_ref[0] * (1.0 + sc_ref[0])
        shift = sh_ref[0]
        sub = min(FFN_FIRST_SUB_ROWS, tm)
        for r0 in range(0, tm, sub):
            rows = slice(r0, r0 + sub)
            x = x_ref[rows, :]
            ms = jnp.mean(x * x, axis=-1, keepdims=True)
            h = ((x * lax.rsqrt(ms + EPS)) * gain + shift).astype(BF16)
            h_ref[rows, :] = h
            o_ref[rows, :] = x + half_gate * swiglu(h, wg_a_ref, wu_a_ref, wo_a_ref)

    @pl.when((k == 1) & (i + 1 < pl.num_programs(0)))
    def _():
        x_copy(i + 1).start()

    @pl.when(k > 0)
    def _():
        h = h_ref[...]
        o_ref[...] += half_gate * (swiglu(h, wg_a_ref, wu_a_ref, wo_a_ref)
                                   + swiglu(h, wg_b_ref, wu_b_ref, wo_b_ref))


def _ffn(x2, norm_g, gi, mods, k0, w_in, w_out, wo, cast_next=None, tm=1024, tf=512):
    rows, d = x2.shape
    f = w_out.shape[1]
    tm, tf = mods.tile(tm, rows), min(tf, f)
    ni, nk = rows // tm, f // tf
    assert nk % 2 == 1 and nk >= 3, "step 0 takes one hidden chunk, every later step two"
    nsteps = (nk + 1) // 2
    first = lambda k: jnp.maximum(2 * k - 1, 0)
    second = lambda k: 2 * k
    in_specs = [
        pl.BlockSpec(memory_space=pl.ANY),
        pl.BlockSpec((1, 1, d), lambda i, k: (gi, 0, 0)),
        mods.spec(k0, tm, d), mods.spec(k0 + 1, tm, d), mods.spec(k0 + 2, tm, d),
    ]
    for chunk in (first, second):
        in_specs += [
            pl.BlockSpec((d, tf), lambda i, k, c=chunk: (0, c(k))),
            pl.BlockSpec((d, tf), lambda i, k, c=chunk: (0, nk + c(k))),
            pl.BlockSpec((None, tf, d), lambda i, k, c=chunk: (wo, c(k), 0)),
        ]
    out_specs = [pl.BlockSpec((tm, d), lambda i, k: (i, 0))]
    out_shape = [jax.ShapeDtypeStruct((rows, d), F32)]
    operands = [x2, norm_g, mods.table, mods.table, mods.table] + [w_in, w_in, w_out] * 2
    if cast_next is not None:
        w_in_all, slab = cast_next
        units = d // ni // BF16_SUBLANES
        assert units * ni * BF16_SUBLANES == d
        nsub = max(s for s in range(1, nsteps) if units % s == 0)
        rb = d // ni // nsub
        sub_of = lambda i, k: i * nsub + jnp.clip(k - 1, 0, nsub - 1)
        in_specs += [pl.BlockSpec((None, rb, 2 * f), lambda i, k: (slab, sub_of(i, k), 0))]
        out_specs += [pl.BlockSpec((rb, 2 * f), lambda i, k: (sub_of(i, k), 0))]
        out_shape += [jax.ShapeDtypeStruct((d, 2 * f), BF16)]
        operands += [w_in_all]
    res = pl.pallas_call(
        _ffn_kernel,
        grid=(ni, nsteps),
        in_specs=in_specs,
        out_specs=out_specs,
        out_shape=out_shape,
        scratch_shapes=[pltpu.VMEM((tm, d), BF16), pltpu.VMEM((tm, d), F32),
                        pltpu.SemaphoreType.DMA(())],
        compiler_params=_params("arbitrary", "arbitrary"),
        name="ffn",
    )(*operands)
    return res[0] if cast_next is None else res


def _mod_kernel(x_ref, g_ref, sh_ref, sc_ref, o_ref):
    o_ref[...] = _modulate(x_ref[...], g_ref[0], sh_ref[0], sc_ref[0]).astype(o_ref.dtype)


def _modulated(x2, norm_g, gi, mods, k0, tm=512):
    rows, d = x2.shape
    tm = mods.tile(tm, rows)
    return pl.pallas_call(
        _mod_kernel,
        grid=(rows // tm,),
        in_specs=[
            pl.BlockSpec((tm, d), lambda i: (i, 0)),
            pl.BlockSpec((1, 1, d), lambda i: (gi, 0, 0)),
            mods.spec(k0, tm, d), mods.spec(k0 + 1, tm, d),
        ],
        out_specs=pl.BlockSpec((tm, d), lambda i: (i, 0)),
        out_shape=jax.ShapeDtypeStruct((rows, d), BF16),
        compiler_params=_params("parallel"),
        name="modulate",
    )(x2, norm_g, mods.table, mods.table)


def _dft_tables(n):
    idx = np.arange(n, dtype=np.int64)
    ang = ((idx[:, None] * idx[None, :]) % n) * (2.0 * math.pi / n)
    s = 1.0 / math.sqrt(n)
    return _const(np.cos(ang) * s, BF16), _const(np.sin(ang) * s, BF16)


def _pos_dft_kernel(c_ref, s_ref, h_ref, zr_ref, zi_ref):
    h = h_ref[0]
    zr_ref[0] = jnp.dot(c_ref[...], h, preferred_element_type=F32).astype(BF16)
    zi_ref[0] = jnp.dot(s_ref[...], h, preferred_element_type=F32).astype(BF16)


def _pos_dft(h3, cos_n, sin_n, tk=256):
    b, n, d = h3.shape
    tk = min(tk, n)
    out = jax.ShapeDtypeStruct((b, n, d), BF16)
    return pl.pallas_call(
        _pos_dft_kernel,
        grid=(b, n // tk),
        in_specs=[
            pl.BlockSpec((tk, n), lambda bi, i: (i, 0)),
            pl.BlockSpec((tk, n), lambda bi, i: (i, 0)),
            pl.BlockSpec((1, n, d), lambda bi, i: (bi, 0, 0), pipeline_mode=pl.Buffered(1)),
        ],
        out_specs=[pl.BlockSpec((1, tk, d), lambda bi, i: (bi, i, 0))] * 2,
        out_shape=[out, out],
        compiler_params=_params("parallel", "parallel"),
        name="pos_dft",
    )(cos_n, sin_n, h3)


def _fft_tables(n):
    s1, r = FFT_SLABS, FFT_CHUNK
    s2 = n // s1
    assert s1 * s2 == n and s1 * r == s2 and s2 % r == 0
    ar = lambda m: np.arange(m, dtype=np.int64)
    j, p, q = np.ogrid[:s2 // r, :2 * s2, :s1 * r]
    part, k1, rr, n1, rr2 = p // s2, (p % s2) // r, p % r, q // r, q % r
    ang = ((k1 * (s2 * n1 + r * j + rr)) % n) * (2.0 * math.pi / n)
    t1 = np.where(rr == rr2, np.where(part == 0, np.cos(ang), -np.sin(ang)), 0.0)
    phi = ((ar(s2)[:, None] * ar(s2)[None, :]) % s2) * (2.0 * math.pi / s2)
    c, s = np.cos(phi) / math.sqrt(n), np.sin(phi) / math.sqrt(n)
    t2 = np.block([[c, s], [s, -c]])
    src = (ar(s1)[None, :] * r + ar(r)[:, None]).reshape(-1)
    perm = src[:, None] == ar(s1 * r)[None, :]
    return _const(t1, BF16), _const(t2, BF16), _const(perm, BF16)


def _fft_kernel(t1_ref, t2_ref, p_ref, h_ref, *refs):
    if len(refs) == 4:
        zr_ref, zi_ref, b_ref, z_ref = refs
    else:
        w_f32_ref, zr_ref, zi_ref, w_bf_ref, b_ref, z_ref = refs
        w_bf_ref[...] = w_f32_ref[...].astype(BF16)
    s1, r = FFT_SLABS, FFT_CHUNK
    s2 = h_ref.shape[1] // s1
    for j in range(s2 // r):
        g = jnp.concatenate([h_ref[0, pl.ds(s2 * n1 + r * j, r), :] for n1 in range(s1)], axis=0)
        res = jnp.dot(t1_ref[j], g, preferred_element_type=F32).astype(BF16)
        for part in range(2):
            for k1 in range(s1):
                row = part * s2 + k1 * r
                b_ref[pl.ds(k1 * 2 * s2 + part * s2 + r * j, r), :] = res[row:row + r]
    for k1 in range(s1):
        rows = pl.ds(k1 * 2 * s2, 2 * s2)
        z_ref[rows, :] = jnp.dot(t2_ref[...], b_ref[rows, :], preferred_element_type=F32).astype(BF16)
    for part, out_ref in ((0, zr_ref), (1, zi_ref)):
        for jp in range(s2 // r):
            g = jnp.concatenate([z_ref[pl.ds(k1 * 2 * s2 + part * s2 + r * jp, r), :]
                                 for k1 in range(s1)], axis=0)
            out_ref[0, pl.ds(s1 * r * jp, s1 * r), :] = jnp.dot(
                p_ref[...], g, preferred_element_type=F32).astype(BF16)


def _pos_dft_two_stage(h3, cast_rest=None, tc=256):
    b, n, d = h3.shape
    tc = min(tc, d)
    nc = d // tc
    t1, t2, perm = _fft_tables(n)
    const = lambda a: pl.BlockSpec(a.shape, lambda bi, ci: (0,) * a.ndim)
    io = pl.BlockSpec((1, n, tc), lambda bi, ci: (bi, 0, ci))
    out = jax.ShapeDtypeStruct((b, n, d), BF16)
    in_specs, out_specs, out_shape, operands = [const(t1), const(t2), const(perm), io], [io, io], \
        [out, out], [t1, t2, perm, h3]
    if cast_rest is not None:
        slabs, wr, wc = cast_rest.shape
        units = wr // BF16_SUBLANES
        per_slab = min(s for s in range(1, units + 1)
                       if units % s == 0 and (slabs - 1) * (units // s) <= b * nc)
        blocks, rb = units // per_slab, per_slab * BF16_SUBLANES
        nblk = (slabs - 1) * blocks
        blk = lambda bi, ci: jnp.minimum(bi * nc + ci, nblk - 1)
        in_specs.append(pl.BlockSpec((None, rb, wc), lambda bi, ci: (1 + blk(bi, ci) // blocks,
                                                                     blk(bi, ci) % blocks, 0)))
        out_specs.append(pl.BlockSpec((None, rb, wc), lambda bi, ci: (blk(bi, ci) // blocks,
                                                                      blk(bi, ci) % blocks, 0)))
        out_shape.append(jax.ShapeDtypeStruct((slabs - 1, wr, wc), BF16))
        operands.append(cast_rest)
    return pl.pallas_call(
        _fft_kernel,
        grid=(b, nc),
        in_specs=in_specs,
        out_specs=out_specs,
        out_shape=out_shape,
        scratch_shapes=[pltpu.VMEM((2 * n, tc), BF16), pltpu.VMEM((2 * n, tc), BF16)],
        compiler_params=_params("arbitrary", "arbitrary"),
        name="pos_fft",
    )(*operands)


def _chan_kernel(x_ref, zr_ref, zi_ref, tab_ref, w_ref, gt_ref, o_ref, f_ref):
    gd = tab_ref.shape[1]
    tab = tab_ref[...]
    for grp in range(FOURIER_GROUPS):
        cols = slice(grp * gd, (grp + 1) * gd)
        z = jnp.concatenate([zr_ref[:, cols], zi_ref[:, cols]], axis=-1)
        f_ref[:, cols] = jnp.dot(z, tab, preferred_element_type=F32).astype(BF16)
    y = jnp.dot(f_ref[...], w_ref[...], preferred_element_type=F32)
    o_ref[...] = x_ref[...] + gt_ref[0] * y


def _chan_mix(x2, zr2, zi2, tab, w_out, mods, k_gate, tm=512):
    rows, d = x2.shape
    tm = mods.tile(tm, rows)
    return pl.pallas_call(
        _chan_kernel,
        grid=(rows // tm,),
        in_specs=[
            pl.BlockSpec((tm, d), lambda i: (i, 0)),
            pl.BlockSpec((tm, d), lambda i: (i, 0)),
            pl.BlockSpec((tm, d), lambda i: (i, 0)),
            pl.BlockSpec(tab.shape, lambda i: (0, 0)),
            pl.BlockSpec((d, d), lambda i: (0, 0)),
            mods.spec(k_gate, tm, d),
        ],
        out_specs=pl.BlockSpec((tm, d), lambda i: (i, 0)),
        out_shape=jax.ShapeDtypeStruct((rows, d), F32),
        scratch_shapes=[pltpu.VMEM((tm, d), BF16)],
        compiler_params=_params("parallel"),
        name="chan_mix",
    )(x2, zr2, zi2, tab, w_out, mods.table)


def _fourier_layer(x3, norm_g, gi, mods, w_out, chan_tab, cast_rest=None):
    b, n, d = x3.shape
    x2 = x3.reshape(b * n, d)
    h = _modulated(x2, norm_g, gi, mods, 3).reshape(b, n, d)
    rest = None
    if n == FFT_SLABS * FFT_SLABS * FFT_CHUNK:
        zr, zi, *rest = _pos_dft_two_stage(h, cast_rest)
        rest = rest[0] if rest else None
    else:
        zr, zi = _pos_dft(h, *_dft_tables(n))
    if cast_rest is not None and rest is None:
        rest = cast_rest[1:].astype(BF16)
    out = _chan_mix(x2, zr.reshape(b * n, d), zi.reshape(b * n, d), chan_tab, w_out, mods, 5)
    out = out.reshape(b, n, d)
    return out if cast_rest is None else (out, rest)


def _rope_tables(n):
    pos = np.arange(n)
    row = (pos // GRID_W).astype(np.float32)
    col = (pos % GRID_W).astype(np.float32)
    inv_freq = np.float32(ROPE_THETA) ** (-np.arange(ROPE_FREQS, dtype=np.float32) / ROPE_FREQS)
    a_r = row[:, None] * inv_freq
    a_c = col[:, None] * inv_freq
    ang = np.concatenate([a_r, a_r, a_c, a_c], axis=-1).astype(np.float64)
    cos, sin = np.cos(ang), np.sin(ang)
    first_half = (np.arange(HEAD_DIM) % (2 * ROPE_FREQS)) < ROPE_FREQS
    sin_from_upper = np.where(first_half[None, :], -sin, 0.0)
    sin_from_lower = np.where(first_half[None, :], 0.0, sin)
    return _const(cos, F32), _const(sin_from_upper, F32), _const(sin_from_lower, F32)


def _head_norm(t, g):
    ms = jnp.mean(t * t, axis=-1, keepdims=True)
    return t * lax.rsqrt(ms + EPS) * g


def _rope(t, cos, s_up, s_lo):
    up = pltpu.roll(t, HEAD_DIM - ROPE_FREQS, 1)
    lo = pltpu.roll(t, ROPE_FREQS, 1)
    return t * cos + up * s_up + lo * s_lo


def _qkv_kernel(x_ref, g_ref, sh_ref, sc_ref, w_ref, qg_ref, kg_ref, *refs,
                n_q_heads, use_rope, q_scale):
    if use_rope:
        cos, s_up, s_lo = (r[...] for r in refs[:3])
        refs = refs[3:]
    q_ref = refs[0] if n_q_heads else None
    k_ref, v_ref, raw_even_ref, raw_odd_ref = refs[-4:]
    step = pl.program_id(0)
    qk_cols = raw_even_ref.shape[1]

    @pl.when(step == 0)
    def _():
        raw_odd_ref[...] = jnp.zeros_like(raw_odd_ref)

    def run(raw_new_ref, raw_old_ref):
        h = _modulate(x_ref[...], g_ref[0], sh_ref[0], sc_ref[0]).astype(BF16)
        y = jnp.dot(h, w_ref[...], preferred_element_type=F32)
        raw_new_ref[...] = y[:, :qk_cols]
        v_ref[...] = y[:, qk_cols:].astype(BF16)
        for hd in range(n_q_heads + N_KV_HEADS):
            is_q = hd < n_q_heads
            gain = (qg_ref[...] * q_scale) if is_q else kg_ref[...]
            t = _head_norm(raw_old_ref[:, hd * HEAD_DIM:(hd + 1) * HEAD_DIM], gain)
            if use_rope:
                t = _rope(t, cos, s_up, s_lo)
            if is_q:
                q_ref[:, hd * HEAD_DIM:(hd + 1) * HEAD_DIM] = t.astype(BF16)
            else:
                kh = hd - n_q_heads
                k_ref[:, kh * HEAD_DIM:(kh + 1) * HEAD_DIM] = t.astype(BF16)

    @pl.when(step % 2 == 0)
    def _():
        run(raw_even_ref, raw_odd_ref)

    @pl.when(step % 2 == 1)
    def _():
        run(raw_odd_ref, raw_even_ref)


def _qkv_proj(x2, norm_g, gi, mods, k0, w, q_g, k_g, rope, n_q_heads, seq, tm=512):
    rows, d = x2.shape
    tm = mods.tile(tm, rows)
    n_tiles = rows // tm
    kv_dim = N_KV_HEADS * HEAD_DIM
    q_dim = n_q_heads * HEAD_DIM
    width = q_dim + 2 * kv_dim
    col_block, rem = divmod(w.shape[1] - width, width)
    assert rem == 0
    use_rope = rope is not None
    rope = rope if use_rope else ()
    tiles_per_seq = seq // tm
    kern = functools.partial(_qkv_kernel, n_q_heads=n_q_heads, use_rope=use_rope,
                             q_scale=math.log2(math.e) / math.sqrt(HEAD_DIM))
    cur = lambda i: jnp.minimum(i, n_tiles - 1)
    prev = lambda i: jnp.maximum(i - 1, 0)
    head_vec = pl.BlockSpec((1, HEAD_DIM), lambda i: (0, 0))
    rope_spec = pl.BlockSpec((tm, HEAD_DIM), lambda i: (prev(i) % tiles_per_seq, 0))
    out_widths = ([q_dim] if n_q_heads else []) + [kv_dim, kv_dim]
    qk_specs = [pl.BlockSpec((tm, wd), lambda i: (prev(i), 0)) for wd in out_widths[:-1]]
    return pl.pallas_call(
        kern,
        grid=(n_tiles + 1,),
        in_specs=[
            pl.BlockSpec((tm, d), lambda i: (cur(i), 0)),
            pl.BlockSpec((1, 1, d), lambda i: (gi, 0, 0)),
            mods.spec(k0, tm, d, cur), mods.spec(k0 + 1, tm, d, cur),
            pl.BlockSpec((d, width), lambda i: (0, col_block)),
            head_vec, head_vec,
        ] + [rope_spec] * len(rope),
        out_specs=qk_specs + [pl.BlockSpec((tm, kv_dim), lambda i: (cur(i), 0))],
        out_shape=[jax.ShapeDtypeStruct((rows, wd), BF16) for wd in out_widths],
        scratch_shapes=[pltpu.VMEM((tm, q_dim + kv_dim), F32)] * 2,
        compiler_params=_params("arbitrary"),
        name="qkv_proj",
    )(x2, norm_g, mods.table, mods.table, w,
      q_g.reshape(1, HEAD_DIM), k_g.reshape(1, HEAD_DIM), *rope)


def _attn_kernel(q_ref, kc_ref, vc_ref, kl_ref, vl_ref, o_ref, k_ref, v_ref, *, group):
    l = kc_ref.shape[1]

    @pl.when(pl.program_id(2) == 0)
    def _():
        k_ref[:l] = kc_ref[0]
        k_ref[l:] = kl_ref[0]
        v_ref[:l, :HEAD_DIM] = vc_ref[0]
        v_ref[l:, :HEAD_DIM] = vl_ref[0]
        v_ref[:, HEAD_DIM:] = jnp.ones((v_ref.shape[0], HEAD_DIM), BF16)

    nt = (((1,), (1,)), ((), ()))
    k, v = k_ref[...], v_ref[...]
    tq = q_ref.shape[1]
    sub = min(tq, ATTN_SUB_ROWS)
    items = [(r0, sub, gi) for r0 in range(0, tq, sub) for gi in range(group)]
    halve = lambda it: [(it[0], it[1] // 2, it[2]), (it[0] + it[1] // 2, it[1] // 2, it[2])]
    if len(items) > 1:
        items = halve(items[0]) + items[1:-1] + halve(items[-1])
    for r0, nrows, gi in items:
        rows = slice(r0, r0 + nrows)
        cols = slice(gi * HEAD_DIM, (gi + 1) * HEAD_DIM)
        s = lax.dot_general(q_ref[0, rows, cols], k, nt, preferred_element_type=F32)
        p = jnp.exp2(s - jnp.max(s, axis=-1, keepdims=True)).astype(BF16)
        oe = jnp.dot(p, v, preferred_element_type=F32)
        o_ref[0, rows, cols] = (oe[:, :HEAD_DIM] / oe[:, HEAD_DIM:HEAD_DIM + 1]).astype(BF16)


def _attention(q3, kc3, vc3, kl3, vl3, tq=1024):
    b, s, qd = q3.shape
    tq = min(tq, s)
    l = kc3.shape[1]
    group = qd // (N_KV_HEADS * HEAD_DIM)
    gw = group * HEAD_DIM
    kv_spec = lambda n: pl.BlockSpec((1, n, HEAD_DIM), lambda bi, kh, i: (bi, 0, kh))
    return pl.pallas_call(
        functools.partial(_attn_kernel, group=group),
        grid=(b, N_KV_HEADS, s // tq),
        in_specs=[
            pl.BlockSpec((1, tq, gw), lambda bi, kh, i: (bi, i, kh)),
            kv_spec(l), kv_spec(l), kv_spec(s), kv_spec(s),
        ],
        out_specs=pl.BlockSpec((1, tq, gw), lambda bi, kh, i: (bi, i, kh)),
        out_shape=jax.ShapeDtypeStruct((b, s, qd), BF16),
        scratch_shapes=[pltpu.VMEM((l + s, HEAD_DIM), BF16), pltpu.VMEM((l + s, 2 * HEAD_DIM), BF16)],
        compiler_params=_params("parallel", "parallel", "arbitrary"),
        name="attention",
    )(q3, kc3, vc3, kl3, vl3)


def _oproj_kernel(x_ref, a_ref, w_ref, gt_ref, o_ref):
    y = jnp.dot(a_ref[...], w_ref[...], preferred_element_type=F32)
    o_ref[...] = x_ref[...] + gt_ref[0] * y


def _out_proj(x2, a2, w, mods, k_gate, tm=1024):
    rows, d = x2.shape
    tm = mods.tile(tm, rows)
    return pl.pallas_call(
        _oproj_kernel,
        grid=(rows // tm,),
        in_specs=[
            pl.BlockSpec((tm, d), lambda i: (i, 0)),
            pl.BlockSpec((tm, a2.shape[1]), lambda i: (i, 0)),
            pl.BlockSpec(w.shape, lambda i: (0, 0), pipeline_mode=pl.Buffered(1)),
            mods.spec(k_gate, tm, d),
        ],
        out_specs=pl.BlockSpec((tm, d), lambda i: (i, 0)),
        out_shape=jax.ShapeDtypeStruct((rows, d), F32),
        compiler_params=_params("parallel"),
        name="out_proj",
    )(x2, a2, w, mods.table)


def kernel(x, c, ctx, c_ctx, w_ada, b_ada, norm_g, w_ffn_in, w_ffn_out, w_fourier_out, w_qkv,
           q_norm_g, k_norm_g, w_attn_out):
    b, n, d = x.shape
    l = ctx.shape[1]
    depth = w_ada.shape[0]
    assert depth == 2 and b <= 8, "layer 0 = Fourier mixer, layer 1 = attention mixer"
    q_dim = w_attn_out.shape[1]
    n_heads = q_dim // HEAD_DIM

    s16 = jnp.zeros((MOD_ROWS, d), F32).at[:b].set(jax.nn.silu(c)).at[8].set(jax.nn.silu(c_ctx))
    table = _ada(s16.astype(BF16), w_ada, b_ada).reshape(depth * MOD_ROWS * N_MOD, 1, d)

    f = w_ffn_out.shape[2]
    w_in_all = w_ffn_in.reshape(depth * 2, d, 2 * f)
    w_out_all = w_ffn_out.reshape(depth * 2, f, d)
    w_in, w_out_first = w_in_all[0].astype(BF16), w_out_all[:1].astype(BF16)
    ng = norm_g.reshape(depth * 3, 1, d)
    w_four = w_fourier_out.astype(BF16)
    w_qkv_b = w_qkv.astype(BF16)
    w_o = w_attn_out.astype(BF16)

    gd = d // FOURIER_GROUPS
    cos_c, sin_c = _dft_tables(gd)
    chan_tab = jnp.concatenate([cos_c, -sin_c], axis=0)

    xl = x.reshape(b * n, d)
    xc = ctx.reshape(b * l, d)

    ml = _Mods(table, 0, n, ctx=False)
    mc = _Mods(table, 0, l, ctx=True)
    xc = _ffn(xc, ng, 0, mc, 0, w_in, w_out_first, 0)
    xl, w_in_next = _ffn(xl, ng, 0, ml, 0, w_in, w_out_first, 0, cast_next=(w_in_all, 1))
    xl, w_out_rest = _fourier_layer(xl.reshape(b, n, d), ng, 1, ml, w_four[0], chan_tab,
                                    cast_rest=w_out_all)
    xl = xl.reshape(b * n, d)
    xc = _fourier_layer(xc.reshape(b, l, d), ng, 1, mc, w_four[0], chan_tab).reshape(b * l, d)
    w_in = w_in_next
    xc = _ffn(xc, ng, 2, mc, 6, w_in, w_out_rest, 0)
    xl, w_in_next = _ffn(xl, ng, 2, ml, 6, w_in, w_out_rest, 0, cast_next=(w_in_all, 2))

    ml = _Mods(table, 1, n, ctx=False)
    mc = _Mods(table, 1, l, ctx=True)
    w_in = w_in_next
    xc = _ffn(xc, ng, 3, mc, 0, w_in, w_out_rest, 1)
    xl, w_in_next = _ffn(xl, ng, 3, ml, 0, w_in, w_out_rest, 1, cast_next=(w_in_all, 3))
    rope = _rope_tables(n)
    q, kl, vl = _qkv_proj(xl, ng, 4, ml, 3, w_qkv_b[0], q_norm_g[0], k_norm_g[0], rope, n_heads, n)
    kc, vc = _qkv_proj(xc, ng, 4, mc, 3, w_qkv_b[0], q_norm_g[0], k_norm_g[0], None, 0, l)
    kv_dim = N_KV_HEADS * HEAD_DIM
    o = _attention(q.reshape(b, n, q_dim), kc.reshape(b, l, kv_dim), vc.reshape(b, l, kv_dim),
                   kl.reshape(b, n, kv_dim), vl.reshape(b, n, kv_dim))
    xl = _out_proj(xl, o.reshape(b * n, q_dim), w_o[0], ml, 5)
    xl = _ffn(xl, ng, 5, ml, 6, w_in_next, w_out_rest, 2)
    return xl.reshape(b, n, d)
```

```python
import functools
import math

import jax
import jax.numpy as jnp
import numpy as np
from jax import lax
from jax.experimental import pallas as pl
from jax.experimental.pallas import tpu as pltpu

F32 = jnp.float32
BF16 = jnp.bfloat16

GRID_W = 64
FOURIER_GROUPS = 8
HEAD_DIM = 128
N_KV_HEADS = 4
ROPE_FREQS = HEAD_DIM // 4
ROPE_THETA = 10000.0
N_MOD = 9
EPS = 1e-6
MOD_ROWS = 16
ATTN_SUB_ROWS = 256
FFN_FIRST_SUB_ROWS = 256
BF16_SUBLANES = 16
MOD_CHUNK_ROWS = 16
MXU_DEPTH = 256
FFT_CHUNK = BF16_SUBLANES
FFT_SLABS = MXU_DEPTH // FFT_CHUNK

VMEM_LIMIT_BYTES = 64 * 1024 * 1024


def _const(table, dtype):
    return jnp.asarray(np.asarray(table, np.float32).astype(dtype))


def _params(*sem):
    return pltpu.CompilerParams(dimension_semantics=sem, vmem_limit_bytes=VMEM_LIMIT_BYTES)


def _modulate(x, g, shift, scale):
    ms = jnp.mean(x * x, axis=-1, keepdims=True)
    return (x * lax.rsqrt(ms + EPS)) * g * (1.0 + scale) + shift


def _ada_kernel(s_ref, w_ref, b_ref, o_ref):
    w = w_ref[0].astype(BF16)
    o_ref[0] = jnp.dot(s_ref[...], w, preferred_element_type=F32) + b_ref[0]


def _ada(s16, w_ada, b_ada, tn=1024):
    depth, d, n = w_ada.shape
    return pl.pallas_call(
        _ada_kernel,
        grid=(depth, n // tn),
        in_specs=[
            pl.BlockSpec((MOD_ROWS, d), lambda l, j: (0, 0)),
            pl.BlockSpec((1, d, tn), lambda l, j: (l, 0, j)),
            pl.BlockSpec((1, 1, tn), lambda l, j: (l, 0, j)),
        ],
        out_specs=pl.BlockSpec((1, MOD_ROWS, tn), lambda l, j: (l, 0, j)),
        out_shape=jax.ShapeDtypeStruct((depth, MOD_ROWS, n), F32),
        compiler_params=_params("parallel", "parallel"),
        name="ada",
    )(s16, w_ada, b_ada.reshape(depth, 1, n))


class _Mods:
    def __init__(self, table, layer, rows_per_mod, ctx):
        self.table, self.layer, self.rows_per_mod, self.ctx = table, layer, rows_per_mod, ctx

    def tile(self, tm, rows):
        tm = min(tm, rows if self.ctx else self.rows_per_mod)
        assert rows % tm == 0 and (self.ctx or self.rows_per_mod % tm == 0)
        return tm

    def spec(self, k, tm, d, tile_of_step=lambda i: i):
        layer, rpm, ctx = self.layer, self.rows_per_mod, self.ctx

        def index(i, *_):
            row = (MOD_ROWS - 8) if ctx else (tile_of_step(i) * tm) // rpm
            return ((layer * MOD_ROWS + row) * N_MOD + k, 0, 0)

        return pl.BlockSpec((1, 1, d), index)


def _ffn_kernel(x_hbm_ref, g_ref, sh_ref, sc_ref, gt_ref, wg_a_ref, wu_a_ref, wo_a_ref,
                wg_b_ref, wu_b_ref, wo_b_ref, *refs):
    if len(refs) == 4:
        o_ref, h_ref, x_ref, x_sem = refs
    else:
        next_in_ref, o_ref, next_in_bf_ref, h_ref, x_ref, x_sem = refs
        next_in_bf_ref[...] = next_in_ref[...].astype(BF16)
    i, k = pl.program_id(0), pl.program_id(1)
    tm = x_ref.shape[0]

    def x_copy(tile):
        return pltpu.make_async_copy(x_hbm_ref.at[pl.ds(tile * tm, tm), :], x_ref, x_sem)

    def swiglu(h, wg_ref, wu_ref, wo_ref):
        gate = jnp.dot(h, wg_ref[...], preferred_element_type=F32)
        up = jnp.dot(h, wu_ref[...], preferred_element_type=F32)
        act = (jax.nn.silu(gate) * up).astype(BF16)
        return jnp.dot(act, wo_ref[...], preferred_element_type=F32)

    half_gate = 0.5 * gt_ref[0]

    @pl.when((k == 0) & (i == 0))
    def _():
        x_copy(0).start()

    @pl.when(k == 0)
    def _():
        x_copy(i).wait()
        gain = g_ref[0] * (1.0 + sc_ref[0])
        shift = sh_ref[0]
        sub = min(FFN_FIRST_SUB_ROWS, tm)
        for r0 in range(0, tm, sub):
            rows = slice(r0, r0 + sub)
            x = x_ref[rows, :]
            ms = jnp.mean(x * x, axis=-1, keepdims=True)
            h = ((x * lax.rsqrt(ms + EPS)) * gain + shift).astype(BF16)
            h_ref[rows, :] = h
            o_ref[rows, :] = x + half_gate * swiglu(h, wg_a_ref, wu_a_ref, wo_a_ref)

    @pl.when((k == 1) & (i + 1 < pl.num_programs(0)))
    def _():
        x_copy(i + 1).start()

    @pl.when(k > 0)
    def _():
        h = h_ref[...]
        o_ref[...] += half_gate * (swiglu(h, wg_a_ref, wu_a_ref, wo_a_ref)
                                   + swiglu(h, wg_b_ref, wu_b_ref, wo_b_ref))


def _ffn(x2, norm_g, gi, mods, k0, w_in, w_out, wo, cast_next=None, tm=1024, tf=512):
    rows, d = x2.shape
    f = w_out.shape[1]
    tm, tf = mods.tile(tm, rows), min(tf, f)
    ni, nk = rows // tm, f // tf
    assert nk % 2 == 1 and nk >= 3, "step 0 takes one hidden chunk, every later step two"
    nsteps = (nk + 1) // 2
    first = lambda k: jnp.maximum(2 * k - 1, 0)
    second = lambda k: 2 * k
    in_specs = [
        pl.BlockSpec(memory_space=pl.ANY),
        pl.BlockSpec((1, 1, d), lambda i, k: (gi, 0, 0)),
        mods.spec(k0, tm, d), mods.spec(k0 + 1, tm, d), mods.spec(k0 + 2, tm, d),
    ]
    for chunk in (first, second):
        in_specs += [
            pl.BlockSpec((d, tf), lambda i, k, c=chunk: (0, c(k))),
            pl.BlockSpec((d, tf), lambda i, k, c=chunk: (0, nk + c(k))),
            pl.BlockSpec((None, tf, d), lambda i, k, c=chunk: (wo, c(k), 0)),
        ]
    out_specs = [pl.BlockSpec((tm, d), lambda i, k: (i, 0))]
    out_shape = [jax.ShapeDtypeStruct((rows, d), F32)]
    operands = [x2, norm_g, mods.table, mods.table, mods.table] + [w_in, w_in, w_out] * 2
    if cast_next is not None:
        w_in_all, slab = cast_next
        units = d // ni // BF16_SUBLANES
        assert units * ni * BF16_SUBLANES == d
        nsub = max(s for s in range(1, nsteps) if units % s == 0)
        rb = d // ni // nsub
        sub_of = lambda i, k: i * nsub + jnp.clip(k - 1, 0, nsub - 1)
        in_specs += [pl.BlockSpec((None, rb, 2 * f), lambda i, k: (slab, sub_of(i, k), 0))]
        out_specs += [pl.BlockSpec((rb, 2 * f), lambda i, k: (sub_of(i, k), 0))]
        out_shape += [jax.ShapeDtypeStruct((d, 2 * f), BF16)]
        operands += [w_in_all]
    res = pl.pallas_call(
        _ffn_kernel,
        grid=(ni, nsteps),
        in_specs=in_specs,
        out_specs=out_specs,
        out_shape=out_shape,
        scratch_shapes=[pltpu.VMEM((tm, d), BF16), pltpu.VMEM((tm, d), F32),
                        pltpu.SemaphoreType.DMA(())],
        compiler_params=_params("arbitrary", "arbitrary"),
        name="ffn",
    )(*operands)
    return res[0] if cast_next is None else res


def _mod_kernel(x_ref, g_ref, sh_ref, sc_ref, o_ref):
    g, shift, scale = g_ref[0], sh_ref[0], sc_ref[0]
    chunk = min(MOD_CHUNK_ROWS, x_ref.shape[0])

    def rows(c, carry):
        r = pl.ds(pl.multiple_of(c * chunk, chunk), chunk)
        o_ref[r, :] = _modulate(x_ref[r, :], g, shift, scale).astype(o_ref.dtype)
        return carry

    lax.fori_loop(0, x_ref.shape[0] // chunk, rows, 0, unroll=8)


def _modulated(x2, norm_g, gi, mods, k0, tm=1024):
    rows, d = x2.shape
    tm = mods.tile(tm, rows)
    return pl.pallas_call(
        _mod_kernel,
        grid=(rows // tm,),
        in_specs=[
            pl.BlockSpec((tm, d), lambda i: (i, 0)),
            pl.BlockSpec((1, 1, d), lambda i: (gi, 0, 0)),
            mods.spec(k0, tm, d), mods.spec(k0 + 1, tm, d),
        ],
        out_specs=pl.BlockSpec((tm, d), lambda i: (i, 0)),
        out_shape=jax.ShapeDtypeStruct((rows, d), BF16),
        compiler_params=_params("parallel"),
        name="modulate",
    )(x2, norm_g, mods.table, mods.table)


def _dft_tables(n):
    idx = np.arange(n, dtype=np.int64)
    ang = ((idx[:, None] * idx[None, :]) % n) * (2.0 * math.pi / n)
    s = 1.0 / math.sqrt(n)
    return _const(np.cos(ang) * s, BF16), _const(np.sin(ang) * s, BF16)


def _pos_dft_kernel(c_ref, s_ref, h_ref, zr_ref, zi_ref):
    h = h_ref[0]
    zr_ref[0] = jnp.dot(c_ref[...], h, preferred_element_type=F32).astype(BF16)
    zi_ref[0] = jnp.dot(s_ref[...], h, preferred_element_type=F32).astype(BF16)


def _pos_dft(h3, cos_n, sin_n, tk=256):
    b, n, d = h3.shape
    tk = min(tk, n)
    out = jax.ShapeDtypeStruct((b, n, d), BF16)
    return pl.pallas_call(
        _pos_dft_kernel,
        grid=(b, n // tk),
        in_specs=[
            pl.BlockSpec((tk, n), lambda bi, i: (i, 0)),
            pl.BlockSpec((tk, n), lambda bi, i: (i, 0)),
            pl.BlockSpec((1, n, d), lambda bi, i: (bi, 0, 0), pipeline_mode=pl.Buffered(1)),
        ],
        out_specs=[pl.BlockSpec((1, tk, d), lambda bi, i: (bi, i, 0))] * 2,
        out_shape=[out, out],
        compiler_params=_params("parallel", "parallel"),
        name="pos_dft",
    )(cos_n, sin_n, h3)


def _fft_tables(n):
    s1, r = FFT_SLABS, FFT_CHUNK
    s2 = n // s1
    assert s1 * s2 == n and s1 * r == s2 and s2 % r == 0
    ar = lambda m: np.arange(m, dtype=np.int64)
    j, p, q = np.ogrid[:s2 // r, :2 * s2, :s1 * r]
    part, k1, rr, n1, rr2 = p // s2, (p % s2) // r, p % r, q // r, q % r
    ang = ((k1 * (s2 * n1 + r * j + rr)) % n) * (2.0 * math.pi / n)
    t1 = np.where(rr == rr2, np.where(part == 0, np.cos(ang), -np.sin(ang)), 0.0)
    phi = ((ar(s2)[:, None] * ar(s2)[None, :]) % s2) * (2.0 * math.pi / s2)
    c, s = np.cos(phi) / math.sqrt(n), np.sin(phi) / math.sqrt(n)
    t2 = np.block([[c, s], [s, -c]])
    src = (ar(s1)[None, :] * r + ar(r)[:, None]).reshape(-1)
    perm = src[:, None] == ar(s1 * r)[None, :]
    return _const(t1, BF16), _const(t2, BF16), _const(perm, BF16)


def _fft_kernel(t1_ref, t2_ref, p_ref, h_ref, *refs):
    if len(refs) == 4:
        zr_ref, zi_ref, b_ref, z_ref = refs
    else:
        w_f32_ref, zr_ref, zi_ref, w_bf_ref, b_ref, z_ref = refs
        w_bf_ref[...] = w_f32_ref[...].astype(BF16)
    s1, r = FFT_SLABS, FFT_CHUNK
    s2 = h_ref.shape[1] // s1
    for j in range(s2 // r):
        g = jnp.concatenate([h_ref[0, pl.ds(s2 * n1 + r * j, r), :] for n1 in range(s1)], axis=0)
        res = jnp.dot(t1_ref[j], g, preferred_element_type=F32).astype(BF16)
        for part in range(2):
            for k1 in range(s1):
                row = part * s2 + k1 * r
                b_ref[pl.ds(k1 * 2 * s2 + part * s2 + r * j, r), :] = res[row:row + r]
    for k1 in range(s1):
        rows = pl.ds(k1 * 2 * s2, 2 * s2)
        z_ref[rows, :] = jnp.dot(t2_ref[...], b_ref[rows, :], preferred_element_type=F32).astype(BF16)
    for part, out_ref in ((0, zr_ref), (1, zi_ref)):
        for jp in range(s2 // r):
            g = jnp.concatenate([z_ref[pl.ds(k1 * 2 * s2 + part * s2 + r * jp, r), :]
                                 for k1 in range(s1)], axis=0)
            out_ref[0, pl.ds(s1 * r * jp, s1 * r), :] = jnp.dot(
                p_ref[...], g, preferred_element_type=F32).astype(BF16)


def _pos_dft_two_stage(h3, cast_rest=None, tc=256):
    b, n, d = h3.shape
    tc = min(tc, d)
    nc = d // tc
    t1, t2, perm = _fft_tables(n)
    const = lambda a: pl.BlockSpec(a.shape, lambda bi, ci: (0,) * a.ndim)
    io = pl.BlockSpec((1, n, tc), lambda bi, ci: (bi, 0, ci))
    out = jax.ShapeDtypeStruct((b, n, d), BF16)
    in_specs, out_specs, out_shape, operands = [const(t1), const(t2), const(perm), io], [io, io], \
        [out, out], [t1, t2, perm, h3]
    if cast_rest is not None:
        slabs, wr, wc = cast_rest.shape
        units = wr // BF16_SUBLANES
        per_slab = min(s for s in range(1, units + 1)
                       if units % s == 0 and (slabs - 1) * (units // s) <= b * nc)
        blocks, rb = units // per_slab, per_slab * BF16_SUBLANES
        nblk = (slabs - 1) * blocks
        blk = lambda bi, ci: jnp.minimum(bi * nc + ci, nblk - 1)
        in_specs.append(pl.BlockSpec((None, rb, wc), lambda bi, ci: (1 + blk(bi, ci) // blocks,
                                                                     blk(bi, ci) % blocks, 0)))
        out_specs.append(pl.BlockSpec((None, rb, wc), lambda bi, ci: (blk(bi, ci) // blocks,
                                                                      blk(bi, ci) % blocks, 0)))
        out_shape.append(jax.ShapeDtypeStruct((slabs - 1, wr, wc), BF16))
        operands.append(cast_rest)
    return pl.pallas_call(
        _fft_kernel,
        grid=(b, nc),
        in_specs=in_specs,
        out_specs=out_specs,
        out_shape=out_shape,
        scratch_shapes=[pltpu.VMEM((2 * n, tc), BF16), pltpu.VMEM((2 * n, tc), BF16)],
        compiler_params=_params("arbitrary", "arbitrary"),
        name="pos_fft",
    )(*operands)


def _chan_kernel(x_ref, zr_ref, zi_ref, tab_ref, w_ref, gt_ref, o_ref, f_ref):
    gd = tab_ref.shape[1]
    tab = tab_ref[...]
    for grp in range(FOURIER_GROUPS):
        cols = slice(grp * gd, (grp + 1) * gd)
        z = jnp.concatenate([zr_ref[:, cols], zi_ref[:, cols]], axis=-1)
        f_ref[:, cols] = jnp.dot(z, tab, preferred_element_type=F32).astype(BF16)
    y = jnp.dot(f_ref[...], w_ref[...], preferred_element_type=F32)
    o_ref[...] = x_ref[...] + gt_ref[0] * y


def _chan_mix(x2, zr2, zi2, tab, w_out, mods, k_gate, tm=512):
    rows, d = x2.shape
    tm = mods.tile(tm, rows)
    return pl.pallas_call(
        _chan_kernel,
        grid=(rows // tm,),
        in_specs=[
            pl.BlockSpec((tm, d), lambda i: (i, 0)),
            pl.BlockSpec((tm, d), lambda i: (i, 0)),
            pl.BlockSpec((tm, d), lambda i: (i, 0)),
            pl.BlockSpec(tab.shape, lambda i: (0, 0)),
            pl.BlockSpec((d, d), lambda i: (0, 0)),
            mods.spec(k_gate, tm, d),
        ],
        out_specs=pl.BlockSpec((tm, d), lambda i: (i, 0)),
        out_shape=jax.ShapeDtypeStruct((rows, d), F32),
        scratch_shapes=[pltpu.VMEM((tm, d), BF16)],
        compiler_params=_params("parallel"),
        name="chan_mix",
    )(x2, zr2, zi2, tab, w_out, mods.table)


def _fourier_layer(x3, norm_g, gi, mods, w_out, chan_tab, cast_rest=None):
    b, n, d = x3.shape
    x2 = x3.reshape(b * n, d)
    h = _modulated(x2, norm_g, gi, mods, 3).reshape(b, n, d)
    rest = None
    if n == FFT_SLABS * FFT_SLABS * FFT_CHUNK:
        zr, zi, *rest = _pos_dft_two_stage(h, cast_rest)
        rest = rest[0] if rest else None
    else:
        zr, zi = _pos_dft(h, *_dft_tables(n))
    if cast_rest is not None and rest is None:
        rest = cast_rest[1:].astype(BF16)
    out = _chan_mix(x2, zr.reshape(b * n, d), zi.reshape(b * n, d), chan_tab, w_out, mods, 5)
    out = out.reshape(b, n, d)
    return out if cast_rest is None else (out, rest)


def _rope_tables(n):
    pos = np.arange(n)
    row = (pos // GRID_W).astype(np.float32)
    col = (pos % GRID_W).astype(np.float32)
    inv_freq = np.float32(ROPE_THETA) ** (-np.arange(ROPE_FREQS, dtype=np.float32) / ROPE_FREQS)
    a_r = row[:, None] * inv_freq
    a_c = col[:, None] * inv_freq
    ang = np.concatenate([a_r, a_r, a_c, a_c], axis=-1).astype(np.float64)
    cos, sin = np.cos(ang), np.sin(ang)
    first_half = (np.arange(HEAD_DIM) % (2 * ROPE_FREQS)) < ROPE_FREQS
    sin_from_upper = np.where(first_half[None, :], -sin, 0.0)
    sin_from_lower = np.where(first_half[None, :], 0.0, sin)
    return _const(cos, F32), _const(sin_from_upper, F32), _const(sin_from_lower, F32)


def _head_norm(t, g):
    ms = jnp.mean(t * t, axis=-1, keepdims=True)
    return t * lax.rsqrt(ms + EPS) * g


def _rope(t, cos, s_up, s_lo):
    up = pltpu.roll(t, HEAD_DIM - ROPE_FREQS, 1)
    lo = pltpu.roll(t, ROPE_FREQS, 1)
    return t * cos + up * s_up + lo * s_lo


def _qkv_kernel(x_ref, g_ref, sh_ref, sc_ref, w_ref, qg_ref, kg_ref, *refs,
                n_q_heads, use_rope, q_scale):
    if use_rope:
        cos, s_up, s_lo = (r[...] for r in refs[:3])
        refs = refs[3:]
    q_ref = refs[0] if n_q_heads else None
    k_ref, v_ref, raw_even_ref, raw_odd_ref = refs[-4:]
    step = pl.program_id(0)
    qk_cols = raw_even_ref.shape[1]

    @pl.when(step == 0)
    def _():
        raw_odd_ref[...] = jnp.zeros_like(raw_odd_ref)

    def run(raw_new_ref, raw_old_ref):
        h = _modulate(x_ref[...], g_ref[0], sh_ref[0], sc_ref[0]).astype(BF16)
        y = jnp.dot(h, w_ref[...], preferred_element_type=F32)
        raw_new_ref[...] = y[:, :qk_cols]
        v_ref[...] = y[:, qk_cols:].astype(BF16)
        for hd in range(n_q_heads + N_KV_HEADS):
            is_q = hd < n_q_heads
            gain = (qg_ref[...] * q_scale) if is_q else kg_ref[...]
            t = _head_norm(raw_old_ref[:, hd * HEAD_DIM:(hd + 1) * HEAD_DIM], gain)
            if use_rope:
                t = _rope(t, cos, s_up, s_lo)
            if is_q:
                q_ref[:, hd * HEAD_DIM:(hd + 1) * HEAD_DIM] = t.astype(BF16)
            else:
                kh = hd - n_q_heads
                k_ref[:, kh * HEAD_DIM:(kh + 1) * HEAD_DIM] = t.astype(BF16)

    @pl.when(step % 2 == 0)
    def _():
        run(raw_even_ref, raw_odd_ref)

    @pl.when(step % 2 == 1)
    def _():
        run(raw_odd_ref, raw_even_ref)


def _qkv_proj(x2, norm_g, gi, mods, k0, w, q_g, k_g, rope, n_q_heads, seq, tm=512):
    rows, d = x2.shape
    tm = mods.tile(tm, rows)
    n_tiles = rows // tm
    kv_dim = N_KV_HEADS * HEAD_DIM
    q_dim = n_q_heads * HEAD_DIM
    width = q_dim + 2 * kv_dim
    col_block, rem = divmod(w.shape[1] - width, width)
    assert rem == 0
    use_rope = rope is not None
    rope = rope if use_rope else ()
    tiles_per_seq = seq // tm
    kern = functools.partial(_qkv_kernel, n_q_heads=n_q_heads, use_rope=use_rope,
                             q_scale=math.log2(math.e) / math.sqrt(HEAD_DIM))
    cur = lambda i: jnp.minimum(i, n_tiles - 1)
    prev = lambda i: jnp.maximum(i - 1, 0)
    head_vec = pl.BlockSpec((1, HEAD_DIM), lambda i: (0, 0))
    rope_spec = pl.BlockSpec((tm, HEAD_DIM), lambda i: (prev(i) % tiles_per_seq, 0))
    out_widths = ([q_dim] if n_q_heads else []) + [kv_dim, kv_dim]
    qk_specs = [pl.BlockSpec((tm, wd), lambda i: (prev(i), 0)) for wd in out_widths[:-1]]
    return pl.pallas_call(
        kern,
        grid=(n_tiles + 1,),
        in_specs=[
            pl.BlockSpec((tm, d), lambda i: (cur(i), 0)),
            pl.BlockSpec((1, 1, d), lambda i: (gi, 0, 0)),
            mods.spec(k0, tm, d, cur), mods.spec(k0 + 1, tm, d, cur),
            pl.BlockSpec((d, width), lambda i: (0, col_block)),
            head_vec, head_vec,
        ] + [rope_spec] * len(rope),
        out_specs=qk_specs + [pl.BlockSpec((tm, kv_dim), lambda i: (cur(i), 0))],
        out_shape=[jax.ShapeDtypeStruct((rows, wd), BF16) for wd in out_widths],
        scratch_shapes=[pltpu.VMEM((tm, q_dim + kv_dim), F32)] * 2,
        compiler_params=_params("arbitrary"),
        name="qkv_proj",
    )(x2, norm_g, mods.table, mods.table, w,
      q_g.reshape(1, HEAD_DIM), k_g.reshape(1, HEAD_DIM), *rope)


def _attn_kernel(q_ref, kc_ref, vc_ref, kl_ref, vl_ref, o_ref, k_ref, v_ref, *, group):
    l = kc_ref.shape[1]

    @pl.when(pl.program_id(2) == 0)
    def _():
        k_ref[:l] = kc_ref[0]
        k_ref[l:] = kl_ref[0]
        v_ref[:l, :HEAD_DIM] = vc_ref[0]
        v_ref[l:, :HEAD_DIM] = vl_ref[0]
        v_ref[:, HEAD_DIM:] = jnp.ones((v_ref.shape[0], HEAD_DIM), BF16)

    nt = (((1,), (1,)), ((), ()))
    k, v = k_ref[...], v_ref[...]
    tq = q_ref.shape[1]
    sub = min(tq, ATTN_SUB_ROWS)
    items = [(r0, sub, gi) for r0 in range(0, tq, sub) for gi in range(group)]
    halve = lambda it: [(it[0], it[1] // 2, it[2]), (it[0] + it[1] // 2, it[1] // 2, it[2])]
    if len(items) > 1:
        items = halve(items[0]) + items[1:-1] + halve(items[-1])
    for r0, nrows, gi in items:
        rows = slice(r0, r0 + nrows)
        cols = slice(gi * HEAD_DIM, (gi + 1) * HEAD_DIM)
        s = lax.dot_general(q_ref[0, rows, cols], k, nt, preferred_element_type=F32)
        p = jnp.exp2(s - jnp.max(s, axis=-1, keepdims=True)).astype(BF16)
        oe = jnp.dot(p, v, preferred_element_type=F32)
        o_ref[0, rows, cols] = (oe[:, :HEAD_DIM] / oe[:, HEAD_DIM:HEAD_DIM + 1]).astype(BF16)


def _attention(q3, kc3, vc3, kl3, vl3, tq=1024):
    b, s, qd = q3.shape
    tq = min(tq, s)
    l = kc3.shape[1]
    group = qd // (N_KV_HEADS * HEAD_DIM)
    gw = group * HEAD_DIM
    kv_spec = lambda n: pl.BlockSpec((1, n, HEAD_DIM), lambda bi, kh, i: (bi, 0, kh))
    return pl.pallas_call(
        functools.partial(_attn_kernel, group=group),
        grid=(b, N_KV_HEADS, s // tq),
        in_specs=[
            pl.BlockSpec((1, tq, gw), lambda bi, kh, i: (bi, i, kh)),
            kv_spec(l), kv_spec(l), kv_spec(s), kv_spec(s),
        ],
        out_specs=pl.BlockSpec((1, tq, gw), lambda bi, kh, i: (bi, i, kh)),
        out_shape=jax.ShapeDtypeStruct((b, s, qd), BF16),
        scratch_shapes=[pltpu.VMEM((l + s, HEAD_DIM), BF16), pltpu.VMEM((l + s, 2 * HEAD_DIM), BF16)],
        compiler_params=_params("parallel", "parallel", "arbitrary"),
        name="attention",
    )(q3, kc3, vc3, kl3, vl3)


def _oproj_kernel(x_ref, a_ref, w_ref, gt_ref, o_ref):
    y = jnp.dot(a_ref[...], w_ref[...], preferred_element_type=F32)
    o_ref[...] = x_ref[...] + gt_ref[0] * y


def _out_proj(x2, a2, w, mods, k_gate, tm=1024):
    rows, d = x2.shape
    tm = mods.tile(tm, rows)
    return pl.pallas_call(
        _oproj_kernel,
        grid=(rows // tm,),
        in_specs=[
            pl.BlockSpec((tm, d), lambda i: (i, 0)),
            pl.BlockSpec((tm, a2.shape[1]), lambda i: (i, 0)),
            pl.BlockSpec(w.shape, lambda i: (0, 0), pipeline_mode=pl.Buffered(1)),
            mods.spec(k_gate, tm, d),
        ],
        out_specs=pl.BlockSpec((tm, d), lambda i: (i, 0)),
        out_shape=jax.ShapeDtypeStruct((rows, d), F32),
        compiler_params=_params("parallel"),
        name="out_proj",
    )(x2, a2, w, mods.table)


def kernel(x, c, ctx, c_ctx, w_ada, b_ada, norm_g, w_ffn_in, w_ffn_out, w_fourier_out, w_qkv,
           q_norm_g, k_norm_g, w_attn_out):
    b, n, d = x.shape
    l = ctx.shape[1]
    depth = w_ada.shape[0]
    assert depth == 2 and b <= 8, "layer 0 = Fourier mixer, layer 1 = attention mixer"
    q_dim = w_attn_out.shape[1]
    n_heads = q_dim // HEAD_DIM

    s16 = jnp.zeros((MOD_ROWS, d), F32).at[:b].set(jax.nn.silu(c)).at[8].set(jax.nn.silu(c_ctx))
    table = _ada(s16.astype(BF16), w_ada, b_ada).reshape(depth * MOD_ROWS * N_MOD, 1, d)

    f = w_ffn_out.shape[2]
    w_in_all = w_ffn_in.reshape(depth * 2, d, 2 * f)
    w_out_all = w_ffn_out.reshape(depth * 2, f, d)
    w_in, w_out_first = w_in_all[0].astype(BF16), w_out_all[:1].astype(BF16)
    ng = norm_g.reshape(depth * 3, 1, d)
    w_four = w_fourier_out.astype(BF16)
    w_qkv_b = w_qkv.astype(BF16)
    w_o = w_attn_out.astype(BF16)

    gd = d // FOURIER_GROUPS
    cos_c, sin_c = _dft_tables(gd)
    chan_tab = jnp.concatenate([cos_c, -sin_c], axis=0)

    xl = x.reshape(b * n, d)
    xc = ctx.reshape(b * l, d)

    ml = _Mods(table, 0, n, ctx=False)
    mc = _Mods(table, 0, l, ctx=True)
    xc = _ffn(xc, ng, 0, mc, 0, w_in, w_out_first, 0)
    xl, w_in_next = _ffn(xl, ng, 0, ml, 0, w_in, w_out_first, 0, cast_next=(w_in_all, 1))
    xl, w_out_rest = _fourier_layer(xl.reshape(b, n, d), ng, 1, ml, w_four[0], chan_tab,
                                    cast_rest=w_out_all)
    xl = xl.reshape(b * n, d)
    xc = _fourier_layer(xc.reshape(b, l, d), ng, 1, mc, w_four[0], chan_tab).reshape(b * l, d)
    w_in = w_in_next
    xc = _ffn(xc, ng, 2, mc, 6, w_in, w_out_rest, 0)
    xl, w_in_next = _ffn(xl, ng, 2, ml, 6, w_in, w_out_rest, 0, cast_next=(w_in_all, 2))

    ml = _Mods(table, 1, n, ctx=False)
    mc = _Mods(table, 1, l, ctx=True)
    w_in = w_in_next
    xc = _ffn(xc, ng, 3, mc, 0, w_in, w_out_rest, 1)
    xl, w_in_next = _ffn(xl, ng, 3, ml, 0, w_in, w_out_rest, 1, cast_next=(w_in_all, 3))
    rope = _rope_tables(n)
    q, kl, vl = _qkv_proj(xl, ng, 4, ml, 3, w_qkv_b[0], q_norm_g[0], k_norm_g[0], rope, n_heads, n)
    kc, vc = _qkv_proj(xc, ng, 4, mc, 3, w_qkv_b[0], q_norm_g[0], k_norm_g[0], None, 0, l)
    kv_dim = N_KV_HEADS * HEAD_DIM
    o = _attention(q.reshape(b, n, q_dim), kc.reshape(b, l, kv_dim), vc.reshape(b, l, kv_dim),
                   kl.reshape(b, n, kv_dim), vl.reshape(b, n, kv_dim))
    xl = _out_proj(xl, o.reshape(b * n, q_dim), w_o[0], ml, 5)
    xl = _ffn(xl, ng, 5, ml, 6, w_in_next, w_out_rest, 2)
    return xl.reshape(b, n, d)
```

```python
import functools
import math

import jax
import jax.numpy as jnp
import numpy as np
from jax import lax
from jax.experimental import pallas as pl
from jax.experimental.pallas import tpu as pltpu

F32 = jnp.float32
BF16 = jnp.bfloat16

GRID_W = 64
FOURIER_GROUPS = 8
HEAD_DIM = 128
N_KV_HEADS = 4
ROPE_FREQS = HEAD_DIM // 4
ROPE_THETA = 10000.0
N_MOD = 9
EPS = 1e-6
MOD_ROWS = 16
ATTN_SUB_ROWS = 256
FFN_FIRST_SUB_ROWS = 512
BF16_SUBLANES = 16
MOD_CHUNK_ROWS = 16
MXU_DEPTH = 256
FFT_CHUNK = BF16_SUBLANES
FFT_SLABS = MXU_DEPTH // FFT_CHUNK

VMEM_LIMIT_BYTES = 64 * 1024 * 1024


def _const(table, dtype):
    return jnp.asarray(np.asarray(table, np.float32).astype(dtype))


def _params(*sem):
    return pltpu.CompilerParams(dimension_semantics=sem, vmem_limit_bytes=VMEM_LIMIT_BYTES)


def _modulate(x, g, shift, scale):
    ms = jnp.mean(x * x, axis=-1, keepdims=True)
    return (x * lax.rsqrt(ms + EPS)) * g * (1.0 + scale) + shift


def _ada_kernel(s_ref, w_ref, b_ref, o_ref):
    w = w_ref[0].astype(BF16)
    o_ref[0] = jnp.dot(s_ref[...], w, preferred_element_type=F32) + b_ref[0]


def _ada(s16, w_ada, b_ada, tn=1024):
    depth, d, n = w_ada.shape
    return pl.pallas_call(
        _ada_kernel,
        grid=(depth, n // tn),
        in_specs=[
            pl.BlockSpec((MOD_ROWS, d), lambda l, j: (0, 0)),
            pl.BlockSpec((1, d, tn), lambda l, j: (l, 0, j)),
            pl.BlockSpec((1, 1, tn), lambda l, j: (l, 0, j)),
        ],
        out_specs=pl.BlockSpec((1, MOD_ROWS, tn), lambda l, j: (l, 0, j)),
        out_shape=jax.ShapeDtypeStruct((depth, MOD_ROWS, n), F32),
        compiler_params=_params("parallel", "parallel"),
        name="ada",
    )(s16, w_ada, b_ada.reshape(depth, 1, n))


class _Mods:
    def __init__(self, table, layer, rows_per_mod, ctx):
        self.table, self.layer, self.rows_per_mod, self.ctx = table, layer, rows_per_mod, ctx

    def tile(self, tm, rows):
        tm = min(tm, rows if self.ctx else self.rows_per_mod)
        assert rows % tm == 0 and (self.ctx or self.rows_per_mod % tm == 0)
        return tm

    def spec(self, k, tm, d, tile_of_step=lambda i: i):
        layer, rpm, ctx = self.layer, self.rows_per_mod, self.ctx

        def index(i, *_):
            row = (MOD_ROWS - 8) if ctx else (tile_of_step(i) * tm) // rpm
            return ((layer * MOD_ROWS + row) * N_MOD + k, 0, 0)

        return pl.BlockSpec((1, 1, d), index)


def _ffn_kernel(x_hbm_ref, g_ref, sh_ref, sc_ref, gt_ref, wg_a_ref, wu_a_ref, wo_a_ref,
                wg_b_ref, wu_b_ref, wo_b_ref, *refs):
    if len(refs) == 4:
        o_ref, h_ref, x_ref, x_sem = refs
    else:
        next_in_ref, o_ref, next_in_bf_ref, h_ref, x_ref, x_sem = refs
        next_in_bf_ref[...] = next_in_ref[...].astype(BF16)
    i, k = pl.program_id(0), pl.program_id(1)
    tm = x_ref.shape[0]

    def x_copy(tile):
        return pltpu.make_async_copy(x_hbm_ref.at[pl.ds(tile * tm, tm), :], x_ref, x_sem)

    def swiglu(h, wg_ref, wu_ref, wo_ref):
        gate = jnp.dot(h, wg_ref[...], preferred_element_type=F32)
        up = jnp.dot(h, wu_ref[...], preferred_element_type=F32)
        act = (jax.nn.silu(gate) * up).astype(BF16)
        return jnp.dot(act, wo_ref[...], preferred_element_type=F32)

    half_gate = 0.5 * gt_ref[0]

    @pl.when((k == 0) & (i == 0))
    def _():
        x_copy(0).start()

    @pl.when(k == 0)
    def _():
        x_copy(i).wait()
        gain = g_ref[0] * (1.0 + sc_ref[0])
        shift = sh_ref[0]
        sub = min(FFN_FIRST_SUB_ROWS, tm)
        for r0 in range(0, tm, sub):
            rows = slice(r0, r0 + sub)
            x = x_ref[rows, :]
            ms = jnp.mean(x * x, axis=-1, keepdims=True)
            h = ((x * lax.rsqrt(ms + EPS)) * gain + shift).astype(BF16)
            h_ref[rows, :] = h
            o_ref[rows, :] = x + half_gate * swiglu(h, wg_a_ref, wu_a_ref, wo_a_ref)

    @pl.when((k == 1) & (i + 1 < pl.num_programs(0)))
    def _():
        x_copy(i + 1).start()

    @pl.when(k > 0)
    def _():
        h = h_ref[...]
        o_ref[...] += half_gate * (swiglu(h, wg_a_ref, wu_a_ref, wo_a_ref)
                                   + swiglu(h, wg_b_ref, wu_b_ref, wo_b_ref))


def _ffn(x2, norm_g, gi, mods, k0, w_in, w_out, wo, cast_next=None, tm=1024, tf=512):
    rows, d = x2.shape
    f = w_out.shape[1]
    tm, tf = mods.tile(tm, rows), min(tf, f)
    ni, nk = rows // tm, f // tf
    assert nk % 2 == 1 and nk >= 3, "step 0 takes one hidden chunk, every later step two"
    nsteps = (nk + 1) // 2
    first = lambda k: jnp.maximum(2 * k - 1, 0)
    second = lambda k: 2 * k
    in_specs = [
        pl.BlockSpec(memory_space=pl.ANY),
        pl.BlockSpec((1, 1, d), lambda i, k: (gi, 0, 0)),
        mods.spec(k0, tm, d), mods.spec(k0 + 1, tm, d), mods.spec(k0 + 2, tm, d),
    ]
    for chunk in (first, second):
        in_specs += [
            pl.BlockSpec((d, tf), lambda i, k, c=chunk: (0, c(k))),
            pl.BlockSpec((d, tf), lambda i, k, c=chunk: (0, nk + c(k))),
            pl.BlockSpec((None, tf, d), lambda i, k, c=chunk: (wo, c(k), 0)),
        ]
    out_specs = [pl.BlockSpec((tm, d), lambda i, k: (i, 0))]
    out_shape = [jax.ShapeDtypeStruct((rows, d), F32)]
    operands = [x2, norm_g, mods.table, mods.table, mods.table] + [w_in, w_in, w_out] * 2
    if cast_next is not None:
        w_in_all, slab = cast_next
        units = d // ni // BF16_SUBLANES
        assert units * ni * BF16_SUBLANES == d
        nsub = max(s for s in range(1, nsteps) if units % s == 0)
        rb = d // ni // nsub
        sub_of = lambda i, k: i * nsub + jnp.clip(k - 1, 0, nsub - 1)
        in_specs += [pl.BlockSpec((None, rb, 2 * f), lambda i, k: (slab, sub_of(i, k), 0))]
        out_specs += [pl.BlockSpec((rb, 2 * f), lambda i, k: (sub_of(i, k), 0))]
        out_shape += [jax.ShapeDtypeStruct((d, 2 * f), BF16)]
        operands += [w_in_all]
    res = pl.pallas_call(
        _ffn_kernel,
        grid=(ni, nsteps),
        in_specs=in_specs,
        out_specs=out_specs,
        out_shape=out_shape,
        scratch_shapes=[pltpu.VMEM((tm, d), BF16), pltpu.VMEM((tm, d), F32),
                        pltpu.SemaphoreType.DMA(())],
        compiler_params=_params("arbitrary", "arbitrary"),
        name="ffn",
    )(*operands)
    return res[0] if cast_next is None else res


def _mod_kernel(x_ref, g_ref, sh_ref, sc_ref, o_ref):
    g, shift, scale = g_ref[0], sh_ref[0], sc_ref[0]
    chunk = min(MOD_CHUNK_ROWS, x_ref.shape[0])

    def rows(c, carry):
        r = pl.ds(pl.multiple_of(c * chunk, chunk), chunk)
        o_ref[r, :] = _modulate(x_ref[r, :], g, shift, scale).astype(o_ref.dtype)
        return carry

    lax.fori_loop(0, x_ref.shape[0] // chunk, rows, 0, unroll=8)


def _modulated(x2, norm_g, gi, mods, k0, tm=1024):
    rows, d = x2.shape
    tm = mods.tile(tm, rows)
    return pl.pallas_call(
        _mod_kernel,
        grid=(rows // tm,),
        in_specs=[
            pl.BlockSpec((tm, d), lambda i: (i, 0)),
            pl.BlockSpec((1, 1, d), lambda i: (gi, 0, 0)),
            mods.spec(k0, tm, d), mods.spec(k0 + 1, tm, d),
        ],
        out_specs=pl.BlockSpec((tm, d), lambda i: (i, 0)),
        out_shape=jax.ShapeDtypeStruct((rows, d), BF16),
        compiler_params=_params("parallel"),
        name="modulate",
    )(x2, norm_g, mods.table, mods.table)


def _dft_tables(n):
    idx = np.arange(n, dtype=np.int64)
    ang = ((idx[:, None] * idx[None, :]) % n) * (2.0 * math.pi / n)
    s = 1.0 / math.sqrt(n)
    return _const(np.cos(ang) * s, BF16), _const(np.sin(ang) * s, BF16)


def _pos_dft_kernel(c_ref, s_ref, h_ref, zr_ref, zi_ref):
    h = h_ref[0]
    zr_ref[0] = jnp.dot(c_ref[...], h, preferred_element_type=F32).astype(BF16)
    zi_ref[0] = jnp.dot(s_ref[...], h, preferred_element_type=F32).astype(BF16)


def _pos_dft(h3, cos_n, sin_n, tk=256):
    b, n, d = h3.shape
    tk = min(tk, n)
    out = jax.ShapeDtypeStruct((b, n, d), BF16)
    return pl.pallas_call(
        _pos_dft_kernel,
        grid=(b, n // tk),
        in_specs=[
            pl.BlockSpec((tk, n), lambda bi, i: (i, 0)),
            pl.BlockSpec((tk, n), lambda bi, i: (i, 0)),
            pl.BlockSpec((1, n, d), lambda bi, i: (bi, 0, 0), pipeline_mode=pl.Buffered(1)),
        ],
        out_specs=[pl.BlockSpec((1, tk, d), lambda bi, i: (bi, i, 0))] * 2,
        out_shape=[out, out],
        compiler_params=_params("parallel", "parallel"),
        name="pos_dft",
    )(cos_n, sin_n, h3)


def _fft_tables(n):
    s1, r = FFT_SLABS, FFT_CHUNK
    s2 = n // s1
    assert s1 * s2 == n and s1 * r == s2 and s2 % r == 0
    ar = lambda m: np.arange(m, dtype=np.int64)
    j, p, q = np.ogrid[:s2 // r, :2 * s2, :s1 * r]
    part, k1, rr, n1, rr2 = p // s2, (p % s2) // r, p % r, q // r, q % r
    ang = ((k1 * (s2 * n1 + r * j + rr)) % n) * (2.0 * math.pi / n)
    t1 = np.where(rr == rr2, np.where(part == 0, np.cos(ang), -np.sin(ang)), 0.0)
    phi = ((ar(s2)[:, None] * ar(s2)[None, :]) % s2) * (2.0 * math.pi / s2)
    c, s = np.cos(phi) / math.sqrt(n), np.sin(phi) / math.sqrt(n)
    t2 = np.block([[c, s], [s, -c]])
    src = (ar(s1)[None, :] * r + ar(r)[:, None]).reshape(-1)
    perm = src[:, None] == ar(s1 * r)[None, :]
    return _const(t1, BF16), _const(t2, BF16), _const(perm, BF16)


def _fft_kernel(t1_ref, t2_ref, p_ref, h_ref, *refs):
    if len(refs) == 4:
        zr_ref, zi_ref, b_ref, z_ref = refs
    else:
        w_f32_ref, zr_ref, zi_ref, w_bf_ref, b_ref, z_ref = refs
        w_bf_ref[...] = w_f32_ref[...].astype(BF16)
    s1, r = FFT_SLABS, FFT_CHUNK
    s2 = h_ref.shape[1] // s1
    for j in range(s2 // r):
        g = jnp.concatenate([h_ref[0, pl.ds(s2 * n1 + r * j, r), :] for n1 in range(s1)], axis=0)
        res = jnp.dot(t1_ref[j], g, preferred_element_type=F32).astype(BF16)
        for part in range(2):
            for k1 in range(s1):
                row = part * s2 + k1 * r
                b_ref[pl.ds(k1 * 2 * s2 + part * s2 + r * j, r), :] = res[row:row + r]
    for k1 in range(s1):
        rows = pl.ds(k1 * 2 * s2, 2 * s2)
        z_ref[rows, :] = jnp.dot(t2_ref[...], b_ref[rows, :], preferred_element_type=F32).astype(BF16)
    for part, out_ref in ((0, zr_ref), (1, zi_ref)):
        for jp in range(s2 // r):
            g = jnp.concatenate([z_ref[pl.ds(k1 * 2 * s2 + part * s2 + r * jp, r), :]
                                 for k1 in range(s1)], axis=0)
            out_ref[0, pl.ds(s1 * r * jp, s1 * r), :] = jnp.dot(
                p_ref[...], g, preferred_element_type=F32).astype(BF16)


def _pos_dft_two_stage(h3, cast_rest=None, tc=256):
    b, n, d = h3.shape
    tc = min(tc, d)
    nc = d // tc
    t1, t2, perm = _fft_tables(n)
    const = lambda a: pl.BlockSpec(a.shape, lambda bi, ci: (0,) * a.ndim)
    io = pl.BlockSpec((1, n, tc), lambda bi, ci: (bi, 0, ci))
    out = jax.ShapeDtypeStruct((b, n, d), BF16)
    in_specs, out_specs, out_shape, operands = [const(t1), const(t2), const(perm), io], [io, io], \
        [out, out], [t1, t2, perm, h3]
    if cast_rest is not None:
        slabs, wr, wc = cast_rest.shape
        units = wr // BF16_SUBLANES
        per_slab = min(s for s in range(1, units + 1)
                       if units % s == 0 and (slabs - 1) * (units // s) <= b * nc)
        blocks, rb = units // per_slab, per_slab * BF16_SUBLANES
        nblk = (slabs - 1) * blocks
        blk = lambda bi, ci: jnp.minimum(bi * nc + ci, nblk - 1)
        in_specs.append(pl.BlockSpec((None, rb, wc), lambda bi, ci: (1 + blk(bi, ci) // blocks,
                                                                     blk(bi, ci) % blocks, 0)))
        out_specs.append(pl.BlockSpec((None, rb, wc), lambda bi, ci: (blk(bi, ci) // blocks,
                                                                      blk(bi, ci) % blocks, 0)))
        out_shape.append(jax.ShapeDtypeStruct((slabs - 1, wr, wc), BF16))
        operands.append(cast_rest)
    return pl.pallas_call(
        _fft_kernel,
        grid=(b, nc),
        in_specs=in_specs,
        out_specs=out_specs,
        out_shape=out_shape,
        scratch_shapes=[pltpu.VMEM((2 * n, tc), BF16), pltpu.VMEM((2 * n, tc), BF16)],
        compiler_params=_params("arbitrary", "arbitrary"),
        name="pos_fft",
    )(*operands)


def _chan_kernel(x_ref, zr_ref, zi_ref, tab_ref, w_ref, gt_ref, o_ref, f_ref):
    gd = tab_ref.shape[1]
    tab = tab_ref[...]
    for grp in range(FOURIER_GROUPS):
        cols = slice(grp * gd, (grp + 1) * gd)
        z = jnp.concatenate([zr_ref[:, cols], zi_ref[:, cols]], axis=-1)
        f_ref[:, cols] = jnp.dot(z, tab, preferred_element_type=F32).astype(BF16)
    y = jnp.dot(f_ref[...], w_ref[...], preferred_element_type=F32)
    o_ref[...] = x_ref[...] + gt_ref[0] * y


def _chan_mix(x2, zr2, zi2, tab, w_out, mods, k_gate, tm=512):
    rows, d = x2.shape
    tm = mods.tile(tm, rows)
    return pl.pallas_call(
        _chan_kernel,
        grid=(rows // tm,),
        in_specs=[
            pl.BlockSpec((tm, d), lambda i: (i, 0)),
            pl.BlockSpec((tm, d), lambda i: (i, 0)),
            pl.BlockSpec((tm, d), lambda i: (i, 0)),
            pl.BlockSpec(tab.shape, lambda i: (0, 0)),
            pl.BlockSpec((d, d), lambda i: (0, 0)),
            mods.spec(k_gate, tm, d),
        ],
        out_specs=pl.BlockSpec((tm, d), lambda i: (i, 0)),
        out_shape=jax.ShapeDtypeStruct((rows, d), F32),
        scratch_shapes=[pltpu.VMEM((tm, d), BF16)],
        compiler_params=_params("parallel"),
        name="chan_mix",
    )(x2, zr2, zi2, tab, w_out, mods.table)


def _fourier_layer(x3, norm_g, gi, mods, w_out, chan_tab, cast_rest=None):
    b, n, d = x3.shape
    x2 = x3.reshape(b * n, d)
    h = _modulated(x2, norm_g, gi, mods, 3).reshape(b, n, d)
    rest = None
    if n == FFT_SLABS * FFT_SLABS * FFT_CHUNK:
        zr, zi, *rest = _pos_dft_two_stage(h, cast_rest)
        rest = rest[0] if rest else None
    else:
        zr, zi = _pos_dft(h, *_dft_tables(n))
    if cast_rest is not None and rest is None:
        rest = cast_rest[1:].astype(BF16)
    out = _chan_mix(x2, zr.reshape(b * n, d), zi.reshape(b * n, d), chan_tab, w_out, mods, 5)
    out = out.reshape(b, n, d)
    return out if cast_rest is None else (out, rest)


def _rope_tables(n):
    pos = np.arange(n)
    row = (pos // GRID_W).astype(np.float32)
    col = (pos % GRID_W).astype(np.float32)
    inv_freq = np.float32(ROPE_THETA) ** (-np.arange(ROPE_FREQS, dtype=np.float32) / ROPE_FREQS)
    a_r = row[:, None] * inv_freq
    a_c = col[:, None] * inv_freq
    ang = np.concatenate([a_r, a_r, a_c, a_c], axis=-1).astype(np.float64)
    cos, sin = np.cos(ang), np.sin(ang)
    first_half = (np.arange(HEAD_DIM) % (2 * ROPE_FREQS)) < ROPE_FREQS
    sin_from_upper = np.where(first_half[None, :], -sin, 0.0)
    sin_from_lower = np.where(first_half[None, :], 0.0, sin)
    return _const(cos, F32), _const(sin_from_upper, F32), _const(sin_from_lower, F32)


def _head_norm(t, g):
    ms = jnp.mean(t * t, axis=-1, keepdims=True)
    return t * lax.rsqrt(ms + EPS) * g


def _rope(t, cos, s_up, s_lo):
    up = pltpu.roll(t, HEAD_DIM - ROPE_FREQS, 1)
    lo = pltpu.roll(t, ROPE_FREQS, 1)
    return t * cos + up * s_up + lo * s_lo


def _qkv_kernel(x_ref, g_ref, sh_ref, sc_ref, w_ref, qg_ref, kg_ref, *refs,
                n_q_heads, use_rope, q_scale):
    if use_rope:
        cos, s_up, s_lo = (r[...] for r in refs[:3])
        refs = refs[3:]
    q_ref = refs[0] if n_q_heads else None
    k_ref, v_ref, raw_even_ref, raw_odd_ref = refs[-4:]
    step = pl.program_id(0)
    qk_cols = raw_even_ref.shape[1]

    @pl.when(step == 0)
    def _():
        raw_odd_ref[...] = jnp.zeros_like(raw_odd_ref)

    def run(raw_new_ref, raw_old_ref):
        h = _modulate(x_ref[...], g_ref[0], sh_ref[0], sc_ref[0]).astype(BF16)
        y = jnp.dot(h, w_ref[...], preferred_element_type=F32)
        raw_new_ref[...] = y[:, :qk_cols]
        v_ref[...] = y[:, qk_cols:].astype(BF16)
        for hd in range(n_q_heads + N_KV_HEADS):
            is_q = hd < n_q_heads
            gain = (qg_ref[...] * q_scale) if is_q else kg_ref[...]
            t = _head_norm(raw_old_ref[:, hd * HEAD_DIM:(hd + 1) * HEAD_DIM], gain)
            if use_rope:
                t = _rope(t, cos, s_up, s_lo)
            if is_q:
                q_ref[:, hd * HEAD_DIM:(hd + 1) * HEAD_DIM] = t.astype(BF16)
            else:
                kh = hd - n_q_heads
                k_ref[:, kh * HEAD_DIM:(kh + 1) * HEAD_DIM] = t.astype(BF16)

    @pl.when(step % 2 == 0)
    def _():
        run(raw_even_ref, raw_odd_ref)

    @pl.when(step % 2 == 1)
    def _():
        run(raw_odd_ref, raw_even_ref)


def _qkv_proj(x2, norm_g, gi, mods, k0, w, q_g, k_g, rope, n_q_heads, seq, tm=512):
    rows, d = x2.shape
    tm = mods.tile(tm, rows)
    n_tiles = rows // tm
    kv_dim = N_KV_HEADS * HEAD_DIM
    q_dim = n_q_heads * HEAD_DIM
    width = q_dim + 2 * kv_dim
    col_block, rem = divmod(w.shape[1] - width, width)
    assert rem == 0
    use_rope = rope is not None
    rope = rope if use_rope else ()
    tiles_per_seq = seq // tm
    kern = functools.partial(_qkv_kernel, n_q_heads=n_q_heads, use_rope=use_rope,
                             q_scale=math.log2(math.e) / math.sqrt(HEAD_DIM))
    cur = lambda i: jnp.minimum(i, n_tiles - 1)
    prev = lambda i: jnp.maximum(i - 1, 0)
    head_vec = pl.BlockSpec((1, HEAD_DIM), lambda i: (0, 0))
    rope_spec = pl.BlockSpec((tm, HEAD_DIM), lambda i: (prev(i) % tiles_per_seq, 0))
    out_widths = ([q_dim] if n_q_heads else []) + [kv_dim, kv_dim]
    qk_specs = [pl.BlockSpec((tm, wd), lambda i: (prev(i), 0)) for wd in out_widths[:-1]]
    return pl.pallas_call(
        kern,
        grid=(n_tiles + 1,),
        in_specs=[
            pl.BlockSpec((tm, d), lambda i: (cur(i), 0)),
            pl.BlockSpec((1, 1, d), lambda i: (gi, 0, 0)),
            mods.spec(k0, tm, d, cur), mods.spec(k0 + 1, tm, d, cur),
            pl.BlockSpec((d, width), lambda i: (0, col_block)),
            head_vec, head_vec,
        ] + [rope_spec] * len(rope),
        out_specs=qk_specs + [pl.BlockSpec((tm, kv_dim), lambda i: (cur(i), 0))],
        out_shape=[jax.ShapeDtypeStruct((rows, wd), BF16) for wd in out_widths],
        scratch_shapes=[pltpu.VMEM((tm, q_dim + kv_dim), F32)] * 2,
        compiler_params=_params("arbitrary"),
        name="qkv_proj",
    )(x2, norm_g, mods.table, mods.table, w,
      q_g.reshape(1, HEAD_DIM), k_g.reshape(1, HEAD_DIM), *rope)


def _attn_kernel(q_ref, kc_ref, vc_ref, kl_ref, vl_ref, o_ref, k_ref, v_ref, *, group):
    l = kc_ref.shape[1]

    @pl.when(pl.program_id(2) == 0)
    def _():
        k_ref[:l] = kc_ref[0]
        k_ref[l:] = kl_ref[0]
        v_ref[:l, :HEAD_DIM] = vc_ref[0]
        v_ref[l:, :HEAD_DIM] = vl_ref[0]
        v_ref[:, HEAD_DIM:] = jnp.ones((v_ref.shape[0], HEAD_DIM), BF16)

    nt = (((1,), (1,)), ((), ()))
    k, v = k_ref[...], v_ref[...]
    tq = q_ref.shape[1]
    sub = min(tq, ATTN_SUB_ROWS)
    items = [(r0, sub, gi) for r0 in range(0, tq, sub) for gi in range(group)]
    halve = lambda it: [(it[0], it[1] // 2, it[2]), (it[0] + it[1] // 2, it[1] // 2, it[2])]
    if len(items) > 1:
        items = halve(items[0]) + items[1:-1] + halve(items[-1])
    for r0, nrows, gi in items:
        rows = slice(r0, r0 + nrows)
        cols = slice(gi * HEAD_DIM, (gi + 1) * HEAD_DIM)
        s = lax.dot_general(q_ref[0, rows, cols], k, nt, preferred_element_type=F32)
        p = jnp.exp2(s - jnp.max(s, axis=-1, keepdims=True)).astype(BF16)
        oe = jnp.dot(p, v, preferred_element_type=F32)
        o_ref[0, rows, cols] = (oe[:, :HEAD_DIM] / oe[:, HEAD_DIM:HEAD_DIM + 1]).astype(BF16)


def _attention(q3, kc3, vc3, kl3, vl3, tq=1024):
    b, s, qd = q3.shape
    tq = min(tq, s)
    l = kc3.shape[1]
    group = qd // (N_KV_HEADS * HEAD_DIM)
    gw = group * HEAD_DIM
    kv_spec = lambda n: pl.BlockSpec((1, n, HEAD_DIM), lambda bi, kh, i: (bi, 0, kh))
    return pl.pallas_call(
        functools.partial(_attn_kernel, group=group),
        grid=(b, N_KV_HEADS, s // tq),
        in_specs=[
            pl.BlockSpec((1, tq, gw), lambda bi, kh, i: (bi, i, kh)),
            kv_spec(l), kv_spec(l), kv_spec(s), kv_spec(s),
        ],
        out_specs=pl.BlockSpec((1, tq, gw), lambda bi, kh, i: (bi, i, kh)),
        out_shape=jax.ShapeDtypeStruct((b, s, qd), BF16),
        scratch_shapes=[pltpu.VMEM((l + s, HEAD_DIM), BF16), pltpu.VMEM((l + s, 2 * HEAD_DIM), BF16)],
        compiler_params=_params("parallel", "parallel", "arbitrary"),
        name="attention",
    )(q3, kc3, vc3, kl3, vl3)


def _oproj_kernel(x_ref, a_ref, w_ref, gt_ref, o_ref):
    y = jnp.dot(a_ref[...], w_ref[...], preferred_element_type=F32)
    o_ref[...] = x_ref[...] + gt_ref[0] * y


def _out_proj(x2, a2, w, mods, k_gate, tm=1024):
    rows, d = x2.shape
    tm = mods.tile(tm, rows)
    return pl.pallas_call(
        _oproj_kernel,
        grid=(rows // tm,),
        in_specs=[
            pl.BlockSpec((tm, d), lambda i: (i, 0)),
            pl.BlockSpec((tm, a2.shape[1]), lambda i: (i, 0)),
            pl.BlockSpec(w.shape, lambda i: (0, 0), pipeline_mode=pl.Buffered(1)),
            mods.spec(k_gate, tm, d),
        ],
        out_specs=pl.BlockSpec((tm, d), lambda i: (i, 0)),
        out_shape=jax.ShapeDtypeStruct((rows, d), F32),
        compiler_params=_params("parallel"),
        name="out_proj",
    )(x2, a2, w, mods.table)


def kernel(x, c, ctx, c_ctx, w_ada, b_ada, norm_g, w_ffn_in, w_ffn_out, w_fourier_out, w_qkv,
           q_norm_g, k_norm_g, w_attn_out):
    b, n, d = x.shape
    l = ctx.shape[1]
    depth = w_ada.shape[0]
    assert depth == 2 and b <= 8, "layer 0 = Fourier mixer, layer 1 = attention mixer"
    q_dim = w_attn_out.shape[1]
    n_heads = q_dim // HEAD_DIM

    s16 = jnp.zeros((MOD_ROWS, d), F32).at[:b].set(jax.nn.silu(c)).at[8].set(jax.nn.silu(c_ctx))
    table = _ada(s16.astype(BF16), w_ada, b_ada).reshape(depth * MOD_ROWS * N_MOD, 1, d)

    f = w_ffn_out.shape[2]
    w_in_all = w_ffn_in.reshape(depth * 2, d, 2 * f)
    w_out_all = w_ffn_out.reshape(depth * 2, f, d)
    w_in, w_out_first = w_in_all[0].astype(BF16), w_out_all[:1].astype(BF16)
    ng = norm_g.reshape(depth * 3, 1, d)
    w_four = w_fourier_out.astype(BF16)
    w_qkv_b = w_qkv.astype(BF16)
    w_o = w_attn_out.astype(BF16)

    gd = d // FOURIER_GROUPS
    cos_c, sin_c = _dft_tables(gd)
    chan_tab = jnp.concatenate([cos_c, -sin_c], axis=0)

    xl = x.reshape(b * n, d)
    xc = ctx.reshape(b * l, d)

    ml = _Mods(table, 0, n, ctx=False)
    mc = _Mods(table, 0, l, ctx=True)
    xc = _ffn(xc, ng, 0, mc, 0, w_in, w_out_first, 0)
    xl, w_in_next = _ffn(xl, ng, 0, ml, 0, w_in, w_out_first, 0, cast_next=(w_in_all, 1))
    xl, w_out_rest = _fourier_layer(xl.reshape(b, n, d), ng, 1, ml, w_four[0], chan_tab,
                                    cast_rest=w_out_all)
    xl = xl.reshape(b * n, d)
    xc = _fourier_layer(xc.reshape(b, l, d), ng, 1, mc, w_four[0], chan_tab).reshape(b * l, d)
    w_in = w_in_next
    xc = _ffn(xc, ng, 2, mc, 6, w_in, w_out_rest, 0)
    xl, w_in_next = _ffn(xl, ng, 2, ml, 6, w_in, w_out_rest, 0, cast_next=(w_in_all, 2))

    ml = _Mods(table, 1, n, ctx=False)
    mc = _Mods(table, 1, l, ctx=True)
    w_in = w_in_next
    xc = _ffn(xc, ng, 3, mc, 0, w_in, w_out_rest, 1)
    xl, w_in_next = _ffn(xl, ng, 3, ml, 0, w_in, w_out_rest, 1, cast_next=(w_in_all, 3))
    rope = _rope_tables(n)
    q, kl, vl = _qkv_proj(xl, ng, 4, ml, 3, w_qkv_b[0], q_norm_g[0], k_norm_g[0], rope, n_heads, n)
    kc, vc = _qkv_proj(xc, ng, 4, mc, 3, w_qkv_b[0], q_norm_g[0], k_norm_g[0], None, 0, l)
    kv_dim = N_KV_HEADS * HEAD_DIM
    o = _attention(q.reshape(b, n, q_dim), kc.reshape(b, l, kv_dim), vc.reshape(b, l, kv_dim),
                   kl.reshape(b, n, kv_dim), vl.reshape(b, n, kv_dim))
    xl = _out_proj(xl, o.reshape(b * n, q_dim), w_o[0], ml, 5)
    xl = _ffn(xl, ng, 5, ml, 6, w_in_next, w_out_rest, 2)
    return xl.reshape(b, n, d)
```

```python
import functools
import math

import jax
import jax.numpy as jnp
import numpy as np
from jax import lax
from jax.experimental import pallas as pl
from jax.experimental.pallas import tpu as pltpu

F32 = jnp.float32
BF16 = jnp.bfloat16

GRID_W = 64
FOURIER_GROUPS = 8
HEAD_DIM = 128
N_KV_HEADS = 4
ROPE_FREQS = HEAD_DIM // 4
ROPE_THETA = 10000.0
N_MOD = 9
EPS = 1e-6
MOD_ROWS = 16
ATTN_SUB_ROWS = 256
FFN_FIRST_SUB_ROWS = 512
BF16_SUBLANES = 16
MOD_CHUNK_ROWS = 16
MXU_DEPTH = 256
FFT_CHUNK = BF16_SUBLANES
FFT_SLABS = MXU_DEPTH // FFT_CHUNK

VMEM_LIMIT_BYTES = 64 * 1024 * 1024


def _const(table, dtype):
    return jnp.asarray(np.asarray(table, np.float32).astype(dtype))


def _params(*sem):
    return pltpu.CompilerParams(dimension_semantics=sem, vmem_limit_bytes=VMEM_LIMIT_BYTES)


def _modulate(x, g, shift, scale):
    ms = jnp.mean(x * x, axis=-1, keepdims=True)
    return (x * lax.rsqrt(ms + EPS)) * g * (1.0 + scale) + shift


def _ada_kernel(s_ref, w_ref, b_ref, o_ref):
    w = w_ref[0].astype(BF16)
    o_ref[0] = jnp.dot(s_ref[...], w, preferred_element_type=F32) + b_ref[0]


def _ada(s16, w_ada, b_ada, tn=1024):
    depth, d, n = w_ada.shape
    return pl.pallas_call(
        _ada_kernel,
        grid=(depth, n // tn),
        in_specs=[
            pl.BlockSpec((MOD_ROWS, d), lambda l, j: (0, 0)),
            pl.BlockSpec((1, d, tn), lambda l, j: (l, 0, j)),
            pl.BlockSpec((1, 1, tn), lambda l, j: (l, 0, j)),
        ],
        out_specs=pl.BlockSpec((1, MOD_ROWS, tn), lambda l, j: (l, 0, j)),
        out_shape=jax.ShapeDtypeStruct((depth, MOD_ROWS, n), F32),
        compiler_params=_params("parallel", "parallel"),
        name="ada",
    )(s16, w_ada, b_ada.reshape(depth, 1, n))


class _Mods:
    def __init__(self, table, layer, rows_per_mod, ctx):
        self.table, self.layer, self.rows_per_mod, self.ctx = table, layer, rows_per_mod, ctx

    def tile(self, tm, rows):
        tm = min(tm, rows if self.ctx else self.rows_per_mod)
        assert rows % tm == 0 and (self.ctx or self.rows_per_mod % tm == 0)
        return tm

    def spec(self, k, tm, d, tile_of_step=lambda i: i):
        layer, rpm, ctx = self.layer, self.rows_per_mod, self.ctx

        def index(i, *_):
            row = (MOD_ROWS - 8) if ctx else (tile_of_step(i) * tm) // rpm
            return ((layer * MOD_ROWS + row) * N_MOD + k, 0, 0)

        return pl.BlockSpec((1, 1, d), index)


def _ffn_kernel(x_hbm_ref, g_ref, sh_ref, sc_ref, gt_ref, wg_a_ref, wu_a_ref, wo_a_ref,
                wg_b_ref, wu_b_ref, wo_b_ref, *refs):
    if len(refs) == 4:
        o_ref, h_ref, x_ref, x_sem = refs
    else:
        next_in_ref, o_ref, next_in_bf_ref, h_ref, x_ref, x_sem = refs
        next_in_bf_ref[...] = next_in_ref[...].astype(BF16)
    i, k = pl.program_id(0), pl.program_id(1)
    tm = x_ref.shape[0]

    def x_copy(tile):
        return pltpu.make_async_copy(x_hbm_ref.at[pl.ds(tile * tm, tm), :], x_ref, x_sem)

    def swiglu(h, wg, wu, wo):
        gate = jnp.dot(h, wg, preferred_element_type=F32)
        up = jnp.dot(h, wu, preferred_element_type=F32)
        act = (jax.nn.silu(gate) * up).astype(BF16)
        return jnp.dot(act, wo, preferred_element_type=F32)

    half_gate = 0.5 * gt_ref[0]

    @pl.when((k == 0) & (i == 0))
    def _():
        x_copy(0).start()

    @pl.when(k == 0)
    def _():
        x_copy(i).wait()
        gain = g_ref[0] * (1.0 + sc_ref[0])
        shift = sh_ref[0]
        sub = min(FFN_FIRST_SUB_ROWS, tm)
        for r0 in range(0, tm, sub):
            rows = slice(r0, r0 + sub)
            x = x_ref[rows, :]
            ms = jnp.mean(x * x, axis=-1, keepdims=True)
            h = ((x * lax.rsqrt(ms + EPS)) * gain + shift).astype(BF16)
            h_ref[rows, :] = h
            o_ref[rows, :] = x + half_gate * swiglu(h, wg_a_ref[...], wu_a_ref[...], wo_a_ref[...])

    @pl.when((k == 1) & (i + 1 < pl.num_programs(0)))
    def _():
        x_copy(i + 1).start()

    @pl.when(k > 0)
    def _():
        wg = jnp.concatenate([wg_a_ref[...], wg_b_ref[...]], axis=1)
        wu = jnp.concatenate([wu_a_ref[...], wu_b_ref[...]], axis=1)
        wo = jnp.concatenate([wo_a_ref[...], wo_b_ref[...]], axis=0)
        o_ref[...] += half_gate * swiglu(h_ref[...], wg, wu, wo)


def _ffn(x2, norm_g, gi, mods, k0, w_in, w_out, wo, cast_next=None, tm=1024, tf=512):
    rows, d = x2.shape
    f = w_out.shape[1]
    tm, tf = mods.tile(tm, rows), min(tf, f)
    ni, nk = rows // tm, f // tf
    assert nk % 2 == 1 and nk >= 3, "step 0 takes one hidden chunk, every later step two"
    nsteps = (nk + 1) // 2
    first = lambda k: jnp.maximum(2 * k - 1, 0)
    second = lambda k: 2 * k
    in_specs = [
        pl.BlockSpec(memory_space=pl.ANY),
        pl.BlockSpec((1, 1, d), lambda i, k: (gi, 0, 0)),
        mods.spec(k0, tm, d), mods.spec(k0 + 1, tm, d), mods.spec(k0 + 2, tm, d),
    ]
    for chunk in (first, second):
        in_specs += [
            pl.BlockSpec((d, tf), lambda i, k, c=chunk: (0, c(k))),
            pl.BlockSpec((d, tf), lambda i, k, c=chunk: (0, nk + c(k))),
            pl.BlockSpec((None, tf, d), lambda i, k, c=chunk: (wo, c(k), 0)),
        ]
    out_specs = [pl.BlockSpec((tm, d), lambda i, k: (i, 0))]
    out_shape = [jax.ShapeDtypeStruct((rows, d), F32)]
    operands = [x2, norm_g, mods.table, mods.table, mods.table] + [w_in, w_in, w_out] * 2
    if cast_next is not None:
        w_in_all, slab = cast_next
        units = d // ni // BF16_SUBLANES
        assert units * ni * BF16_SUBLANES == d
        nsub = max(s for s in range(1, nsteps) if units % s == 0)
        rb = d // ni // nsub
        sub_of = lambda i, k: i * nsub + jnp.clip(k - 1, 0, nsub - 1)
        in_specs += [pl.BlockSpec((None, rb, 2 * f), lambda i, k: (slab, sub_of(i, k), 0))]
        out_specs += [pl.BlockSpec((rb, 2 * f), lambda i, k: (sub_of(i, k), 0))]
        out_shape += [jax.ShapeDtypeStruct((d, 2 * f), BF16)]
        operands += [w_in_all]
    res = pl.pallas_call(
        _ffn_kernel,
        grid=(ni, nsteps),
        in_specs=in_specs,
        out_specs=out_specs,
        out_shape=out_shape,
        scratch_shapes=[pltpu.VMEM((tm, d), BF16), pltpu.VMEM((tm, d), F32),
                        pltpu.SemaphoreType.DMA(())],
        compiler_params=_params("arbitrary", "arbitrary"),
        name="ffn",
    )(*operands)
    return res[0] if cast_next is None else res


def _mod_kernel(x_ref, g_ref, sh_ref, sc_ref, o_ref):
    g, shift, scale = g_ref[0], sh_ref[0], sc_ref[0]
    chunk = min(MOD_CHUNK_ROWS, x_ref.shape[0])

    def rows(c, carry):
        r = pl.ds(pl.multiple_of(c * chunk, chunk), chunk)
        o_ref[r, :] = _modulate(x_ref[r, :], g, shift, scale).astype(o_ref.dtype)
        return carry

    lax.fori_loop(0, x_ref.shape[0] // chunk, rows, 0, unroll=8)


def _modulated(x2, norm_g, gi, mods, k0, tm=1024):
    rows, d = x2.shape
    tm = mods.tile(tm, rows)
    return pl.pallas_call(
        _mod_kernel,
        grid=(rows // tm,),
        in_specs=[
            pl.BlockSpec((tm, d), lambda i: (i, 0)),
            pl.BlockSpec((1, 1, d), lambda i: (gi, 0, 0)),
            mods.spec(k0, tm, d), mods.spec(k0 + 1, tm, d),
        ],
        out_specs=pl.BlockSpec((tm, d), lambda i: (i, 0)),
        out_shape=jax.ShapeDtypeStruct((rows, d), BF16),
        compiler_params=_params("parallel"),
        name="modulate",
    )(x2, norm_g, mods.table, mods.table)


def _dft_tables(n):
    idx = np.arange(n, dtype=np.int64)
    ang = ((idx[:, None] * idx[None, :]) % n) * (2.0 * math.pi / n)
    s = 1.0 / math.sqrt(n)
    return _const(np.cos(ang) * s, BF16), _const(np.sin(ang) * s, BF16)


def _pos_dft_kernel(c_ref, s_ref, h_ref, zr_ref, zi_ref):
    h = h_ref[0]
    zr_ref[0] = jnp.dot(c_ref[...], h, preferred_element_type=F32).astype(BF16)
    zi_ref[0] = jnp.dot(s_ref[...], h, preferred_element_type=F32).astype(BF16)


def _pos_dft(h3, cos_n, sin_n, tk=256):
    b, n, d = h3.shape
    tk = min(tk, n)
    out = jax.ShapeDtypeStruct((b, n, d), BF16)
    return pl.pallas_call(
        _pos_dft_kernel,
        grid=(b, n // tk),
        in_specs=[
            pl.BlockSpec((tk, n), lambda bi, i: (i, 0)),
            pl.BlockSpec((tk, n), lambda bi, i: (i, 0)),
            pl.BlockSpec((1, n, d), lambda bi, i: (bi, 0, 0), pipeline_mode=pl.Buffered(1)),
        ],
        out_specs=[pl.BlockSpec((1, tk, d), lambda bi, i: (bi, i, 0))] * 2,
        out_shape=[out, out],
        compiler_params=_params("parallel", "parallel"),
        name="pos_dft",
    )(cos_n, sin_n, h3)


def _fft_tables(n):
    s1, r = FFT_SLABS, FFT_CHUNK
    s2 = n // s1
    assert s1 * s2 == n and s1 * r == s2 and s2 % r == 0
    ar = lambda m: np.arange(m, dtype=np.int64)
    j, p, q = np.ogrid[:s2 // r, :2 * s2, :s1 * r]
    part, k1, rr, n1, rr2 = p // s2, (p % s2) // r, p % r, q // r, q % r
    ang = ((k1 * (s2 * n1 + r * j + rr)) % n) * (2.0 * math.pi / n)
    t1 = np.where(rr == rr2, np.where(part == 0, np.cos(ang), -np.sin(ang)), 0.0)
    phi = ((ar(s2)[:, None] * ar(s2)[None, :]) % s2) * (2.0 * math.pi / s2)
    c, s = np.cos(phi) / math.sqrt(n), np.sin(phi) / math.sqrt(n)
    t2 = np.block([[c, s], [s, -c]])
    src = (ar(s1)[None, :] * r + ar(r)[:, None]).reshape(-1)
    perm = src[:, None] == ar(s1 * r)[None, :]
    return _const(t1, BF16), _const(t2, BF16), _const(perm, BF16)


def _fft_kernel(t1_ref, t2_ref, p_ref, h_ref, *refs):
    if len(refs) == 4:
        zr_ref, zi_ref, b_ref, z_ref = refs
    else:
        w_f32_ref, zr_ref, zi_ref, w_bf_ref, b_ref, z_ref = refs
        w_bf_ref[...] = w_f32_ref[...].astype(BF16)
    s1, r = FFT_SLABS, FFT_CHUNK
    s2 = h_ref.shape[1] // s1
    for j in range(s2 // r):
        g = jnp.concatenate([h_ref[0, pl.ds(s2 * n1 + r * j, r), :] for n1 in range(s1)], axis=0)
        res = jnp.dot(t1_ref[j], g, preferred_element_type=F32).astype(BF16)
        for part in range(2):
            for k1 in range(s1):
                row = part * s2 + k1 * r
                b_ref[pl.ds(k1 * 2 * s2 + part * s2 + r * j, r), :] = res[row:row + r]
    for k1 in range(s1):
        rows = pl.ds(k1 * 2 * s2, 2 * s2)
        z_ref[rows, :] = jnp.dot(t2_ref[...], b_ref[rows, :], preferred_element_type=F32).astype(BF16)
    for part, out_ref in ((0, zr_ref), (1, zi_ref)):
        for jp in range(s2 // r):
            g = jnp.concatenate([z_ref[pl.ds(k1 * 2 * s2 + part * s2 + r * jp, r), :]
                                 for k1 in range(s1)], axis=0)
            out_ref[0, pl.ds(s1 * r * jp, s1 * r), :] = jnp.dot(
                p_ref[...], g, preferred_element_type=F32).astype(BF16)


def _pos_dft_two_stage(h3, cast_rest=None, tc=256):
    b, n, d = h3.shape
    tc = min(tc, d)
    nc = d // tc
    t1, t2, perm = _fft_tables(n)
    const = lambda a: pl.BlockSpec(a.shape, lambda bi, ci: (0,) * a.ndim)
    io = pl.BlockSpec((1, n, tc), lambda bi, ci: (bi, 0, ci))
    out = jax.ShapeDtypeStruct((b, n, d), BF16)
    in_specs, out_specs, out_shape, operands = [const(t1), const(t2), const(perm), io], [io, io], \
        [out, out], [t1, t2, perm, h3]
    if cast_rest is not None:
        slabs, wr, wc = cast_rest.shape
        units = wr // BF16_SUBLANES
        per_slab = min(s for s in range(1, units + 1)
                       if units % s == 0 and (slabs - 1) * (units // s) <= b * nc)
        blocks, rb = units // per_slab, per_slab * BF16_SUBLANES
        nblk = (slabs - 1) * blocks
        blk = lambda bi, ci: jnp.minimum(bi * nc + ci, nblk - 1)
        in_specs.append(pl.BlockSpec((None, rb, wc), lambda bi, ci: (1 + blk(bi, ci) // blocks,
                                                                     blk(bi, ci) % blocks, 0)))
        out_specs.append(pl.BlockSpec((None, rb, wc), lambda bi, ci: (blk(bi, ci) // blocks,
                                                                      blk(bi, ci) % blocks, 0)))
        out_shape.append(jax.ShapeDtypeStruct((slabs - 1, wr, wc), BF16))
        operands.append(cast_rest)
    return pl.pallas_call(
        _fft_kernel,
        grid=(b, nc),
        in_specs=in_specs,
        out_specs=out_specs,
        out_shape=out_shape,
        scratch_shapes=[pltpu.VMEM((2 * n, tc), BF16), pltpu.VMEM((2 * n, tc), BF16)],
        compiler_params=_params("arbitrary", "arbitrary"),
        name="pos_fft",
    )(*operands)


def _chan_kernel(x_ref, zr_ref, zi_ref, tab_ref, w_ref, gt_ref, o_ref, f_ref):
    gd = tab_ref.shape[1]
    tab = tab_ref[...]
    for grp in range(FOURIER_GROUPS):
        cols = slice(grp * gd, (grp + 1) * gd)
        z = jnp.concatenate([zr_ref[:, cols], zi_ref[:, cols]], axis=-1)
        f_ref[:, cols] = jnp.dot(z, tab, preferred_element_type=F32).astype(BF16)
    y = jnp.dot(f_ref[...], w_ref[...], preferred_element_type=F32)
    o_ref[...] = x_ref[...] + gt_ref[0] * y


def _chan_mix(x2, zr2, zi2, tab, w_out, mods, k_gate, tm=512):
    rows, d = x2.shape
    tm = mods.tile(tm, rows)
    return pl.pallas_call(
        _chan_kernel,
        grid=(rows // tm,),
        in_specs=[
            pl.BlockSpec((tm, d), lambda i: (i, 0)),
            pl.BlockSpec((tm, d), lambda i: (i, 0)),
            pl.BlockSpec((tm, d), lambda i: (i, 0)),
            pl.BlockSpec(tab.shape, lambda i: (0, 0)),
            pl.BlockSpec((d, d), lambda i: (0, 0)),
            mods.spec(k_gate, tm, d),
        ],
        out_specs=pl.BlockSpec((tm, d), lambda i: (i, 0)),
        out_shape=jax.ShapeDtypeStruct((rows, d), F32),
        scratch_shapes=[pltpu.VMEM((tm, d), BF16)],
        compiler_params=_params("parallel"),
        name="chan_mix",
    )(x2, zr2, zi2, tab, w_out, mods.table)


def _fourier_layer(x3, norm_g, gi, mods, w_out, chan_tab, cast_rest=None):
    b, n, d = x3.shape
    x2 = x3.reshape(b * n, d)
    h = _modulated(x2, norm_g, gi, mods, 3).reshape(b, n, d)
    rest = None
    if n == FFT_SLABS * FFT_SLABS * FFT_CHUNK:
        zr, zi, *rest = _pos_dft_two_stage(h, cast_rest)
        rest = rest[0] if rest else None
    else:
        zr, zi = _pos_dft(h, *_dft_tables(n))
    if cast_rest is not None and rest is None:
        rest = cast_rest[1:].astype(BF16)
    out = _chan_mix(x2, zr.reshape(b * n, d), zi.reshape(b * n, d), chan_tab, w_out, mods, 5)
    out = out.reshape(b, n, d)
    return out if cast_rest is None else (out, rest)


def _rope_tables(n):
    pos = np.arange(n)
    row = (pos // GRID_W).astype(np.float32)
    col = (pos % GRID_W).astype(np.float32)
    inv_freq = np.float32(ROPE_THETA) ** (-np.arange(ROPE_FREQS, dtype=np.float32) / ROPE_FREQS)
    a_r = row[:, None] * inv_freq
    a_c = col[:, None] * inv_freq
    ang = np.concatenate([a_r, a_r, a_c, a_c], axis=-1).astype(np.float64)
    cos, sin = np.cos(ang), np.sin(ang)
    first_half = (np.arange(HEAD_DIM) % (2 * ROPE_FREQS)) < ROPE_FREQS
    sin_from_upper = np.where(first_half[None, :], -sin, 0.0)
    sin_from_lower = np.where(first_half[None, :], 0.0, sin)
    return _const(cos, F32), _const(sin_from_upper, F32), _const(sin_from_lower, F32)


def _head_norm(t, g):
    ms = jnp.mean(t * t, axis=-1, keepdims=True)
    return t * lax.rsqrt(ms + EPS) * g


def _rope(t, cos, s_up, s_lo):
    up = pltpu.roll(t, HEAD_DIM - ROPE_FREQS, 1)
    lo = pltpu.roll(t, ROPE_FREQS, 1)
    return t * cos + up * s_up + lo * s_lo


def _qkv_kernel(x_ref, g_ref, sh_ref, sc_ref, w_ref, qg_ref, kg_ref, *refs,
                n_q_heads, use_rope, q_scale):
    if use_rope:
        cos, s_up, s_lo = (r[...] for r in refs[:3])
        refs = refs[3:]
    q_ref = refs[0] if n_q_heads else None
    k_ref, v_ref, raw_even_ref, raw_odd_ref = refs[-4:]
    step = pl.program_id(0)
    qk_cols = raw_even_ref.shape[1]

    @pl.when(step == 0)
    def _():
        raw_odd_ref[...] = jnp.zeros_like(raw_odd_ref)

    def run(raw_new_ref, raw_old_ref):
        h = _modulate(x_ref[...], g_ref[0], sh_ref[0], sc_ref[0]).astype(BF16)
        y = jnp.dot(h, w_ref[...], preferred_element_type=F32)
        raw_new_ref[...] = y[:, :qk_cols]
        v_ref[...] = y[:, qk_cols:].astype(BF16)
        for hd in range(n_q_heads + N_KV_HEADS):
            is_q = hd < n_q_heads
            gain = (qg_ref[...] * q_scale) if is_q else kg_ref[...]
            t = _head_norm(raw_old_ref[:, hd * HEAD_DIM:(hd + 1) * HEAD_DIM], gain)
            if use_rope:
                t = _rope(t, cos, s_up, s_lo)
            if is_q:
                q_ref[:, hd * HEAD_DIM:(hd + 1) * HEAD_DIM] = t.astype(BF16)
            else:
                kh = hd - n_q_heads
                k_ref[:, kh * HEAD_DIM:(kh + 1) * HEAD_DIM] = t.astype(BF16)

    @pl.when(step % 2 == 0)
    def _():
        run(raw_even_ref, raw_odd_ref)

    @pl.when(step % 2 == 1)
    def _():
        run(raw_odd_ref, raw_even_ref)


def _qkv_proj(x2, norm_g, gi, mods, k0, w, q_g, k_g, rope, n_q_heads, seq, tm=512):
    rows, d = x2.shape
    tm = mods.tile(tm, rows)
    n_tiles = rows // tm
    kv_dim = N_KV_HEADS * HEAD_DIM
    q_dim = n_q_heads * HEAD_DIM
    width = q_dim + 2 * kv_dim
    col_block, rem = divmod(w.shape[1] - width, width)
    assert rem == 0
    use_rope = rope is not None
    rope = rope if use_rope else ()
    tiles_per_seq = seq // tm
    kern = functools.partial(_qkv_kernel, n_q_heads=n_q_heads, use_rope=use_rope,
                             q_scale=math.log2(math.e) / math.sqrt(HEAD_DIM))
    cur = lambda i: jnp.minimum(i, n_tiles - 1)
    prev = lambda i: jnp.maximum(i - 1, 0)
    head_vec = pl.BlockSpec((1, HEAD_DIM), lambda i: (0, 0))
    rope_spec = pl.BlockSpec((tm, HEAD_DIM), lambda i: (prev(i) % tiles_per_seq, 0))
    out_widths = ([q_dim] if n_q_heads else []) + [kv_dim, kv_dim]
    qk_specs = [pl.BlockSpec((tm, wd), lambda i: (prev(i), 0)) for wd in out_widths[:-1]]
    return pl.pallas_call(
        kern,
        grid=(n_tiles + 1,),
        in_specs=[
            pl.BlockSpec((tm, d), lambda i: (cur(i), 0)),
            pl.BlockSpec((1, 1, d), lambda i: (gi, 0, 0)),
            mods.spec(k0, tm, d, cur), mods.spec(k0 + 1, tm, d, cur),
            pl.BlockSpec((d, width), lambda i: (0, col_block)),
            head_vec, head_vec,
        ] + [rope_spec] * len(rope),
        out_specs=qk_specs + [pl.BlockSpec((tm, kv_dim), lambda i: (cur(i), 0))],
        out_shape=[jax.ShapeDtypeStruct((rows, wd), BF16) for wd in out_widths],
        scratch_shapes=[pltpu.VMEM((tm, q_dim + kv_dim), F32)] * 2,
        compiler_params=_params("arbitrary"),
        name="qkv_proj",
    )(x2, norm_g, mods.table, mods.table, w,
      q_g.reshape(1, HEAD_DIM), k_g.reshape(1, HEAD_DIM), *rope)


def _attn_kernel(q_ref, kc_ref, vc_ref, kl_ref, vl_ref, o_ref, k_ref, v_ref, *, group):
    l = kc_ref.shape[1]

    @pl.when(pl.program_id(2) == 0)
    def _():
        k_ref[:l] = kc_ref[0]
        k_ref[l:] = kl_ref[0]
        v_ref[:l, :HEAD_DIM] = vc_ref[0]
        v_ref[l:, :HEAD_DIM] = vl_ref[0]
        v_ref[:, HEAD_DIM:] = jnp.ones((v_ref.shape[0], HEAD_DIM), BF16)

    nt = (((1,), (1,)), ((), ()))
    k, v = k_ref[...], v_ref[...]
    tq = q_ref.shape[1]
    sub = min(tq, ATTN_SUB_ROWS)
    items = [(r0, sub, gi) for r0 in range(0, tq, sub) for gi in range(group)]
    halve = lambda it: [(it[0], it[1] // 2, it[2]), (it[0] + it[1] // 2, it[1] // 2, it[2])]
    if len(items) > 1:
        items = halve(items[0]) + items[1:-1] + halve(items[-1])
    for r0, nrows, gi in items:
        rows = slice(r0, r0 + nrows)
        cols = slice(gi * HEAD_DIM, (gi + 1) * HEAD_DIM)
        s = lax.dot_general(q_ref[0, rows, cols], k, nt, preferred_element_type=F32)
        p = jnp.exp2(s - jnp.max(s, axis=-1, keepdims=True)).astype(BF16)
        oe = jnp.dot(p, v, preferred_element_type=F32)
        o_ref[0, rows, cols] = (oe[:, :HEAD_DIM] / oe[:, HEAD_DIM:HEAD_DIM + 1]).astype(BF16)


def _attention(q3, kc3, vc3, kl3, vl3, tq=1024):
    b, s, qd = q3.shape
    tq = min(tq, s)
    l = kc3.shape[1]
    group = qd // (N_KV_HEADS * HEAD_DIM)
    gw = group * HEAD_DIM
    kv_spec = lambda n: pl.BlockSpec((1, n, HEAD_DIM), lambda bi, kh, i: (bi, 0, kh))
    return pl.pallas_call(
        functools.partial(_attn_kernel, group=group),
        grid=(b, N_KV_HEADS, s // tq),
        in_specs=[
            pl.BlockSpec((1, tq, gw), lambda bi, kh, i: (bi, i, kh)),
            kv_spec(l), kv_spec(l), kv_spec(s), kv_spec(s),
        ],
        out_specs=pl.BlockSpec((1, tq, gw), lambda bi, kh, i: (bi, i, kh)),
        out_shape=jax.ShapeDtypeStruct((b, s, qd), BF16),
        scratch_shapes=[pltpu.VMEM((l + s, HEAD_DIM), BF16), pltpu.VMEM((l + s, 2 * HEAD_DIM), BF16)],
        compiler_params=_params("parallel", "parallel", "arbitrary"),
        name="attention",
    )(q3, kc3, vc3, kl3, vl3)


def _oproj_kernel(x_ref, a_ref, w_ref, gt_ref, o_ref):
    y = jnp.dot(a_ref[...], w_ref[...], preferred_element_type=F32)
    o_ref[...] = x_ref[...] + gt_ref[0] * y


def _out_proj(x2, a2, w, mods, k_gate, tm=1024):
    rows, d = x2.shape
    tm = mods.tile(tm, rows)
    return pl.pallas_call(
        _oproj_kernel,
        grid=(rows // tm,),
        in_specs=[
            pl.BlockSpec((tm, d), lambda i: (i, 0)),
            pl.BlockSpec((tm, a2.shape[1]), lambda i: (i, 0)),
            pl.BlockSpec(w.shape, lambda i: (0, 0), pipeline_mode=pl.Buffered(1)),
            mods.spec(k_gate, tm, d),
        ],
        out_specs=pl.BlockSpec((tm, d), lambda i: (i, 0)),
        out_shape=jax.ShapeDtypeStruct((rows, d), F32),
        compiler_params=_params("parallel"),
        name="out_proj",
    )(x2, a2, w, mods.table)


def kernel(x, c, ctx, c_ctx, w_ada, b_ada, norm_g, w_ffn_in, w_ffn_out, w_fourier_out, w_qkv,
           q_norm_g, k_norm_g, w_attn_out):
    b, n, d = x.shape
    l = ctx.shape[1]
    depth = w_ada.shape[0]
    assert depth == 2 and b <= 8, "layer 0 = Fourier mixer, layer 1 = attention mixer"
    q_dim = w_attn_out.shape[1]
    n_heads = q_dim // HEAD_DIM

    s16 = jnp.zeros((MOD_ROWS, d), F32).at[:b].set(jax.nn.silu(c)).at[8].set(jax.nn.silu(c_ctx))
    table = _ada(s16.astype(BF16), w_ada, b_ada).reshape(depth * MOD_ROWS * N_MOD, 1, d)

    f = w_ffn_out.shape[2]
    w_in_all = w_ffn_in.reshape(depth * 2, d, 2 * f)
    w_out_all = w_ffn_out.reshape(depth * 2, f, d)
    w_in, w_out_first = w_in_all[0].astype(BF16), w_out_all[:1].astype(BF16)
    ng = norm_g.reshape(depth * 3, 1, d)
    w_four = w_fourier_out.astype(BF16)
    w_qkv_b = w_qkv.astype(BF16)
    w_o = w_attn_out.astype(BF16)

    gd = d // FOURIER_GROUPS
    cos_c, sin_c = _dft_tables(gd)
    chan_tab = jnp.concatenate([cos_c, -sin_c], axis=0)

    xl = x.reshape(b * n, d)
    xc = ctx.reshape(b * l, d)

    ml = _Mods(table, 0, n, ctx=False)
    mc = _Mods(table, 0, l, ctx=True)
    xc = _ffn(xc, ng, 0, mc, 0, w_in, w_out_first, 0)
    xl, w_in_next = _ffn(xl, ng, 0, ml, 0, w_in, w_out_first, 0, cast_next=(w_in_all, 1))
    xl, w_out_rest = _fourier_layer(xl.reshape(b, n, d), ng, 1, ml, w_four[0], chan_tab,
                                    cast_rest=w_out_all)
    xl = xl.reshape(b * n, d)
    xc = _fourier_layer(xc.reshape(b, l, d), ng, 1, mc, w_four[0], chan_tab).reshape(b * l, d)
    w_in = w_in_next
    xc = _ffn(xc, ng, 2, mc, 6, w_in, w_out_rest, 0)
    xl, w_in_next = _ffn(xl, ng, 2, ml, 6, w_in, w_out_rest, 0, cast_next=(w_in_all, 2))

    ml = _Mods(table, 1, n, ctx=False)
    mc = _Mods(table, 1, l, ctx=True)
    w_in = w_in_next
    xc = _ffn(xc, ng, 3, mc, 0, w_in, w_out_rest, 1)
    xl, w_in_next = _ffn(xl, ng, 3, ml, 0, w_in, w_out_rest, 1, cast_next=(w_in_all, 3))
    rope = _rope_tables(n)
    q, kl, vl = _qkv_proj(xl, ng, 4, ml, 3, w_qkv_b[0], q_norm_g[0], k_norm_g[0], rope, n_heads, n)
    kc, vc = _qkv_proj(xc, ng, 4, mc, 3, w_qkv_b[0], q_norm_g[0], k_norm_g[0], None, 0, l)
    kv_dim = N_KV_HEADS * HEAD_DIM
    o = _attention(q.reshape(b, n, q_dim), kc.reshape(b, l, kv_dim), vc.reshape(b, l, kv_dim),
                   kl.reshape(b, n, kv_dim), vl.reshape(b, n, kv_dim))
    xl = _out_proj(xl, o.reshape(b * n, q_dim), w_o[0], ml, 5)
    xl = _ffn(xl, ng, 5, ml, 6, w_in_next, w_out_rest, 2)
    return xl.reshape(b, n, d)
```
